```python
import math
import jax
import jax.numpy as jnp
from jax import lax
import numpy as np

D_MODEL = 1024
BATCH = 8
SEQ = 2048
DEPTH = 2
DEC_BATCH = 128
DEC_SEQ = 4
PAST_LEN = 2048
PAGE_SIZE = 128

A_HEADS = 4
A_DK = 64
A_DV = 64
A_WIDTH = A_HEADS * A_DV
B_HEADS = 8
B_HD = 64
B_WIDTH = B_HEADS * B_HD
C_HEADS = 4
C_DK = 64
C_DV = 64
C_WIDTH = C_HEADS * C_DV
CONV_W = 4
C_CONV_CH = C_HEADS * (2 * C_DK + C_DV)
D_MIX = A_WIDTH + B_WIDTH + C_WIDTH
LIN_CHUNK = 64
Q_BLOCK = 128
N_GROUPS = 4
EXPERTS_PER_GROUP = 8
N_EXPERTS = N_GROUPS * EXPERTS_PER_GROUP
TOP_K = 2
D_EXPERT = D_MODEL // 4
EPS = 1e-6
IN_SIZES = (A_HEADS * A_DK, A_HEADS * A_DK, A_WIDTH, A_WIDTH,
            B_WIDTH, B_WIDTH, B_WIDTH, B_HEADS,
            C_CONV_CH, C_WIDTH, C_HEADS, C_HEADS)
IN_COLS = sum(IN_SIZES)
IN_SPLITS = tuple(int(s) for s in np.cumsum(IN_SIZES)[:-1])

kernel_name = 'hybrid_hgrn2_fox_gdn_hmoe_step'


def _rmsnorm(x, g):
    xf = x.astype(jnp.float32)
    y = xf * lax.rsqrt(jnp.mean(xf * xf, axis=-1, keepdims=True) + EPS)
    return (y * g.astype(jnp.float32)).astype(x.dtype)


def _l2norm(x):
    return x * lax.rsqrt(jnp.sum(x * x, axis=-1, keepdims=True) + EPS)


def _chunk_len(n):
    return LIN_CHUNK if n % LIN_CHUNK == 0 else n


def _to_chunks(a, c):
    b, h, n = a.shape[:3]
    return jnp.moveaxis(a.reshape(b, h, n // c, c, *a.shape[3:]), 2, 0)


def _from_chunks(o):
    n, b, h, c, d = o.shape
    return jnp.moveaxis(o, 0, 2).reshape(b, h, n * c, d)


def _gla_chunked(q, k, v, log_f, s0):
    c = _chunk_len(q.shape[2])
    incl = jnp.tril(jnp.ones((c, c), dtype=bool))

    def step(s, inp):
        qc, kc, vc, lc = inp
        g = jnp.cumsum(lc, axis=2)
        diff = g[:, :, :, None, :] - g[:, :, None, :, :]
        decay = jnp.exp(jnp.where(incl[:, :, None], diff, -jnp.inf))
        att = jnp.einsum('bhtsk,bhsk->bhts', qc[:, :, :, None, :] * decay, kc)
        o = (jnp.einsum('bhtk,bhkv->bhtv', qc * jnp.exp(g), s)
             + jnp.einsum('bhts,bhsv->bhtv', att, vc))
        g_end = g[:, :, -1:, :]
        s = (jnp.exp(g_end[:, :, 0, :, None]) * s
             + jnp.einsum('bhsk,bhsv->bhkv', kc * jnp.exp(g_end - g), vc))
        return s, o

    s, o = lax.scan(step, s0, (_to_chunks(q, c), _to_chunks(k, c), _to_chunks(v, c), _to_chunks(log_f, c)))
    return _from_chunks(o), s


def _gated_delta_chunked(q, k, v, g_log, beta, s0):
    c = _chunk_len(q.shape[2])
    dv = v.shape[-1]
    incl = jnp.tril(jnp.ones((c, c), dtype=bool))
    strict = jnp.tril(jnp.ones((c, c), dtype=bool), -1)

    def step(s, inp):
        qc, kc, vc, gc, bc = inp
        g = jnp.cumsum(gc, axis=-1)
        decay = jnp.exp(jnp.where(incl, g[..., :, None] - g[..., None, :], -jnp.inf))
        kk = jnp.einsum('bhtk,bhsk->bhts', kc, kc)
        lower = jnp.where(strict, bc[..., :, None] * kk * decay, 0.0)
        rhs = jnp.concatenate([vc * bc[..., None], kc * (bc * jnp.exp(g))[..., None]], axis=-1)
        sol = lax.linalg.triangular_solve(lower, rhs, left_side=True, lower=True, unit_diagonal=True)
        u, w = sol[..., :dv], sol[..., dv:]
        v_new = u - jnp.einsum('bhtk,bhkv->bhtv', w, s)
        qk = jnp.einsum('bhtk,bhsk->bhts', qc, kc) * decay
        o = (jnp.einsum('bhtk,bhkv->bhtv', qc * jnp.exp(g)[..., None], s)
             + jnp.einsum('bhts,bhsv->bhtv', qk, v_new))
        g_end = g[..., -1]
        s = (jnp.exp(g_end)[..., None, None] * s
             + jnp.einsum('bhsk,bhsv->bhkv', kc * jnp.exp(g_end[..., None] - g)[..., None], v_new))
        return s, o

    s, o = lax.scan(step, s0, (_to_chunks(q, c), _to_chunks(k, c), _to_chunks(v, c),
                               _to_chunks(g_log, c), _to_chunks(beta, c)))
    return _from_chunks(o), s


def _hgrn2(q_raw, f_raw, i_raw, g_raw, lb, norm_g, s0):
    b, n, _ = q_raw.shape
    dt = q_raw.dtype
    f_logit = f_raw.astype(jnp.float32)
    lb = lb.astype(jnp.float32)
    log_f = jnp.log(lb + (1.0 - lb) * jax.nn.sigmoid(f_logit))
    key = (1.0 - lb) * jax.nn.sigmoid(-f_logit)
    q = jax.nn.silu(q_raw.astype(jnp.float32)) * (A_DK ** -0.5)

    def heads(a, d):
        return a.reshape(b, n, A_HEADS, d).transpose(0, 2, 1, 3)

    o, s = _gla_chunked(heads(q, A_DK), heads(key, A_DK), heads(i_raw.astype(jnp.float32), A_DV),
                        heads(log_f, A_DK), s0)
    o = _rmsnorm(o.transpose(0, 2, 1, 3), norm_g) * jax.nn.silu(g_raw.astype(jnp.float32)).reshape(b, n, A_HEADS, A_DV)
    return o.reshape(b, n, A_WIDTH).astype(dt), s


def _gdn(qkv_raw, z_raw, a_raw, b_raw, conv_w, a_log, dt_bias, norm_g, conv_buf, s0):
    b, n, _ = qkv_raw.shape
    dt = qkv_raw.dtype
    xpad = jnp.concatenate([conv_buf.astype(dt), qkv_raw], axis=1)
    conv = lax.conv_general_dilated(xpad, conv_w.astype(dt)[:, None, :], window_strides=(1,),
                                    padding='VALID', dimension_numbers=('NWC', 'WIO', 'NWC'),
                                    feature_group_count=C_CONV_CH)
    act = jax.nn.silu(conv.astype(jnp.float32))
    q, k, v = jnp.split(act, (C_HEADS * C_DK, 2 * C_HEADS * C_DK), axis=-1)

    def heads(a, d):
        return a.reshape(b, n, C_HEADS, d).transpose(0, 2, 1, 3)

    q = _l2norm(heads(q, C_DK)) * (C_DK ** -0.5)
    k = _l2norm(heads(k, C_DK))
    v = heads(v, C_DV)
    beta = jax.nn.sigmoid(b_raw.astype(jnp.float32)).transpose(0, 2, 1)
    g_log = (-jnp.exp(a_log.astype(jnp.float32))
             * jax.nn.softplus(a_raw.astype(jnp.float32) + dt_bias.astype(jnp.float32))).transpose(0, 2, 1)
    o, s = _gated_delta_chunked(q, k, v, g_log, beta, s0)
    o = _rmsnorm(o.transpose(0, 2, 1, 3), norm_g) * jax.nn.silu(z_raw.astype(jnp.float32)).reshape(b, n, C_HEADS, C_DV)
    return o.reshape(b, n, C_WIDTH).astype(dt), s, xpad[:, -(CONV_W - 1):]


def _fox_attend(q, k, v, c, offset):
    t_len = q.shape[1]
    outs = []
    for start in range(0, t_len, Q_BLOCK):
        end = min(start + Q_BLOCK, t_len)
        k_end = offset + end
        q_pos = offset + jnp.arange(start, end)
        k_pos = jnp.arange(k_end)
        s = jnp.einsum('bqhd,bkhd->bhqk', q[:, start:end], k[:, :k_end], preferred_element_type=jnp.float32)
        cq = c[:, offset + start:offset + end].transpose(0, 2, 1)
        ck = c[:, :k_end].transpose(0, 2, 1)
        s = s + cq[:, :, :, None] - ck[:, :, None, :]
        s = jnp.where(k_pos[None, :] <= q_pos[:, None], s, -jnp.inf)
        p = jax.nn.softmax(s, axis=-1)
        outs.append(jnp.einsum('bhqk,bkhd->bqhd', p.astype(v.dtype), v[:, :k_end]))
    return jnp.concatenate(outs, axis=1)


def _hier_moe(h, w_group, b_group, w_router, b_router, w_gate, w_up, w_down):
    t = h.shape[0]
    grp_prob = jax.nn.softmax((h @ w_group).astype(jnp.float32) + b_group.astype(jnp.float32), axis=-1)
    top_gp, top_g = lax.top_k(grp_prob, 1)
    e_logits = ((h @ w_router).astype(jnp.float32) + b_router.astype(jnp.float32)).reshape(t, N_GROUPS, EXPERTS_PER_GROUP)
    in_grp = jnp.take_along_axis(e_logits, top_g[:, :, None], axis=1)[:, 0]
    top_ep, top_e = lax.top_k(jax.nn.softmax(in_grp, axis=-1), TOP_K)
    weight = top_gp * top_ep / jnp.sum(top_ep, axis=-1, keepdims=True)
    eid = top_g * EXPERTS_PER_GROUP + top_e
    gates = jnp.zeros((t, N_EXPERTS), jnp.float32).at[jnp.arange(t)[:, None], eid].add(weight)
    y = jnp.zeros(h.shape, h.dtype)
    for gi in range(N_GROUPS):
        sl = slice(gi * EXPERTS_PER_GROUP, (gi + 1) * EXPERTS_PER_GROUP)
        a = jax.nn.silu(jnp.einsum('td,edf->tef', h, w_gate[sl])) * jnp.einsum('td,edf->tef', h, w_up[sl])
        y = y + jnp.einsum('tef,efd->td', a * gates[:, sl, None].astype(a.dtype), w_down[sl])
    return y


def _layer(x, p, lb, past):
    bsz, n, _ = x.shape
    dt = x.dtype
    hn = _rmsnorm(x, p['ln_mix'])
    (a_q, a_f, a_i, a_g, b_q, b_k, b_v, b_f,
     c_qkv, c_z, c_a, c_b) = jnp.split(hn @ p['w_in'], IN_SPLITS, axis=-1)
    if past is None:
        s_hgrn0 = jnp.zeros((bsz, A_HEADS, A_DK, A_DV), jnp.float32)
        s_gdn0 = jnp.zeros((bsz, C_HEADS, C_DK, C_DV), jnp.float32)
        conv0 = jnp.zeros((bsz, CONV_W - 1, C_CONV_CH), dt)
    else:
        s_hgrn0 = past['hgrn'].astype(jnp.float32)
        s_gdn0 = past['gdn'].astype(jnp.float32)
        conv0 = past['conv']
    out_a, s_hgrn = _hgrn2(a_q, a_f, a_i, a_g, lb, p['hgrn_norm'], s_hgrn0)
    qb = _rmsnorm(b_q.reshape(bsz, n, B_HEADS, B_HD), p['fox_qnorm']) * (B_HD ** -0.5)
    kb = _rmsnorm(b_k.reshape(bsz, n, B_HEADS, B_HD), p['fox_knorm'])
    vb = b_v.reshape(bsz, n, B_HEADS, B_HD)
    log_f = jax.nn.log_sigmoid(b_f.astype(jnp.float32) + p['fox_bf'].astype(jnp.float32))
    if past is None:
        out_b = _fox_attend(qb, kb, vb, jnp.cumsum(log_f, axis=1), 0)
    else:
        n_past = past['k'].shape[1]
        k_all = jnp.concatenate([past['k'].astype(dt), kb], axis=1)
        v_all = jnp.concatenate([past['v'].astype(dt), vb], axis=1)
        c_all = jnp.cumsum(jnp.concatenate([past['logf'].astype(jnp.float32), log_f], axis=1), axis=1)
        out_b = _fox_attend(qb, k_all, v_all, c_all, n_past)
    out_c, s_gdn, conv_new = _gdn(c_qkv, c_z, c_a, c_b, p['gdn_conv'], p['gdn_a_log'], p['gdn_dt_bias'],
                                  p['gdn_norm'], conv0, s_gdn0)
    mix = jnp.concatenate([out_a, out_b.reshape(bsz, n, B_WIDTH), out_c], axis=-1)
    x = x + mix @ p['w_out']
    hf = _rmsnorm(x, p['ln_ffn']).reshape(bsz * n, D_MODEL)
    x = x + _hier_moe(hf, p['w_group'], p['b_group'], p['w_router'], p['b_router'],
                      p['w_gate'], p['w_up'], p['w_down']).reshape(bsz, n, D_MODEL)
    return x, (kb, vb, log_f, s_hgrn, s_gdn, conv_new)


def _stack(states, i):
    return jnp.stack([st[i] for st in states], axis=0)


def setup_inputs(seed: int = 0) -> dict:
    key = jax.random.key(seed)
    ks = jax.random.split(key, 32)

    def nrm(i, shape, scale):
        return scale * jax.random.normal(ks[i], shape, jnp.float32)

    n_pages = PAST_LEN // PAGE_SIZE
    n_phys = (5 * DEC_BATCH * n_pages) // 4
    page_table = jax.random.permutation(ks[0], n_phys)[:DEC_BATCH * n_pages].reshape(DEC_BATCH, n_pages).astype(jnp.int32)
    dt_init = jnp.exp(jax.random.uniform(ks[18], (DEPTH, C_HEADS), jnp.float32, math.log(1e-3), math.log(1e-1)))
    return {
        'x_prompt': nrm(1, (BATCH, SEQ, D_MODEL), 1.0),
        'x_sample': nrm(2, (DEC_BATCH, DEC_SEQ, D_MODEL), 1.0),
        'cache_k': nrm(3, (DEPTH, n_phys, PAGE_SIZE, B_HEADS, B_HD), 1.0),
        'cache_v': nrm(4, (DEPTH, n_phys, PAGE_SIZE, B_HEADS, B_HD), 1.0),
        'cache_logf': jax.nn.log_sigmoid(2.0 + nrm(5, (DEPTH, n_phys, PAGE_SIZE, B_HEADS), 0.5)),
        'page_table': page_table,
        'state_hgrn': nrm(6, (DEPTH, DEC_BATCH, A_HEADS, A_DK, A_DV), 0.5),
        'state_gdn': nrm(7, (DEPTH, DEC_BATCH, C_HEADS, C_DK, C_DV), 0.3),
        'state_conv': nrm(8, (DEPTH, DEC_BATCH, CONV_W - 1, C_CONV_CH), 1.0),
        'ln_mix': 1.0 + nrm(9, (DEPTH, D_MODEL), 0.02),
        'w_in': nrm(10, (DEPTH, D_MODEL, IN_COLS), D_MODEL ** -0.5),
        'hgrn_lb': nrm(11, (DEPTH, A_HEADS * A_DK), 1.0),
        'hgrn_norm': 1.0 + nrm(12, (DEPTH, A_DV), 0.02),
        'fox_bf': 2.0 + nrm(13, (DEPTH, B_HEADS), 0.5),
        'fox_qnorm': 1.0 + nrm(14, (DEPTH, B_HD), 0.02),
        'fox_knorm': 1.0 + nrm(15, (DEPTH, B_HD), 0.02),
        'gdn_conv': nrm(16, (DEPTH, CONV_W, C_CONV_CH), CONV_W ** -0.5),
        'gdn_a_log': jnp.log(jax.random.uniform(ks[17], (DEPTH, C_HEADS), jnp.float32, 1.0, 16.0)),
        'gdn_dt_bias': dt_init + jnp.log(-jnp.expm1(-dt_init)),
        'gdn_norm': 1.0 + nrm(19, (DEPTH, C_DV), 0.02),
        'w_out': nrm(21, (DEPTH, D_MIX, D_MODEL), D_MIX ** -0.5),
        'ln_ffn': 1.0 + nrm(22, (DEPTH, D_MODEL), 0.02),
        'w_group': nrm(23, (DEPTH, D_MODEL, N_GROUPS), D_MODEL ** -0.5),
        'b_group': nrm(24, (DEPTH, N_GROUPS), 0.01),
        'w_router': nrm(25, (DEPTH, D_MODEL, N_EXPERTS), D_MODEL ** -0.5),
        'b_router': nrm(26, (DEPTH, N_EXPERTS), 0.01),
        'w_gate': nrm(27, (DEPTH, N_EXPERTS, D_MODEL, D_EXPERT), D_MODEL ** -0.5),
        'w_up': nrm(28, (DEPTH, N_EXPERTS, D_MODEL, D_EXPERT), D_MODEL ** -0.5),
        'w_down': nrm(29, (DEPTH, N_EXPERTS, D_EXPERT, D_MODEL), D_EXPERT ** -0.5),
    }


def reference(x_prompt, x_sample, cache_k, cache_v, cache_logf, page_table, state_hgrn, state_gdn, state_conv,
              ln_mix, w_in, hgrn_lb, hgrn_norm, fox_bf, fox_qnorm, fox_knorm, gdn_conv, gdn_a_log, gdn_dt_bias,
              gdn_norm, w_out, ln_ffn, w_group, b_group, w_router, b_router, w_gate, w_up, w_down):
    lb_prob = jax.nn.softmax(hgrn_lb.astype(jnp.float32), axis=0)
    lower_bound = jnp.cumsum(lb_prob, axis=0) - lb_prob[0]
    n_seq, n_pages = page_table.shape
    page = cache_k.shape[2]
    y_p, y_s = x_prompt, x_sample
    st_p, st_s = [], []
    for l in range(DEPTH):
        p = {'ln_mix': ln_mix[l], 'w_in': w_in[l], 'hgrn_norm': hgrn_norm[l], 'fox_bf': fox_bf[l],
             'fox_qnorm': fox_qnorm[l], 'fox_knorm': fox_knorm[l], 'gdn_conv': gdn_conv[l],
             'gdn_a_log': gdn_a_log[l], 'gdn_dt_bias': gdn_dt_bias[l], 'gdn_norm': gdn_norm[l],
             'w_out': w_out[l], 'ln_ffn': ln_ffn[l], 'w_group': w_group[l], 'b_group': b_group[l],
             'w_router': w_router[l], 'b_router': b_router[l], 'w_gate': w_gate[l], 'w_up': w_up[l],
             'w_down': w_down[l]}
        y_p, new_p = _layer(y_p, p, lower_bound[l], None)
        past = {'k': jnp.take(cache_k[l], page_table, axis=0).reshape(n_seq, n_pages * page, B_HEADS, B_HD),
                'v': jnp.take(cache_v[l], page_table, axis=0).reshape(n_seq, n_pages * page, B_HEADS, B_HD),
                'logf': jnp.take(cache_logf[l], page_table, axis=0).reshape(n_seq, n_pages * page, B_HEADS),
                'hgrn': state_hgrn[l], 'gdn': state_gdn[l], 'conv': state_conv[l]}
        y_s, new_s = _layer(y_s, p, lower_bound[l], past)
        st_p.append(new_p)
        st_s.append(new_s)
    return (y_p, y_s,
            _stack(st_p, 0), _stack(st_p, 1), _stack(st_p, 2), _stack(st_p, 3), _stack(st_p, 4), _stack(st_p, 5),
            _stack(st_s, 0), _stack(st_s, 1), _stack(st_s, 2), _stack(st_s, 3), _stack(st_s, 4), _stack(st_s, 5))
```

```python
import functools
import math

import numpy as np
import jax
import jax.numpy as jnp
from jax import lax
from jax.experimental import pallas as pl
from jax.experimental.pallas import tpu as pltpu

F32 = jnp.float32
BF16 = jnp.bfloat16
EPS = 1e-6
NEG = -1e30

D_MODEL = 1024
HD = 64
A_HEADS = 4
B_HEADS = 8
C_HEADS = 4
CONV_W = 4
N_GROUPS = 4
EXPERTS_PER_GROUP = 8
N_EXPERTS = N_GROUPS * EXPERTS_PER_GROUP
D_EXPERT = D_MODEL // 4
SUB = 16
LANES = 128
VMEM_LIMIT = 56 * 1024 * 1024


def _cp(*sem):
    return pltpu.CompilerParams(dimension_semantics=sem, vmem_limit_bytes=VMEM_LIMIT)


def _dot(a, b):
    return jnp.dot(a, b, preferred_element_type=F32)


def _dot_nt(a, b):
    return lax.dot_general(a, b, (((1,), (1,)), ((), ())), preferred_element_type=F32)


def _dot_tn(a, b):
    return lax.dot_general(a, b, (((0,), (0,)), ((), ())), preferred_element_type=F32)


def _split3(x):
    hi = x.astype(BF16)
    r = x - hi.astype(F32)
    mid = r.astype(BF16)
    lo = (r - mid.astype(F32)).astype(BF16)
    return hi, mid, lo


def _exact_left(m, x):
    hi, mid, lo = _split3(x)
    return (_dot(m, hi) + _dot(m, mid)) + _dot(m, lo)


def _exact_right(x, m):
    hi, mid, lo = _split3(x)
    return (_dot(hi, m) + _dot(mid, m)) + _dot(lo, m)


def _idiv(x, n):
    return jnp.right_shift(x, int(math.log2(n)))


def _imod(x, n):
    return jnp.bitwise_and(x, n - 1)


def _sigmoid(x):
    return 1.0 / (1.0 + jnp.exp(-x))


def _softplus(z):
    return jnp.maximum(z, 0.0) + jnp.log(1.0 + jnp.exp(-jnp.abs(z)))


def _head_ssq(z, hs):
    zz = z * z
    hi = zz.astype(BF16)
    lo = (zz - hi.astype(F32)).astype(BF16)
    return _dot(hi, hs) + _dot(lo, hs)


def _const(a, dtype=BF16):
    return jnp.asarray(a, dtype=dtype)


def _head_sum_matrix(width):
    i = np.arange(width)
    return (i[:, None] // HD == i[None, :] // HD).astype(np.float32)


def _chunk_tril(n, c):
    i = np.arange(n)
    return ((i[:, None] // c == i[None, :] // c) & (i[None, :] <= i[:, None])).astype(np.float32)


PROJ_COLS = 3712


def _proj_kernel(x_ref, ln_ref, w_ref, hs_ref, qg_ref, kg_ref, p1_ref, p2_ref,
                 a_ref, bq_ref, bk_ref, bv_ref, c_ref, g_ref):
    x = x_ref[...]
    ms = jnp.mean(x * x, axis=-1, keepdims=True)
    hn = ((x * lax.rsqrt(ms + EPS)) * ln_ref[...]).astype(BF16)
    a_ref[...] = _dot(hn, w_ref[:, 0:1024])
    hs = hs_ref[...]
    q = _dot(hn, w_ref[:, 1024:1536])
    bq_ref[...] = ((q * lax.rsqrt(_head_ssq(q, hs) * (1.0 / HD) + EPS)) * qg_ref[...]) * (HD ** -0.5)
    k = _dot(hn, w_ref[:, 1536:2048])
    bk_ref[...] = (k * lax.rsqrt(_head_ssq(k, hs) * (1.0 / HD) + EPS)) * kg_ref[...]
    bv_ref[...] = _dot(hn, w_ref[:, 2048:2560])
    c_ref[...] = _dot(hn, w_ref[:, 2560:3584])
    gr = _dot(hn, w_ref[:, 3584:3712])
    lane = lax.broadcasted_iota(jnp.int32, gr.shape, 1)
    z = gr + p1_ref[...]
    sp = _softplus(z)
    logsig = jnp.minimum(z, 0.0) - jnp.log(1.0 + jnp.exp(-jnp.abs(z)))
    glog = -jnp.exp(p2_ref[...]) * sp
    beta = _sigmoid(gr)
    g_ref[...] = jnp.where(lane < 8, logsig,
                           jnp.where(lane < 12, glog, jnp.where(lane < 16, beta, 0.0)))


def _proj(x, ln, w, hs512, qg, kg, p1, p2, tm):
    t = x.shape[0]
    row = lambda i: (i, 0)
    fix = lambda i: (0, 0)
    outs = [jax.ShapeDtypeStruct((t, n), F32) for n in (1024, 512, 512, 512, 1024, 128)]
    return pl.pallas_call(
        _proj_kernel,
        grid=(t // tm,),
        in_specs=[pl.BlockSpec((tm, D_MODEL), row), pl.BlockSpec((1, D_MODEL), fix),
                  pl.BlockSpec((D_MODEL, PROJ_COLS), fix), pl.BlockSpec((512, 512), fix),
                  pl.BlockSpec((1, 512), fix), pl.BlockSpec((1, 512), fix),
                  pl.BlockSpec((1, LANES), fix), pl.BlockSpec((1, LANES), fix)],
        out_specs=[pl.BlockSpec((tm, n), row) for n in (1024, 512, 512, 512, 1024, 128)],
        out_shape=outs,
        compiler_params=_cp("arbitrary"),
        name="proj",
    )(x, ln, w, hs512, qg, kg, p1, p2)


def _foxprep_kernel(bq_ref, bk_ref, g_ref, tril_ref, pq_ref, pcq_ref, pck_ref, oq_ref, ok_ref,
                    qa_ref, ka_ref, carry_ref):
    i = pl.program_id(1)

    @pl.when(i == 0)
    def _():
        carry_ref[...] = jnp.zeros_like(carry_ref)

    g = g_ref[...]
    c = _exact_left(tril_ref[...], g) + carry_ref[...]
    carry_ref[...] = c[-1:, :]
    hi, mid, lo = _split3(c)
    pq = pq_ref[...]
    qa = _dot(bq_ref[...].astype(BF16), pq) + oq_ref[...]
    ka = _dot(bk_ref[...].astype(BF16), pq) + ok_ref[...]
    for j, part in enumerate((hi, mid, lo)):
        qa = qa + _dot(part, pcq_ref[j])
        ka = ka - _dot(part, pck_ref[j])
    qa_ref[...] = qa.astype(BF16)
    ka_ref[...] = ka.astype(BF16)


def _foxprep(bq, bk, g, nb, tm):
    t = bq.shape[0]
    nblk = t // nb // tm
    pq = np.zeros((512, 1024), np.float32)
    for h in range(B_HEADS):
        pq[h * HD + np.arange(HD), h * LANES + np.arange(HD)] = 1.0
    pcq = np.zeros((3, LANES, 1024), np.float32)
    pck = np.zeros((3, LANES, 1024), np.float32)
    oq = np.zeros((1, 1024), np.float32)
    ok = np.zeros((1, 1024), np.float32)
    for h in range(B_HEADS):
        for j in range(3):
            pcq[j, h, h * LANES + HD + j] = 1.0
            pck[j, h, h * LANES + HD + 3 + j] = 1.0
            oq[0, h * LANES + HD + 3 + j] = 1.0
            ok[0, h * LANES + HD + j] = 1.0
    row = lambda b, i: (b * nblk + i, 0)
    fix2 = lambda b, i: (0, 0)
    fix3 = lambda b, i: (0, 0, 0)
    return pl.pallas_call(
        _foxprep_kernel,
        grid=(nb, nblk),
        in_specs=[pl.BlockSpec((tm, 512), row), pl.BlockSpec((tm, 512), row),
                  pl.BlockSpec((tm, LANES), row), pl.BlockSpec((tm, tm), fix2),
                  pl.BlockSpec((512, 1024), fix2), pl.BlockSpec((3, LANES, 1024), fix3),
                  pl.BlockSpec((3, LANES, 1024), fix3), pl.BlockSpec((1, 1024), fix2),
                  pl.BlockSpec((1, 1024), fix2)],
        out_specs=[pl.BlockSpec((tm, 1024), row), pl.BlockSpec((tm, 1024), row)],
        out_shape=[jax.ShapeDtypeStruct((t, 1024), BF16)] * 2,
        scratch_shapes=[pltpu.VMEM((1, LANES), F32)],
        compiler_params=_cp("arbitrary", "arbitrary"),
        name="foxprep",
    )(bq, bk, g, _const(_chunk_tril(tm, tm)), _const(pq), _const(pcq), _const(pck),
      _const(oq, F32), _const(ok, F32))


def _fox_kernel(q_ref, k_ref, v_ref, o_ref, m_ref, l_ref, acc_ref, *, tq):
    qi = pl.program_id(1)
    ki = pl.program_id(2)

    @pl.when(ki == 0)
    def _():
        m_ref[...] = jnp.full_like(m_ref, NEG)
        l_ref[...] = jnp.zeros_like(l_ref)
        acc_ref[...] = jnp.zeros_like(acc_ref)

    @pl.when(ki <= qi)
    def _():
        rows = lax.broadcasted_iota(jnp.int32, (tq, tq), 0)
        cols = lax.broadcasted_iota(jnp.int32, (tq, tq), 1)
        keep = (cols <= rows) | (ki < qi)
        lane = lax.broadcasted_iota(jnp.int32, (tq, LANES), 1)
        left = lane < HD
        for pr in range(B_HEADS // 2):
            vp = v_ref[:, pr * LANES:(pr + 1) * LANES]
            upd = None
            alphas = []
            for hh in range(2):
                h = 2 * pr + hh
                s = _dot_nt(q_ref[:, h * LANES:(h + 1) * LANES], k_ref[:, h * LANES:(h + 1) * LANES])
                s = jnp.where(keep, s, NEG)
                m_old = m_ref[h]
                m_new = jnp.maximum(m_old, jnp.max(s, axis=1, keepdims=True))
                p = jnp.exp(s - m_new[:, 0:1])
                alpha = jnp.exp(m_old - m_new)
                l_ref[h] = alpha * l_ref[h] + jnp.sum(p, axis=1, keepdims=True)
                m_ref[h] = m_new
                vm = jnp.where((lane < HD) == (hh == 0), vp, 0.0).astype(BF16)
                pv = _dot(p.astype(BF16), vm)
                upd = pv if upd is None else upd + pv
                alphas.append(alpha)
            a = jnp.where(left, alphas[0], alphas[1])
            acc_ref[:, pr * LANES:(pr + 1) * LANES] = a * acc_ref[:, pr * LANES:(pr + 1) * LANES] + upd

    @pl.when(ki == qi)
    def _():
        lane = lax.broadcasted_iota(jnp.int32, (tq, LANES), 1)
        for pr in range(B_HEADS // 2):
            l = jnp.where(lane < HD, l_ref[2 * pr], l_ref[2 * pr + 1])
            o_ref[:, pr * LANES:(pr + 1) * LANES] = acc_ref[:, pr * LANES:(pr + 1) * LANES] / l


def _fox(qa, ka, bv, nb, tq):
    t = qa.shape[0]
    nblk = t // nb // tq
    return pl.pallas_call(
        functools.partial(_fox_kernel, tq=tq),
        grid=(nb, nblk, nblk),
        in_specs=[pl.BlockSpec((tq, 1024), lambda b, qi, ki: (b * nblk + qi, 0)),
                  pl.BlockSpec((tq, 1024), lambda b, qi, ki: (b * nblk + jnp.minimum(ki, qi), 0)),
                  pl.BlockSpec((tq, 512), lambda b, qi, ki: (b * nblk + jnp.minimum(ki, qi), 0))],
        out_specs=pl.BlockSpec((tq, 512), lambda b, qi, ki: (b * nblk + qi, 0)),
        out_shape=jax.ShapeDtypeStruct((t, 512), F32),
        scratch_shapes=[pltpu.VMEM((B_HEADS, tq, LANES), F32), pltpu.VMEM((B_HEADS, tq, LANES), F32),
                        pltpu.VMEM((tq, 512), F32)],
        compiler_params=_cp("arbitrary", "arbitrary", "arbitrary"),
        name="fox",
    )(qa, ka, bv)


def _foxdec_kernel(pt_ref, q_ref, kn_ref, vn_ref, lfn_ref, triu_ref, ones_ref, *rest, pp, dec):
    kp = rest[0:pp]
    vp = rest[pp:2 * pp]
    lp = rest[2 * pp:3 * pp]
    o_ref = rest[3 * pp]
    qr_ref, m_ref, l_ref, r_ref, acc_ref = rest[3 * pp + 1:]
    step = pl.program_id(1)
    nrow = dec * B_HEADS
    rowi = lax.broadcasted_iota(jnp.int32, (nrow, 512), 0)
    coli = lax.broadcasted_iota(jnp.int32, (nrow, 512), 1)
    bd = _idiv(coli, HD) == _imod(rowi, B_HEADS)

    @pl.when(step == 0)
    def _():
        q = q_ref[0]
        qrows = jnp.concatenate(
            [jnp.broadcast_to(q[t:t + 1, :], (B_HEADS, 512)) for t in range(dec)], axis=0)
        qr_ref[...] = jnp.where(bd, qrows, 0.0).astype(BF16)
        m_ref[...] = jnp.full_like(m_ref, NEG)
        l_ref[...] = jnp.zeros_like(l_ref)
        r_ref[...] = jnp.zeros_like(r_ref)
        acc_ref[...] = jnp.zeros_like(acc_ref)

    def page(kpage, vpage, lft, causal):
        s = _dot(qr_ref[...], kpage.astype(BF16))
        w8 = _exact_right(lft, triu_ref[...])
        tot8 = _exact_right(lft, ones_ref[...])
        w = jnp.concatenate([w8] * dec, axis=0)
        tot = jnp.concatenate([tot8] * dec, axis=0)
        logit = s - (r_ref[...] + w)
        if causal:
            key = lax.broadcasted_iota(jnp.int32, (nrow, LANES), 1)
            trow = _idiv(lax.broadcasted_iota(jnp.int32, (nrow, LANES), 0), B_HEADS)
            logit = jnp.where(key <= trow, logit, NEG)
        m_old = m_ref[...]
        m_new = jnp.maximum(m_old, jnp.max(logit, axis=1, keepdims=True))
        p = jnp.exp(logit - m_new)
        alpha = jnp.exp(m_old - m_new)
        l_ref[...] = alpha * l_ref[...] + jnp.sum(p, axis=1, keepdims=True)
        m_ref[...] = m_new
        acc_ref[...] = alpha[:, 0:1] * acc_ref[...] + _dot_nt(p.astype(BF16), vpage.astype(BF16))
        r_ref[...] = r_ref[...] + tot

    for u in range(pp):
        page(kp[u][...], vp[u][...], lp[u][...], False)

    @pl.when(step == pl.num_programs(1) - 1)
    def _():
        page(kn_ref[0], vn_ref[0], lfn_ref[0], True)
        res = jnp.where(bd, acc_ref[...] / l_ref[:, 0:1], 0.0)
        for t in range(dec):
            o_ref[0, t:t + 1, :] = jnp.sum(res[t * B_HEADS:(t + 1) * B_HEADS], axis=0, keepdims=True)


def _foxdec(layer, page_table, q, knew, vnew, lfn_t, cache_k, cache_v, cache_lft, pp):
    ns, dec, _ = q.shape
    n_pages = page_table.shape[1]
    page = cache_k.shape[3]
    nsteps = n_pages // pp
    nrow = dec * B_HEADS
    triu = np.triu(np.ones((page, page), np.float32))
    seq3 = lambda s, p, pt: (s, 0, 0)
    fix2 = lambda s, p, pt: (0, 0)

    def cache_map(u):
        return lambda s, p, pt: (layer, pt[s, p * pp + u], 0, 0)

    in_specs = [pl.BlockSpec((1, dec, 512), seq3), pl.BlockSpec((1, 512, page), seq3),
                pl.BlockSpec((1, 512, page), seq3), pl.BlockSpec((1, B_HEADS, page), seq3),
                pl.BlockSpec((page, page), fix2), pl.BlockSpec((page, page), fix2)]
    in_specs += [pl.BlockSpec((None, None, 512, page), cache_map(u)) for u in range(pp)]
    in_specs += [pl.BlockSpec((None, None, 512, page), cache_map(u)) for u in range(pp)]
    in_specs += [pl.BlockSpec((None, None, B_HEADS, page), cache_map(u)) for u in range(pp)]
    grid_spec = pltpu.PrefetchScalarGridSpec(
        num_scalar_prefetch=1,
        grid=(ns, nsteps),
        in_specs=in_specs,
        out_specs=pl.BlockSpec((1, dec, 512), seq3),
        scratch_shapes=[pltpu.VMEM((nrow, 512), BF16), pltpu.VMEM((nrow, LANES), F32),
                        pltpu.VMEM((nrow, LANES), F32), pltpu.VMEM((nrow, LANES), F32),
                        pltpu.VMEM((nrow, 512), F32)],
    )
    return pl.pallas_call(
        functools.partial(_foxdec_kernel, pp=pp, dec=dec),
        grid_spec=grid_spec,
        out_shape=jax.ShapeDtypeStruct((ns, dec, 512), F32),
        compiler_params=_cp("arbitrary", "arbitrary"),
        name="foxdec",
    )(page_table, q, knew, vnew, lfn_t, _const(triu), _const(np.ones((page, page), np.float32)),
      *([cache_k] * pp), *([cache_v] * pp), *([cache_lft] * pp))


def _hgrn_kernel(*refs, layer, ns, cps, valid, has_init):
    (q_ref, f_ref, v_ref, gate_ref, lbraw_ref, ng_ref, tril_ref, rs_ref, tile_ref, hs_ref) = refs[:10]
    pos = 10
    s0_ref = None
    if has_init:
        s0_ref = refs[pos]
        pos += 1
    o_ref, sout_ref, st_ref, oin_ref, tmp_ref = refs[pos:pos + 5]
    i = pl.program_id(1)
    tb = q_ref.shape[0]
    nchunk = tb // SUB
    width = A_HEADS * HD

    lbraw = lbraw_ref[...]
    e = jnp.exp(lbraw - jnp.max(lbraw, axis=0, keepdims=True))
    prob = e / jnp.sum(e, axis=0, keepdims=True)
    lb = jnp.zeros((1, width), F32)
    for d in range(1, layer + 1):
        lb = lb + prob[d:d + 1, :]

    fr = f_ref[...]
    qr = q_ref[...]
    vr = v_ref[...]
    logf = jnp.log(lb + (1.0 - lb) * _sigmoid(fr))
    key = (1.0 - lb) * _sigmoid(-fr)
    if valid < SUB:
        rowv = _imod(lax.broadcasted_iota(jnp.int32, (tb, width), 0), SUB) < valid
        logf = jnp.where(rowv, logf, 0.0)
        key = jnp.where(rowv, key, 0.0)
    q = (qr * _sigmoid(qr)) * (HD ** -0.5)
    g = _exact_left(tril_ref[...], logf)
    g3 = g.reshape(nchunk, SUB, width)
    q3 = q.reshape(nchunk, SUB, width)
    k3 = key.reshape(nchunk, SUB, width)

    att = jnp.zeros((tb, A_HEADS * LANES), F32)
    for s in range(SUB):
        d = jnp.minimum(g3 - g3[:, s:s + 1, :], 0.0)
        p = (q3 * jnp.exp(d)) * k3[:, s:s + 1, :]
        att = att + _dot(p.reshape(tb, width).astype(BF16), rs_ref[s])

    rows = lax.broadcasted_iota(jnp.int32, (tb, tb), 0)
    cols = lax.broadcasted_iota(jnp.int32, (tb, tb), 1)
    causal = (_idiv(rows, SUB) == _idiv(cols, SUB)) & (cols <= rows)
    lane = lax.broadcasted_iota(jnp.int32, (tb, LANES), 1)
    for pr in range(2):
        vpair = vr[:, pr * LANES:(pr + 1) * LANES]
        acc = None
        for hh in range(2):
            h = 2 * pr + hh
            full = _dot(att[:, h * LANES:(h + 1) * LANES].astype(BF16), tile_ref[...])
            full = jnp.where(causal, full, 0.0).astype(BF16)
            vm = jnp.where((lane < HD) == (hh == 0), vpair, 0.0).astype(BF16)
            term = _dot(full, vm)
            acc = term if acc is None else acc + term
        oin_ref[:, pr * LANES:(pr + 1) * LANES] = acc

    gend3 = g3[:, SUB - 1:SUB, :]
    kd = (k3 * jnp.exp(gend3 - g3)).reshape(tb, width).astype(BF16)
    eg = jnp.exp(g)
    qg = (q * eg).astype(BF16)
    r128 = lax.broadcasted_iota(jnp.int32, (LANES, LANES), 0)
    c128 = lax.broadcasted_iota(jnp.int32, (LANES, LANES), 1)
    bdm = _idiv(r128, HD) == _idiv(c128, HD)
    vb = vr.astype(BF16)
    last = i == pl.num_programs(1) - 1
    for sq in range(ns):
        for pr in range(2):
            sl = slice(pr * LANES, (pr + 1) * LANES)
            if has_init:
                tmp_ref[...] = jnp.zeros_like(tmp_ref)
                tmp_ref[0:HD, 0:HD] = s0_ref[sq, 2 * pr]
                tmp_ref[HD:LANES, HD:LANES] = s0_ref[sq, 2 * pr + 1]
                st = tmp_ref[...].T
            else:
                st = jnp.where(i == 0, 0.0, st_ref[pr])
            for c in range(cps):
                ci = sq * cps + c
                r0 = ci * SUB
                oin_ref[r0:r0 + SUB, sl] = oin_ref[r0:r0 + SUB, sl] + _dot_nt(qg[r0:r0 + SUB, sl],
                                                                              st.astype(BF16))
                u = _dot_tn(vb[r0:r0 + SUB, sl], kd[r0:r0 + SUB, sl])
                st = st * eg[r0 + SUB - 1:r0 + SUB, sl] + jnp.where(bdm, u, 0.0)
            if not has_init:
                st_ref[pr] = st

            @pl.when(last)
            def _():
                tmp_ref[...] = st.T
                sout_ref[sq, 2 * pr] = tmp_ref[0:HD, 0:HD]
                sout_ref[sq, 2 * pr + 1] = tmp_ref[HD:LANES, HD:LANES]

    o = oin_ref[...]
    gr = gate_ref[...]
    o = (o * lax.rsqrt(_head_ssq(o, hs_ref[...]) * (1.0 / HD) + EPS)) * ng_ref[...]
    o_ref[...] = o * (gr * _sigmoid(gr))


def _hgrn(a, lbraw, ng, s0, layer, n_outer, n_inner, tb, ns, cps, valid):
    rows = a.shape[0]
    width = A_HEADS * HD
    has_init = s0 is not None
    nseq = n_outer * ns
    rs = np.zeros((SUB, width, A_HEADS * LANES), np.float32)
    for s in range(SUB):
        for h in range(A_HEADS):
            rs[s, h * HD + np.arange(HD), h * LANES + s] = 1.0
    tile = np.zeros((LANES, tb), np.float32)
    for s in range(SUB):
        tile[s, np.arange(tb // SUB) * SUB + s] = 1.0

    def col(j):
        return lambda o, i: (o * n_inner + i, j)

    fix2 = lambda o, i: (0, 0)
    fix3 = lambda o, i: (0, 0, 0)
    in_specs = [pl.BlockSpec((tb, width), col(j)) for j in range(4)]
    in_specs += [pl.BlockSpec(lbraw.shape, fix2), pl.BlockSpec((1, width), fix2),
                 pl.BlockSpec((tb, tb), fix2), pl.BlockSpec((SUB, width, A_HEADS * LANES), fix3),
                 pl.BlockSpec((LANES, tb), fix2), pl.BlockSpec((width, width), fix2)]
    args = [a, a, a, a, lbraw, ng, _const(_chunk_tril(tb, SUB)), _const(rs), _const(tile),
            _const(_head_sum_matrix(width))]
    if has_init:
        in_specs.append(pl.BlockSpec((ns, A_HEADS, HD, HD), lambda o, i: (o, 0, 0, 0)))
        args.append(s0)
    return pl.pallas_call(
        functools.partial(_hgrn_kernel, layer=layer, ns=ns, cps=cps, valid=valid, has_init=has_init),
        grid=(n_outer, n_inner),
        in_specs=in_specs,
        out_specs=[pl.BlockSpec((tb, width), lambda o, i: (o * n_inner + i, 0)),
                   pl.BlockSpec((ns, A_HEADS, HD, HD), lambda o, i: (o, 0, 0, 0))],
        out_shape=[jax.ShapeDtypeStruct((rows, width), F32),
                   jax.ShapeDtypeStruct((nseq, A_HEADS, HD, HD), F32)],
        scratch_shapes=[pltpu.VMEM((2, LANES, LANES), F32), pltpu.VMEM((tb, width), F32),
                        pltpu.VMEM((LANES, LANES), F32)],
        compiler_params=_cp("arbitrary", "arbitrary"),
        name="hgrn",
    )(*args)


def _gdn_kernel(*refs, ns, cps, chunk, has_init):
    (x_ref, z_ref, g_ref, cw_ref, ng_ref, tril_ref, hs_ref, eg_ref, eb_ref, eg2_ref, eb2_ref) = refs[:11]
    pos = 11
    cinit_ref = s0_ref = None
    if has_init:
        cinit_ref, s0_ref = refs[pos:pos + 2]
        pos += 2
    o_ref, sout_ref, st_ref, cv_ref, oacc_ref, tmp_ref = refs[pos:pos + 6]
    i = pl.program_id(1)
    tb = x_ref.shape[0]
    width = C_HEADS * HD
    rows_seq = cps * chunk
    last = i == pl.num_programs(1) - 1

    x = x_ref[...]
    cw = cw_ref[...]
    acts = []
    for sq in range(ns):
        xs = x[sq * rows_seq:(sq + 1) * rows_seq]
        if has_init:
            prev = cinit_ref[sq]
        else:
            prev = jnp.where(i == 0, 0.0, cv_ref[...])
        xc = jnp.concatenate([prev, xs], axis=0)
        conv = xc * cw[CONV_W - 1:CONV_W, :]
        for j in range(1, CONV_W):
            conv = conv + pltpu.roll(xc, j, 0) * cw[CONV_W - 1 - j:CONV_W - j, :]
        acts.append(conv[8:])
        if not has_init:
            cv_ref[...] = xs[rows_seq - 8:]
    conv = acts[0] if ns == 1 else jnp.concatenate(acts, axis=0)
    act = conv * _sigmoid(conv)
    hs = hs_ref[...]
    q = act[:, 0:width]
    k = act[:, width:2 * width]
    v = act[:, 2 * width:3 * width]
    q = (q * lax.rsqrt(_head_ssq(q, hs) + EPS)) * (HD ** -0.5)
    k = k * lax.rsqrt(_head_ssq(k, hs) + EPS)

    gt = g_ref[...]
    gc = _exact_left(tril_ref[...], gt)
    gexp = _exact_right(gc, eg_ref[...])
    bexp = _exact_right(gt, eb_ref[...])
    gcol = _exact_right(gc, eg2_ref[...])
    bcol = _exact_right(gt, eb2_ref[...])
    nchunk = tb // chunk
    gexp3 = gexp.reshape(nchunk, chunk, width)
    gend3 = gexp3[:, chunk - 1:chunk, :]
    eg = jnp.exp(gexp)
    rhs = jnp.concatenate([v * bexp, (k * bexp) * eg], axis=1).astype(BF16)
    qg = (q * eg).astype(BF16)
    kd = (k.reshape(nchunk, chunk, width) * jnp.exp(gend3 - gexp3)).reshape(tb, width).astype(BF16)
    kb = k.astype(BF16)
    lane = lax.broadcasted_iota(jnp.int32, (tb, width), 1)
    kmask = [jnp.where(_idiv(lane, HD) == h, k, 0.0).astype(BF16) for h in range(C_HEADS)]
    qmask = [jnp.where(_idiv(lane, HD) == h, q, 0.0).astype(BF16) for h in range(C_HEADS)]

    rc = lax.broadcasted_iota(jnp.int32, (chunk, chunk), 0)
    cc = lax.broadcasted_iota(jnp.int32, (chunk, chunk), 1)
    eye = rc == cc
    incl = cc <= rc
    strict = cc < rc
    ones_c = jnp.ones((chunk, chunk), BF16)
    hl = _idiv(lax.broadcasted_iota(jnp.int32, (chunk, width), 1), HD)
    r256 = lax.broadcasted_iota(jnp.int32, (width, width), 0)
    c256 = lax.broadcasted_iota(jnp.int32, (width, width), 1)
    bdm = _idiv(r256, HD) == _idiv(c256, HD)
    nstage = int(math.log2(chunk))

    for sq in range(ns):
        if has_init:
            tmp_ref[...] = jnp.zeros_like(tmp_ref)
            for h in range(C_HEADS):
                tmp_ref[h * HD:(h + 1) * HD, h * HD:(h + 1) * HD] = s0_ref[sq, h]
            st = tmp_ref[...]
        else:
            st = jnp.where(i == 0, 0.0, st_ref[...])
        for c in range(cps):
            ci = sq * cps + c
            r0 = ci * chunk
            rsl = slice(r0, r0 + chunk)
            u = jnp.zeros((chunk, width), F32)
            w = jnp.zeros((chunk, width), F32)
            qkd = []
            for h in range(C_HEADS):
                gcm = gcol[rsl, h * LANES:h * LANES + chunk]
                grow = _exact_left(ones_c, jnp.where(eye, gcm, 0.0))
                decay = jnp.exp(jnp.minimum(gcm - grow, 0.0))
                kk = _dot_nt(kmask[h][rsl], kb[rsl])
                qk = _dot_nt(qmask[h][rsl], kb[rsl])
                lower = jnp.where(strict, (bcol[rsl, h * LANES:h * LANES + chunk] * kk) * decay, 0.0)
                xm = -lower
                tm = jnp.where(eye, 1.0, 0.0) + xm
                for _ in range(nstage - 1):
                    xb = xm.astype(BF16)
                    xm = _dot(xb, xb)
                    tm = tm + _dot(tm.astype(BF16), xm.astype(BF16))
                sol = _dot(tm.astype(BF16), rhs[rsl])
                u = u + jnp.where(hl == h, sol[:, 0:width], 0.0)
                w = w + jnp.where(hl == h, sol[:, width:2 * width], 0.0)
                qkd.append(jnp.where(incl, qk * decay, 0.0).astype(BF16))
            stb = st.astype(BF16)
            vnew = u - _dot(w.astype(BF16), stb)
            vnb = vnew.astype(BF16)
            o = _dot(qg[rsl], stb)
            for h in range(C_HEADS):
                o = o + jnp.where(hl == h, _dot(qkd[h], vnb), 0.0)
            oacc_ref[rsl, :] = o
            st = st * eg[r0 + chunk - 1:r0 + chunk, :] + jnp.where(bdm, _dot_tn(kd[rsl], vnb), 0.0)
        if not has_init:
            st_ref[...] = st

        @pl.when(last)
        def _():
            tmp_ref[...] = st
            for h in range(C_HEADS):
                sout_ref[sq, h] = tmp_ref[h * HD:(h + 1) * HD, h * HD:(h + 1) * HD]

    o = oacc_ref[...]
    z = z_ref[...]
    o = (o * lax.rsqrt(_head_ssq(o, hs) * (1.0 / HD) + EPS)) * ng_ref[...]
    o_ref[...] = o * (z * _sigmoid(z))


def _gdn(cfull, g, cw, ng, cinit, s0, n_outer, n_inner, tb, ns, cps, chunk):
    rows = cfull.shape[0]
    width = C_HEADS * HD
    has_init = s0 is not None
    nseq = n_outer * ns
    eg = np.zeros((LANES, width), np.float32)
    eb = np.zeros((LANES, width), np.float32)
    eg2 = np.zeros((LANES, C_HEADS * LANES), np.float32)
    eb2 = np.zeros((LANES, C_HEADS * LANES), np.float32)
    for h in range(C_HEADS):
        eg[8 + h, h * HD:(h + 1) * HD] = 1.0
        eb[12 + h, h * HD:(h + 1) * HD] = 1.0
        eg2[8 + h, h * LANES:(h + 1) * LANES] = 1.0
        eb2[12 + h, h * LANES:(h + 1) * LANES] = 1.0
    rowm = lambda o, i: (o * n_inner + i, 0)
    fix2 = lambda o, i: (0, 0)
    in_specs = [pl.BlockSpec((tb, 3 * width), rowm),
                pl.BlockSpec((tb, width), lambda o, i: (o * n_inner + i, 3)),
                pl.BlockSpec((tb, LANES), rowm), pl.BlockSpec((CONV_W, 3 * width), fix2),
                pl.BlockSpec((1, width), fix2), pl.BlockSpec((tb, tb), fix2),
                pl.BlockSpec((width, width), fix2), pl.BlockSpec((LANES, width), fix2),
                pl.BlockSpec((LANES, width), fix2), pl.BlockSpec((LANES, C_HEADS * LANES), fix2),
                pl.BlockSpec((LANES, C_HEADS * LANES), fix2)]
    args = [cfull, cfull, g, cw, ng, _const(_chunk_tril(tb, chunk)), _const(_head_sum_matrix(width)),
            _const(eg), _const(eb), _const(eg2), _const(eb2)]
    if has_init:
        in_specs += [pl.BlockSpec((ns, 8, 3 * width), lambda o, i: (o, 0, 0)),
                     pl.BlockSpec((ns, C_HEADS, HD, HD), lambda o, i: (o, 0, 0, 0))]
        args += [cinit, s0]
    return pl.pallas_call(
        functools.partial(_gdn_kernel, ns=ns, cps=cps, chunk=chunk, has_init=has_init),
        grid=(n_outer, n_inner),
        in_specs=in_specs,
        out_specs=[pl.BlockSpec((tb, width), rowm),
                   pl.BlockSpec((ns, C_HEADS, HD, HD), lambda o, i: (o, 0, 0, 0))],
        out_shape=[jax.ShapeDtypeStruct((rows, width), F32),
                   jax.ShapeDtypeStruct((nseq, C_HEADS, HD, HD), F32)],
        scratch_shapes=[pltpu.VMEM((width, width), F32), pltpu.VMEM((8, 3 * width), F32),
                        pltpu.VMEM((tb, width), F32), pltpu.VMEM((width, width), F32)],
        compiler_params=_cp("arbitrary", "arbitrary"),
        name="gdn",
    )(*args)


def _post_kernel(x_ref, oa_ref, ob_ref, oc_ref, wo_ref, ln_ref, wr_ref, br_ref,
                 xn_ref, hf_ref, gate_ref):
    x = x_ref[...]
    mix = _dot(oa_ref[...].astype(BF16), wo_ref[0:256, :])
    mix = mix + _dot(ob_ref[...].astype(BF16), wo_ref[256:768, :])
    mix = mix + _dot(oc_ref[...].astype(BF16), wo_ref[768:1024, :])
    xn = x + mix
    xn_ref[...] = xn
    ms = jnp.mean(xn * xn, axis=-1, keepdims=True)
    hf = ((xn * lax.rsqrt(ms + EPS)) * ln_ref[...]).astype(BF16)
    hf_ref[...] = hf
    logits = _dot(hf, wr_ref[...]) + br_ref[...]
    lane = lax.broadcasted_iota(jnp.int32, logits.shape, 1)
    big = jnp.int32(1 << 20)
    isg = lane < N_GROUPS
    gl = jnp.where(isg, logits, NEG)
    gm = jnp.max(gl, axis=1, keepdims=True)
    gidx = jnp.min(jnp.where(isg & (gl == gm), lane, big), axis=1, keepdims=True)
    top_gp = 1.0 / jnp.sum(jnp.where(isg, jnp.exp(gl - gm), 0.0), axis=1, keepdims=True)
    eidx = lane - N_GROUPS
    ing = (eidx >= 0) & (eidx < N_EXPERTS) & (_idiv(eidx, EXPERTS_PER_GROUP) == gidx)
    el = jnp.where(ing, logits, NEG)
    em = jnp.max(el, axis=1, keepdims=True)
    ee = jnp.where(ing, jnp.exp(el - em), 0.0)
    prob = ee / jnp.sum(ee, axis=1, keepdims=True)
    p1 = jnp.max(prob, axis=1, keepdims=True)
    i1 = jnp.min(jnp.where(ing & (prob == p1), lane, big), axis=1, keepdims=True)
    rest = jnp.where(ing & (lane != i1), prob, -1.0)
    p2 = jnp.max(rest, axis=1, keepdims=True)
    i2 = jnp.min(jnp.where(ing & (lane != i1) & (rest == p2), lane, big), axis=1, keepdims=True)
    den = p1 + p2
    gsel = jnp.where(lane == i1, (top_gp * p1) / den, jnp.where(lane == i2, (top_gp * p2) / den, 0.0))
    sh_r = lax.broadcasted_iota(jnp.int32, (LANES, LANES), 0)
    sh_c = lax.broadcasted_iota(jnp.int32, (LANES, LANES), 1)
    shift = jnp.where(sh_r == sh_c + N_GROUPS, 1.0, 0.0).astype(BF16)
    gate_ref[...] = _exact_right(gsel, shift)


def _post(x, oa, ob, oc, wo, ln, wr, br, tm):
    t = x.shape[0]
    row = lambda i: (i, 0)
    fix = lambda i: (0, 0)
    return pl.pallas_call(
        _post_kernel,
        grid=(t // tm,),
        in_specs=[pl.BlockSpec((tm, D_MODEL), row), pl.BlockSpec((tm, 256), row),
                  pl.BlockSpec((tm, 512), row), pl.BlockSpec((tm, 256), row),
                  pl.BlockSpec((D_MODEL, D_MODEL), fix), pl.BlockSpec((1, D_MODEL), fix),
                  pl.BlockSpec((D_MODEL, LANES), fix), pl.BlockSpec((1, LANES), fix)],
        out_specs=[pl.BlockSpec((tm, D_MODEL), row), pl.BlockSpec((tm, D_MODEL), row),
                   pl.BlockSpec((tm, LANES), row)],
        out_shape=[jax.ShapeDtypeStruct((t, D_MODEL), F32), jax.ShapeDtypeStruct((t, D_MODEL), BF16),
                   jax.ShapeDtypeStruct((t, LANES), F32)],
        compiler_params=_cp("arbitrary"),
        name="post",
    )(x, oa, ob, oc, wo, ln, wr, br)


def _moe_kernel(h_ref, gate_ref, x_ref, wg_ref, wu_ref, wd_ref, o_ref, acc_ref):
    e = pl.program_id(1)

    @pl.when(e == 0)
    def _():
        acc_ref[...] = x_ref[...]

    h = h_ref[...]
    gates = gate_ref[...]
    lane = lax.broadcasted_iota(jnp.int32, gates.shape, 1)
    ge = jnp.sum(jnp.where(lane == e, gates, 0.0), axis=1, keepdims=True)
    gp = _dot(h, wg_ref[0])
    up = _dot(h, wu_ref[0])
    a = ((gp * _sigmoid(gp)) * up) * ge
    acc_ref[...] += _dot(a.astype(BF16), wd_ref[0])

    @pl.when(e == pl.num_programs(1) - 1)
    def _():
        o_ref[...] = acc_ref[...]


def _moe(hf, gates, xn, wg, wu, wd, tm):
    t = hf.shape[0]
    ne = wg.shape[0]
    row = lambda i, e: (i, 0)
    exp3 = lambda i, e: (e, 0, 0)
    return pl.pallas_call(
        _moe_kernel,
        grid=(t // tm, ne),
        in_specs=[pl.BlockSpec((tm, D_MODEL), row), pl.BlockSpec((tm, LANES), row),
                  pl.BlockSpec((tm, D_MODEL), row),
                  pl.BlockSpec((1, D_MODEL, D_EXPERT), exp3), pl.BlockSpec((1, D_MODEL, D_EXPERT), exp3),
                  pl.BlockSpec((1, D_EXPERT, D_MODEL), exp3)],
        out_specs=pl.BlockSpec((tm, D_MODEL), row),
        out_shape=jax.ShapeDtypeStruct((t, D_MODEL), F32),
        scratch_shapes=[pltpu.VMEM((tm, D_MODEL), F32)],
        compiler_params=_cp("arbitrary", "arbitrary"),
        name="moe",
    )(hf, gates, xn, wg, wu, wd)


def _pad_rows(a, nseq, dec):
    c = a.shape[-1]
    return jnp.pad(a.reshape(nseq, dec, c), ((0, 0), (0, SUB - dec), (0, 0))).reshape(nseq * SUB, c)


def _layer_params(l, ln_mix, w_in, hgrn_norm, fox_bf, fox_qnorm, fox_knorm, gdn_conv, gdn_a_log,
                  gdn_dt_bias, gdn_norm, w_out, ln_ffn, w_group, b_group, w_router, b_router,
                  w_gate, w_up, w_down):
    w = w_in[l]
    gates = jnp.concatenate([w[:, 2560:2568], w[:, 3592:3600]], axis=1)
    wp = jnp.concatenate([w[:, 0:2560], w[:, 2568:3592], gates,
                          jnp.zeros((D_MODEL, LANES - 16), F32)], axis=1).astype(BF16)
    p1 = jnp.zeros((1, LANES), F32).at[0, 0:8].set(fox_bf[l]).at[0, 8:12].set(gdn_dt_bias[l])
    p2 = jnp.zeros((1, LANES), F32).at[0, 8:12].set(gdn_a_log[l])
    wr = jnp.concatenate([w_group[l], w_router[l],
                          jnp.zeros((D_MODEL, LANES - N_GROUPS - N_EXPERTS), F32)], axis=1).astype(BF16)
    br = jnp.zeros((1, LANES), F32).at[0, 0:N_GROUPS].set(b_group[l])
    br = br.at[0, N_GROUPS:N_GROUPS + N_EXPERTS].set(b_router[l])
    return dict(
        ln_mix=ln_mix[l][None, :], w=wp, p1=p1, p2=p2,
        qg=jnp.tile(fox_qnorm[l], B_HEADS)[None, :], kg=jnp.tile(fox_knorm[l], B_HEADS)[None, :],
        hgrn_ng=jnp.tile(hgrn_norm[l], A_HEADS)[None, :], gdn_ng=jnp.tile(gdn_norm[l], C_HEADS)[None, :],
        conv=gdn_conv[l], w_out=w_out[l].astype(BF16), ln_ffn=ln_ffn[l][None, :], wr=wr, br=br,
        wg=w_gate[l].astype(BF16), wu=w_up[l].astype(BF16), wd=w_down[l].astype(BF16))


def kernel(x_prompt, x_sample, cache_k, cache_v, cache_logf, page_table, state_hgrn, state_gdn, state_conv,
           ln_mix, w_in, hgrn_lb, hgrn_norm, fox_bf, fox_qnorm, fox_knorm, gdn_conv, gdn_a_log, gdn_dt_bias,
           gdn_norm, w_out, ln_ffn, w_group, b_group, w_router, b_router, w_gate, w_up, w_down):
    nb, seq, _ = x_prompt.shape
    nsq, dec, _ = x_sample.shape
    depth = ln_mix.shape[0]
    n_phys, page = cache_k.shape[1], cache_k.shape[2]
    tp = nb * seq
    ts = nsq * dec
    hs512 = _const(_head_sum_matrix(512))
    ck = jnp.transpose(cache_k, (0, 1, 3, 4, 2)).reshape(depth, n_phys, 512, page)
    cv = jnp.transpose(cache_v, (0, 1, 3, 4, 2)).reshape(depth, n_phys, 512, page)
    clt = jnp.swapaxes(cache_logf, 2, 3)
    tbp = min(256, seq)
    gchunk = min(64, seq)
    seq_blk = 16

    yp = x_prompt.reshape(tp, D_MODEL)
    ys = x_sample.reshape(ts, D_MODEL)
    outs_p, outs_s = [], []
    for l in range(depth):
        P = _layer_params(l, ln_mix, w_in, hgrn_norm, fox_bf, fox_qnorm, fox_knorm, gdn_conv, gdn_a_log,
                          gdn_dt_bias, gdn_norm, w_out, ln_ffn, w_group, b_group, w_router, b_router,
                          w_gate, w_up, w_down)

        a, bq, bk, bv, c, g = _proj(yp, P["ln_mix"], P["w"], hs512, P["qg"], P["kg"], P["p1"], P["p2"],
                                    tm=min(256, tp))
        qa, ka = _foxprep(bq, bk, g, nb, tm=min(256, seq))
        ob = _fox(qa, ka, bv, nb, tq=min(256, seq))
        oa, hst = _hgrn(a, hgrn_lb, P["hgrn_ng"], None, l, nb, seq // tbp, tbp, 1, tbp // SUB, SUB)
        oc, gst = _gdn(c, g, P["conv"], P["gdn_ng"], None, None, nb, seq // tbp, tbp, 1,
                       tbp // gchunk, gchunk)
        xn, hf, gates = _post(yp, oa, ob, oc, P["w_out"], P["ln_ffn"], P["wr"], P["br"], tm=min(256, tp))
        yp_new = _moe(hf, gates, xn, P["wg"], P["wu"], P["wd"], tm=min(512, tp))
        outs_p.append((bk.reshape(nb, seq, B_HEADS, HD), bv.reshape(nb, seq, B_HEADS, HD),
                       g[:, 0:8].reshape(nb, seq, B_HEADS), hst, gst,
                       c[:, 0:768].reshape(nb, seq, 768)[:, seq - (CONV_W - 1):, :]))
        yp = yp_new

        a, bq, bk, bv, c, g = _proj(ys, P["ln_mix"], P["w"], hs512, P["qg"], P["kg"], P["p1"], P["p2"],
                                    tm=min(256, ts))
        knew = jnp.pad(jnp.swapaxes(bk.reshape(nsq, dec, 512), 1, 2), ((0, 0), (0, 0), (0, page - dec)))
        vnew = jnp.pad(jnp.swapaxes(bv.reshape(nsq, dec, 512), 1, 2), ((0, 0), (0, 0), (0, page - dec)))
        lfn = jnp.pad(jnp.swapaxes(g[:, 0:8].reshape(nsq, dec, B_HEADS), 1, 2),
                      ((0, 0), (0, 0), (0, page - dec)))
        ob = _foxdec(l, page_table, bq.reshape(nsq, dec, 512), knew, vnew, lfn, ck, cv, clt,
                     pp=4).reshape(ts, 512)
        n_outer = nsq // seq_blk
        oa, hst = _hgrn(_pad_rows(a, nsq, dec), hgrn_lb, P["hgrn_ng"], state_hgrn[l], l,
                        n_outer, 1, seq_blk * SUB, seq_blk, 1, dec)
        cinit = jnp.pad(state_conv[l], ((0, 0), (8 - (CONV_W - 1), 0), (0, 0)))
        oc, gst = _gdn(_pad_rows(c, nsq, dec), _pad_rows(g, nsq, dec), P["conv"], P["gdn_ng"], cinit,
                       state_gdn[l], n_outer, 1, seq_blk * SUB, seq_blk, 1, SUB)
        oa = oa.reshape(nsq, SUB, 256)[:, :dec].reshape(ts, 256)
        oc = oc.reshape(nsq, SUB, 256)[:, :dec].reshape(ts, 256)
        xn, hf, gates = _post(ys, oa, ob, oc, P["w_out"], P["ln_ffn"], P["wr"], P["br"], tm=min(256, ts))
        ys_new = _moe(hf, gates, xn, P["wg"], P["wu"], P["wd"], tm=min(512, ts))
        xpad = jnp.concatenate([state_conv[l], c[:, 0:768].reshape(nsq, dec, 768)], axis=1)
        outs_s.append((bk.reshape(nsq, dec, B_HEADS, HD), bv.reshape(nsq, dec, B_HEADS, HD),
                       g[:, 0:8].reshape(nsq, dec, B_HEADS), hst, gst, xpad[:, -(CONV_W - 1):, :]))
        ys = ys_new

    stack = lambda outs, j: jnp.stack([o[j] for o in outs], axis=0)
    return (yp.reshape(nb, seq, D_MODEL), ys.reshape(nsq, dec, D_MODEL),
            *[stack(outs_p, j) for j in range(6)], *[stack(outs_s, j) for j in range(6)])
```

```python
import functools
import math

import numpy as np
import jax
import jax.numpy as jnp
from jax import lax
from jax.experimental import pallas as pl
from jax.experimental.pallas import tpu as pltpu

F32 = jnp.float32
BF16 = jnp.bfloat16
EPS = 1e-6
NEG = -1e30

D_MODEL = 1024
HD = 64
A_HEADS = 4
B_HEADS = 8
C_HEADS = 4
CONV_W = 4
N_GROUPS = 4
EXPERTS_PER_GROUP = 8
N_EXPERTS = N_GROUPS * EXPERTS_PER_GROUP
D_EXPERT = D_MODEL // 4
SUB = 16
LANES = 128
VMEM_LIMIT = 56 * 1024 * 1024


def _cp(*sem):
    return pltpu.CompilerParams(dimension_semantics=sem, vmem_limit_bytes=VMEM_LIMIT)


def _dot(a, b):
    return jnp.dot(a, b, preferred_element_type=F32)


def _dot_nt(a, b):
    return lax.dot_general(a, b, (((1,), (1,)), ((), ())), preferred_element_type=F32)


def _dot_tn(a, b):
    return lax.dot_general(a, b, (((0,), (0,)), ((), ())), preferred_element_type=F32)


def _split3(x):
    hi = x.astype(BF16)
    r = x - hi.astype(F32)
    mid = r.astype(BF16)
    lo = (r - mid.astype(F32)).astype(BF16)
    return hi, mid, lo


def _exact_left(m, x):
    hi, mid, lo = _split3(x)
    return (_dot(m, hi) + _dot(m, mid)) + _dot(m, lo)


def _exact_right(x, m):
    hi, mid, lo = _split3(x)
    return (_dot(hi, m) + _dot(mid, m)) + _dot(lo, m)


def _idiv(x, n):
    return jnp.right_shift(x, int(math.log2(n)))


def _imod(x, n):
    return jnp.bitwise_and(x, n - 1)


def _sigmoid(x):
    return 1.0 / (1.0 + jnp.exp(-x))


def _softplus(z):
    return jnp.maximum(z, 0.0) + jnp.log(1.0 + jnp.exp(-jnp.abs(z)))


def _head_ssq(z, hs):
    zz = z * z
    hi = zz.astype(BF16)
    lo = (zz - hi.astype(F32)).astype(BF16)
    return _dot(hi, hs) + _dot(lo, hs)


def _const(a, dtype=BF16):
    return jnp.asarray(a, dtype=dtype)


def _head_sum_matrix(width):
    i = np.arange(width)
    return (i[:, None] // HD == i[None, :] // HD).astype(np.float32)


def _chunk_tril(n, c):
    i = np.arange(n)
    return ((i[:, None] // c == i[None, :] // c) & (i[None, :] <= i[:, None])).astype(np.float32)


PROJ_COLS = 3712


def _proj_kernel(x_ref, ln_ref, w_ref, hs_ref, qg_ref, kg_ref, p1_ref, p2_ref,
                 a_ref, bq_ref, bk_ref, bv_ref, c_ref, g_ref):
    x = x_ref[...]
    ms = jnp.mean(x * x, axis=-1, keepdims=True)
    hn = ((x * lax.rsqrt(ms + EPS)) * ln_ref[...]).astype(BF16)
    a_ref[...] = _dot(hn, w_ref[:, 0:1024])
    hs = hs_ref[...]
    q = _dot(hn, w_ref[:, 1024:1536])
    bq_ref[...] = ((q * lax.rsqrt(_head_ssq(q, hs) * (1.0 / HD) + EPS)) * qg_ref[...]) * (HD ** -0.5)
    k = _dot(hn, w_ref[:, 1536:2048])
    bk_ref[...] = (k * lax.rsqrt(_head_ssq(k, hs) * (1.0 / HD) + EPS)) * kg_ref[...]
    bv_ref[...] = _dot(hn, w_ref[:, 2048:2560])
    c_ref[...] = _dot(hn, w_ref[:, 2560:3584])
    gr = _dot(hn, w_ref[:, 3584:3712])
    lane = lax.broadcasted_iota(jnp.int32, gr.shape, 1)
    z = gr + p1_ref[...]
    sp = _softplus(z)
    logsig = jnp.minimum(z, 0.0) - jnp.log(1.0 + jnp.exp(-jnp.abs(z)))
    glog = -jnp.exp(p2_ref[...]) * sp
    beta = _sigmoid(gr)
    g_ref[...] = jnp.where(lane < 8, logsig,
                           jnp.where(lane < 12, glog, jnp.where(lane < 16, beta, 0.0)))


def _proj(x, ln, w, hs512, qg, kg, p1, p2, tm):
    t = x.shape[0]
    row = lambda i: (i, 0)
    fix = lambda i: (0, 0)
    outs = [jax.ShapeDtypeStruct((t, n), F32) for n in (1024, 512, 512, 512, 1024, 128)]
    return pl.pallas_call(
        _proj_kernel,
        grid=(t // tm,),
        in_specs=[pl.BlockSpec((tm, D_MODEL), row), pl.BlockSpec((1, D_MODEL), fix),
                  pl.BlockSpec((D_MODEL, PROJ_COLS), fix), pl.BlockSpec((512, 512), fix),
                  pl.BlockSpec((1, 512), fix), pl.BlockSpec((1, 512), fix),
                  pl.BlockSpec((1, LANES), fix), pl.BlockSpec((1, LANES), fix)],
        out_specs=[pl.BlockSpec((tm, n), row) for n in (1024, 512, 512, 512, 1024, 128)],
        out_shape=outs,
        compiler_params=_cp("arbitrary"),
        name="proj",
    )(x, ln, w, hs512, qg, kg, p1, p2)


def _foxprep_kernel(bq_ref, bk_ref, g_ref, tril_ref, pq_ref, pcq_ref, pck_ref, oq_ref, ok_ref,
                    qa_ref, ka_ref, carry_ref):
    i = pl.program_id(1)

    @pl.when(i == 0)
    def _():
        carry_ref[...] = jnp.zeros_like(carry_ref)

    g = g_ref[...]
    c = _exact_left(tril_ref[...], g) + carry_ref[...]
    carry_ref[...] = c[-1:, :]
    hi, mid, lo = _split3(c)
    pq = pq_ref[...]
    qa = _dot(bq_ref[...].astype(BF16), pq) + oq_ref[...]
    ka = _dot(bk_ref[...].astype(BF16), pq) + ok_ref[...]
    for j, part in enumerate((hi, mid, lo)):
        qa = qa + _dot(part, pcq_ref[j])
        ka = ka - _dot(part, pck_ref[j])
    qa_ref[...] = qa.astype(BF16)
    ka_ref[...] = ka.astype(BF16)


def _foxprep(bq, bk, g, nb, tm):
    t = bq.shape[0]
    nblk = t // nb // tm
    pq = np.zeros((512, 1024), np.float32)
    for h in range(B_HEADS):
        pq[h * HD + np.arange(HD), h * LANES + np.arange(HD)] = 1.0
    pcq = np.zeros((3, LANES, 1024), np.float32)
    pck = np.zeros((3, LANES, 1024), np.float32)
    oq = np.zeros((1, 1024), np.float32)
    ok = np.zeros((1, 1024), np.float32)
    for h in range(B_HEADS):
        for j in range(3):
            pcq[j, h, h * LANES + HD + j] = 1.0
            pck[j, h, h * LANES + HD + 3 + j] = 1.0
            oq[0, h * LANES + HD + 3 + j] = 1.0
            ok[0, h * LANES + HD + j] = 1.0
    row = lambda b, i: (b * nblk + i, 0)
    fix2 = lambda b, i: (0, 0)
    fix3 = lambda b, i: (0, 0, 0)
    return pl.pallas_call(
        _foxprep_kernel,
        grid=(nb, nblk),
        in_specs=[pl.BlockSpec((tm, 512), row), pl.BlockSpec((tm, 512), row),
                  pl.BlockSpec((tm, LANES), row), pl.BlockSpec((tm, tm), fix2),
                  pl.BlockSpec((512, 1024), fix2), pl.BlockSpec((3, LANES, 1024), fix3),
                  pl.BlockSpec((3, LANES, 1024), fix3), pl.BlockSpec((1, 1024), fix2),
                  pl.BlockSpec((1, 1024), fix2)],
        out_specs=[pl.BlockSpec((tm, 1024), row), pl.BlockSpec((tm, 1024), row)],
        out_shape=[jax.ShapeDtypeStruct((t, 1024), BF16)] * 2,
        scratch_shapes=[pltpu.VMEM((1, LANES), F32)],
        compiler_params=_cp("arbitrary", "arbitrary"),
        name="foxprep",
    )(bq, bk, g, _const(_chunk_tril(tm, tm)), _const(pq), _const(pcq), _const(pck),
      _const(oq, F32), _const(ok, F32))


def _fox_kernel(q_ref, k_ref, v_ref, o_ref, m_ref, l_ref, acc_ref, *, tq):
    qi = pl.program_id(1)
    ki = pl.program_id(2)

    @pl.when(ki == 0)
    def _():
        m_ref[...] = jnp.full_like(m_ref, NEG)
        l_ref[...] = jnp.zeros_like(l_ref)
        acc_ref[...] = jnp.zeros_like(acc_ref)

    @pl.when(ki <= qi)
    def _():
        rows = lax.broadcasted_iota(jnp.int32, (tq, tq), 0)
        cols = lax.broadcasted_iota(jnp.int32, (tq, tq), 1)
        keep = (cols <= rows) | (ki < qi)
        lane = lax.broadcasted_iota(jnp.int32, (tq, LANES), 1)
        left = lane < HD
        for pr in range(B_HEADS // 2):
            vp = v_ref[:, pr * LANES:(pr + 1) * LANES]
            upd = None
            alphas = []
            for hh in range(2):
                h = 2 * pr + hh
                s = _dot_nt(q_ref[:, h * LANES:(h + 1) * LANES], k_ref[:, h * LANES:(h + 1) * LANES])
                s = jnp.where(keep, s, NEG)
                m_old = m_ref[h]
                m_new = jnp.maximum(m_old, jnp.max(s, axis=1, keepdims=True))
                p = jnp.exp(s - m_new[:, 0:1])
                alpha = jnp.exp(m_old - m_new)
                l_ref[h] = alpha * l_ref[h] + jnp.sum(p, axis=1, keepdims=True)
                m_ref[h] = m_new
                vm = jnp.where((lane < HD) == (hh == 0), vp, 0.0).astype(BF16)
                pv = _dot(p.astype(BF16), vm)
                upd = pv if upd is None else upd + pv
                alphas.append(alpha)
            a = jnp.where(left, alphas[0], alphas[1])
            acc_ref[:, pr * LANES:(pr + 1) * LANES] = a * acc_ref[:, pr * LANES:(pr + 1) * LANES] + upd

    @pl.when(ki == qi)
    def _():
        lane = lax.broadcasted_iota(jnp.int32, (tq, LANES), 1)
        for pr in range(B_HEADS // 2):
            l = jnp.where(lane < HD, l_ref[2 * pr], l_ref[2 * pr + 1])
            o_ref[:, pr * LANES:(pr + 1) * LANES] = acc_ref[:, pr * LANES:(pr + 1) * LANES] / l


def _fox(qa, ka, bv, nb, tq):
    t = qa.shape[0]
    nblk = t // nb // tq
    return pl.pallas_call(
        functools.partial(_fox_kernel, tq=tq),
        grid=(nb, nblk, nblk),
        in_specs=[pl.BlockSpec((tq, 1024), lambda b, qi, ki: (b * nblk + qi, 0)),
                  pl.BlockSpec((tq, 1024), lambda b, qi, ki: (b * nblk + jnp.minimum(ki, qi), 0)),
                  pl.BlockSpec((tq, 512), lambda b, qi, ki: (b * nblk + jnp.minimum(ki, qi), 0))],
        out_specs=pl.BlockSpec((tq, 512), lambda b, qi, ki: (b * nblk + qi, 0)),
        out_shape=jax.ShapeDtypeStruct((t, 512), F32),
        scratch_shapes=[pltpu.VMEM((B_HEADS, tq, LANES), F32), pltpu.VMEM((B_HEADS, tq, LANES), F32),
                        pltpu.VMEM((tq, 512), F32)],
        compiler_params=_cp("arbitrary", "arbitrary", "arbitrary"),
        name="fox",
    )(qa, ka, bv)


def _foxdec_kernel(pt_ref, q_ref, kn_ref, vn_ref, lfn_ref, cum_ref, *rest, pp, dec):
    kp = rest[0:pp]
    vp = rest[pp:2 * pp]
    lp = rest[2 * pp:3 * pp]
    o_ref = rest[3 * pp]
    qr_ref, m_ref, l_ref, r_ref, acc_ref = rest[3 * pp + 1:]
    step = pl.program_id(1)
    nrow = dec * B_HEADS
    rowi = lax.broadcasted_iota(jnp.int32, (nrow, 512), 0)
    coli = lax.broadcasted_iota(jnp.int32, (nrow, 512), 1)
    bd = _idiv(coli, HD) == _imod(rowi, B_HEADS)

    @pl.when(step == 0)
    def _():
        q = q_ref[0]
        qrows = jnp.concatenate(
            [jnp.broadcast_to(q[t:t + 1, :], (B_HEADS, 512)) for t in range(dec)], axis=0)
        qr_ref[...] = jnp.where(bd, qrows, 0.0).astype(BF16)
        m_ref[...] = jnp.full_like(m_ref, NEG)
        l_ref[...] = jnp.zeros_like(l_ref)
        r_ref[...] = jnp.zeros_like(r_ref)
        acc_ref[...] = jnp.zeros_like(acc_ref)

    def pages(krefs, vrefs, lrefs, causal):
        n = len(krefs)
        qr = qr_ref[...]
        ss = [_dot(qr, kr().astype(BF16)) for kr in krefs]
        cums = [_exact_right(lr(), cum_ref[...]) for lr in lrefs]
        r = r_ref[...]
        logits = []
        for u in range(n):
            w = jnp.concatenate([cums[u][:, 0:LANES]] * dec, axis=0)
            logit = ss[u] - (r + w)
            if causal:
                key = lax.broadcasted_iota(jnp.int32, (nrow, LANES), 1)
                trow = _idiv(lax.broadcasted_iota(jnp.int32, (nrow, LANES), 0), B_HEADS)
                logit = jnp.where(key <= trow, logit, NEG)
            logits.append(logit)
            r = r + jnp.concatenate([cums[u][:, LANES:2 * LANES]] * dec, axis=0)
        r_ref[...] = r
        m_old = m_ref[...]
        m_new = m_old
        for lg in logits:
            m_new = jnp.maximum(m_new, jnp.max(lg, axis=1, keepdims=True))
        ps = [jnp.exp(lg - m_new) for lg in logits]
        alpha = jnp.exp(m_old - m_new)
        lsum = jnp.sum(ps[0], axis=1, keepdims=True)
        for p in ps[1:]:
            lsum = lsum + jnp.sum(p, axis=1, keepdims=True)
        l_ref[...] = alpha * l_ref[...] + lsum
        m_ref[...] = m_new
        pv = _dot_nt(ps[0].astype(BF16), vrefs[0]().astype(BF16))
        for u in range(1, n):
            pv = pv + _dot_nt(ps[u].astype(BF16), vrefs[u]().astype(BF16))
        acc_ref[...] = alpha[:, 0:1] * acc_ref[...] + pv

    pages([lambda u=u: kp[u][...] for u in range(pp)], [lambda u=u: vp[u][...] for u in range(pp)],
          [lambda u=u: lp[u][...] for u in range(pp)], False)

    @pl.when(step == pl.num_programs(1) - 1)
    def _():
        pages([lambda: kn_ref[0]], [lambda: vn_ref[0]], [lambda: lfn_ref[0]], True)
        res = jnp.where(bd, acc_ref[...] / l_ref[:, 0:1], 0.0)
        for t in range(dec):
            o_ref[0, t:t + 1, :] = jnp.sum(res[t * B_HEADS:(t + 1) * B_HEADS], axis=0, keepdims=True)


def _foxdec(layer, page_table, q, knew, vnew, lfn_t, cache_k, cache_v, cache_lft, pp):
    ns, dec, _ = q.shape
    n_pages = page_table.shape[1]
    page = cache_k.shape[3]
    nsteps = n_pages // pp
    nrow = dec * B_HEADS
    cum = np.concatenate([np.triu(np.ones((page, page), np.float32)), np.ones((page, page), np.float32)], axis=1)
    seq3 = lambda s, p, pt: (s, 0, 0)
    fix2 = lambda s, p, pt: (0, 0)

    def cache_map(u):
        return lambda s, p, pt: (layer, pt[s, p * pp + u], 0, 0)

    in_specs = [pl.BlockSpec((1, dec, 512), seq3), pl.BlockSpec((1, 512, page), seq3),
                pl.BlockSpec((1, 512, page), seq3), pl.BlockSpec((1, B_HEADS, page), seq3),
                pl.BlockSpec((page, 2 * page), fix2)]
    in_specs += [pl.BlockSpec((None, None, 512, page), cache_map(u)) for u in range(pp)]
    in_specs += [pl.BlockSpec((None, None, 512, page), cache_map(u)) for u in range(pp)]
    in_specs += [pl.BlockSpec((None, None, B_HEADS, page), cache_map(u)) for u in range(pp)]
    grid_spec = pltpu.PrefetchScalarGridSpec(
        num_scalar_prefetch=1,
        grid=(ns, nsteps),
        in_specs=in_specs,
        out_specs=pl.BlockSpec((1, dec, 512), seq3),
        scratch_shapes=[pltpu.VMEM((nrow, 512), BF16), pltpu.VMEM((nrow, LANES), F32),
                        pltpu.VMEM((nrow, LANES), F32), pltpu.VMEM((nrow, LANES), F32),
                        pltpu.VMEM((nrow, 512), F32)],
    )
    return pl.pallas_call(
        functools.partial(_foxdec_kernel, pp=pp, dec=dec),
        grid_spec=grid_spec,
        out_shape=jax.ShapeDtypeStruct((ns, dec, 512), F32),
        compiler_params=_cp("arbitrary", "arbitrary"),
        name="foxdec",
    )(page_table, q, knew, vnew, lfn_t, _const(cum),
      *([cache_k] * pp), *([cache_v] * pp), *([cache_lft] * pp))


def _hgrn_kernel(*refs, layer, ns, cps, valid, has_init):
    (q_ref, f_ref, v_ref, gate_ref, lbraw_ref, ng_ref, tril_ref, rs_ref, tile_ref, hs_ref) = refs[:10]
    pos = 10
    s0_ref = None
    if has_init:
        s0_ref = refs[pos]
        pos += 1
    o_ref, sout_ref, st_ref, oin_ref, oint_ref, tmp_ref = refs[pos:pos + 6]
    i = pl.program_id(1)
    tb = q_ref.shape[0]
    nchunk = tb // SUB
    width = A_HEADS * HD

    lbraw = lbraw_ref[...]
    e = jnp.exp(lbraw - jnp.max(lbraw, axis=0, keepdims=True))
    prob = e / jnp.sum(e, axis=0, keepdims=True)
    lb = jnp.zeros((1, width), F32)
    for d in range(1, layer + 1):
        lb = lb + prob[d:d + 1, :]

    fr = f_ref[...]
    qr = q_ref[...]
    vr = v_ref[...]
    logf = jnp.log(lb + (1.0 - lb) * _sigmoid(fr))
    key = (1.0 - lb) * _sigmoid(-fr)
    if valid < SUB:
        rowv = _imod(lax.broadcasted_iota(jnp.int32, (tb, width), 0), SUB) < valid
        logf = jnp.where(rowv, logf, 0.0)
        key = jnp.where(rowv, key, 0.0)
    q = (qr * _sigmoid(qr)) * (HD ** -0.5)
    g = _exact_left(tril_ref[...], logf)
    g3 = g.reshape(nchunk, SUB, width)
    q3 = q.reshape(nchunk, SUB, width)
    k3 = key.reshape(nchunk, SUB, width)

    att = jnp.zeros((tb, A_HEADS * LANES), F32)
    for s in range(SUB):
        d = jnp.minimum(g3 - g3[:, s:s + 1, :], 0.0)
        p = (q3 * jnp.exp(d)) * k3[:, s:s + 1, :]
        att = att + _dot(p.reshape(tb, width).astype(BF16), rs_ref[s])

    rows = lax.broadcasted_iota(jnp.int32, (tb, tb), 0)
    cols = lax.broadcasted_iota(jnp.int32, (tb, tb), 1)
    causal = (_idiv(rows, SUB) == _idiv(cols, SUB)) & (cols <= rows)
    lane = lax.broadcasted_iota(jnp.int32, (tb, LANES), 1)
    for pr in range(2):
        vpair = vr[:, pr * LANES:(pr + 1) * LANES]
        acc = None
        for hh in range(2):
            h = 2 * pr + hh
            full = _dot(att[:, h * LANES:(h + 1) * LANES].astype(BF16), tile_ref[...])
            full = jnp.where(causal, full, 0.0).astype(BF16)
            vm = jnp.where((lane < HD) == (hh == 0), vpair, 0.0).astype(BF16)
            term = _dot(full, vm)
            acc = term if acc is None else acc + term
        oin_ref[:, pr * LANES:(pr + 1) * LANES] = acc

    gend3 = g3[:, SUB - 1:SUB, :]
    kd = (k3 * jnp.exp(gend3 - g3)).reshape(tb, width).astype(BF16)
    eg = jnp.exp(g)
    qg = (q * eg).astype(BF16)
    r128 = lax.broadcasted_iota(jnp.int32, (LANES, LANES), 0)
    c128 = lax.broadcasted_iota(jnp.int32, (LANES, LANES), 1)
    bdm = _idiv(r128, HD) == _idiv(c128, HD)
    vb = vr.astype(BF16)
    last = i == pl.num_programs(1) - 1
    units = [(sq, pr) for sq in range(ns) for pr in range(2)]
    rowsl = lambda sq, c: slice((sq * cps + c) * SUB, (sq * cps + c + 1) * SUB)
    lanesl = lambda pr: slice(pr * LANES, (pr + 1) * LANES)
    incs = {}
    for sq, pr in units:
        for c in range(cps):
            u = _dot_tn(vb[rowsl(sq, c), lanesl(pr)], kd[rowsl(sq, c), lanesl(pr)])
            incs[sq, pr, c] = jnp.where(bdm, u, 0.0)
    if has_init:
        tmp_ref[...] = jnp.zeros_like(tmp_ref)
        for sq, pr in units:
            tmp_ref[2 * sq + pr, 0:HD, 0:HD] = s0_ref[sq, 2 * pr]
            tmp_ref[2 * sq + pr, HD:LANES, HD:LANES] = s0_ref[sq, 2 * pr + 1]
    seen = {}
    final = {}
    for sq, pr in units:
        if has_init:
            st = tmp_ref[2 * sq + pr].T
        else:
            st = jnp.where(i == 0, 0.0, st_ref[pr])
        for c in range(cps):
            seen[sq, pr, c] = st.astype(BF16)
            r_end = (sq * cps + c + 1) * SUB
            st = st * eg[r_end - 1:r_end, lanesl(pr)] + incs[sq, pr, c]
        if not has_init:
            st_ref[pr] = st
        final[sq, pr] = st
    for sq, pr in units:
        for c in range(cps):
            oint_ref[rowsl(sq, c), lanesl(pr)] = _dot_nt(qg[rowsl(sq, c), lanesl(pr)], seen[sq, pr, c])

    @pl.when(last)
    def _():
        for sq, pr in units:
            tmp_ref[2 * sq + pr] = final[sq, pr].T
        for sq, pr in units:
            sout_ref[sq, 2 * pr] = tmp_ref[2 * sq + pr, 0:HD, 0:HD]
            sout_ref[sq, 2 * pr + 1] = tmp_ref[2 * sq + pr, HD:LANES, HD:LANES]

    o = oin_ref[...] + oint_ref[...]
    gr = gate_ref[...]
    o = (o * lax.rsqrt(_head_ssq(o, hs_ref[...]) * (1.0 / HD) + EPS)) * ng_ref[...]
    o_ref[...] = o * (gr * _sigmoid(gr))


def _hgrn(a, lbraw, ng, s0, layer, n_outer, n_inner, tb, ns, cps, valid):
    rows = a.shape[0]
    width = A_HEADS * HD
    has_init = s0 is not None
    nseq = n_outer * ns
    rs = np.zeros((SUB, width, A_HEADS * LANES), np.float32)
    for s in range(SUB):
        for h in range(A_HEADS):
            rs[s, h * HD + np.arange(HD), h * LANES + s] = 1.0
    tile = np.zeros((LANES, tb), np.float32)
    for s in range(SUB):
        tile[s, np.arange(tb // SUB) * SUB + s] = 1.0

    def col(j):
        return lambda o, i: (o * n_inner + i, j)

    fix2 = lambda o, i: (0, 0)
    fix3 = lambda o, i: (0, 0, 0)
    in_specs = [pl.BlockSpec((tb, width), col(j)) for j in range(4)]
    in_specs += [pl.BlockSpec(lbraw.shape, fix2), pl.BlockSpec((1, width), fix2),
                 pl.BlockSpec((tb, tb), fix2), pl.BlockSpec((SUB, width, A_HEADS * LANES), fix3),
                 pl.BlockSpec((LANES, tb), fix2), pl.BlockSpec((width, width), fix2)]
    args = [a, a, a, a, lbraw, ng, _const(_chunk_tril(tb, SUB)), _const(rs), _const(tile),
            _const(_head_sum_matrix(width))]
    if has_init:
        in_specs.append(pl.BlockSpec((ns, A_HEADS, HD, HD), lambda o, i: (o, 0, 0, 0)))
        args.append(s0)
    return pl.pallas_call(
        functools.partial(_hgrn_kernel, layer=layer, ns=ns, cps=cps, valid=valid, has_init=has_init),
        grid=(n_outer, n_inner),
        in_specs=in_specs,
        out_specs=[pl.BlockSpec((tb, width), lambda o, i: (o * n_inner + i, 0)),
                   pl.BlockSpec((ns, A_HEADS, HD, HD), lambda o, i: (o, 0, 0, 0))],
        out_shape=[jax.ShapeDtypeStruct((rows, width), F32),
                   jax.ShapeDtypeStruct((nseq, A_HEADS, HD, HD), F32)],
        scratch_shapes=[pltpu.VMEM((2, LANES, LANES), F32), pltpu.VMEM((tb, width), F32),
                        pltpu.VMEM((tb, width), F32), pltpu.VMEM((2 * ns, LANES, LANES), F32)],
        compiler_params=_cp("arbitrary", "arbitrary"),
        name="hgrn",
    )(*args)


def _gdn_kernel(*refs, ns, cps, chunk, has_init):
    (x_ref, z_ref, g_ref, cw_ref, ng_ref, tril_ref, hs_ref, eg_ref, eb_ref, eg2_ref, eb2_ref) = refs[:11]
    pos = 11
    cinit_ref = s0_ref = None
    if has_init:
        cinit_ref, s0_ref = refs[pos:pos + 2]
        pos += 2
    o_ref, sout_ref, st_ref, cv_ref, oacc_ref, vn_ref, tmp_ref = refs[pos:pos + 7]
    i = pl.program_id(1)
    tb = x_ref.shape[0]
    width = C_HEADS * HD
    rows_seq = cps * chunk
    last = i == pl.num_programs(1) - 1

    x = x_ref[...]
    cw = cw_ref[...]
    acts = []
    for sq in range(ns):
        xs = x[sq * rows_seq:(sq + 1) * rows_seq]
        if has_init:
            prev = cinit_ref[sq]
        else:
            prev = jnp.where(i == 0, 0.0, cv_ref[...])
        xc = jnp.concatenate([prev, xs], axis=0)
        conv = xc * cw[CONV_W - 1:CONV_W, :]
        for j in range(1, CONV_W):
            conv = conv + pltpu.roll(xc, j, 0) * cw[CONV_W - 1 - j:CONV_W - j, :]
        acts.append(conv[8:])
        if not has_init:
            cv_ref[...] = xs[rows_seq - 8:]
    conv = acts[0] if ns == 1 else jnp.concatenate(acts, axis=0)
    act = conv * _sigmoid(conv)
    hs = hs_ref[...]
    q = act[:, 0:width]
    k = act[:, width:2 * width]
    v = act[:, 2 * width:3 * width]
    q = (q * lax.rsqrt(_head_ssq(q, hs) + EPS)) * (HD ** -0.5)
    k = k * lax.rsqrt(_head_ssq(k, hs) + EPS)

    gt = g_ref[...]
    gc = _exact_left(tril_ref[...], gt)
    gexp = _exact_right(gc, eg_ref[...])
    bexp = _exact_right(gt, eb_ref[...])
    gcol = _exact_right(gc, eg2_ref[...])
    bcol = _exact_right(gt, eb2_ref[...])
    nchunk = tb // chunk
    gexp3 = gexp.reshape(nchunk, chunk, width)
    gend3 = gexp3[:, chunk - 1:chunk, :]
    eg = jnp.exp(gexp)
    rhs = jnp.concatenate([v * bexp, (k * bexp) * eg], axis=1).astype(BF16)
    qg = (q * eg).astype(BF16)
    kd = (k.reshape(nchunk, chunk, width) * jnp.exp(gend3 - gexp3)).reshape(tb, width).astype(BF16)
    kb = k.astype(BF16)
    lane = lax.broadcasted_iota(jnp.int32, (tb, width), 1)
    kmask = [jnp.where(_idiv(lane, HD) == h, k, 0.0).astype(BF16) for h in range(C_HEADS)]
    qmask = [jnp.where(_idiv(lane, HD) == h, q, 0.0).astype(BF16) for h in range(C_HEADS)]

    rc = lax.broadcasted_iota(jnp.int32, (tb, tb), 0)
    cc = lax.broadcasted_iota(jnp.int32, (tb, tb), 1)
    same = _idiv(rc, chunk) == _idiv(cc, chunk)
    eye = rc == cc
    incl = same & (cc <= rc)
    strict = same & (cc < rc)
    hl = _idiv(lane, HD)
    r256 = lax.broadcasted_iota(jnp.int32, (width, width), 0)
    c256 = lax.broadcasted_iota(jnp.int32, (width, width), 1)
    bdm = _idiv(r256, HD) == _idiv(c256, HD)
    nstage = int(math.log2(chunk))
    heads = range(C_HEADS)
    reps = tb // LANES
    gct = gc.T
    decay, xm, tm, qkd = [], [], [], []
    for h in heads:
        g_t = jnp.concatenate([gcol[:, h * LANES:(h + 1) * LANES]] * reps, axis=1)
        decay.append(jnp.exp(jnp.minimum(g_t - gct[8 + h:9 + h, :], 0.0)))
    kk = [_dot_nt(kmask[h], kb) for h in heads]
    qk = [_dot_nt(qmask[h], kb) for h in heads]
    for h in heads:
        b_t = jnp.concatenate([bcol[:, h * LANES:(h + 1) * LANES]] * reps, axis=1)
        x0 = -jnp.where(strict, (b_t * kk[h]) * decay[h], 0.0)
        xm.append(x0)
        tm.append(jnp.where(eye, 1.0, 0.0) + x0)
        qkd.append(jnp.where(incl, qk[h] * decay[h], 0.0).astype(BF16))
    for _ in range(nstage - 1):
        xb = [x.astype(BF16) for x in xm]
        xm = [_dot(xb[h], xb[h]) for h in heads]
        tm = [tm[h] + _dot(tm[h].astype(BF16), xm[h].astype(BF16)) for h in heads]
    sol = [_dot(tm[h].astype(BF16), rhs) for h in heads]
    u = jnp.zeros((tb, width), F32)
    w = jnp.zeros((tb, width), F32)
    for h in heads:
        u = u + jnp.where(hl == h, sol[h][:, 0:width], 0.0)
        w = w + jnp.where(hl == h, sol[h][:, width:2 * width], 0.0)
    wb = w.astype(BF16)

    if has_init:
        tmp_ref[...] = jnp.zeros_like(tmp_ref)
        for sq in range(ns):
            for h in heads:
                tmp_ref[sq, h * HD:(h + 1) * HD, h * HD:(h + 1) * HD] = s0_ref[sq, h]
        sts = [tmp_ref[sq] for sq in range(ns)]
    else:
        sts = [jnp.where(i == 0, 0.0, st_ref[...])]
    rowsl = lambda sq, c: slice((sq * cps + c) * chunk, (sq * cps + c + 1) * chunk)
    for c in range(cps):
        stb = [st.astype(BF16) for st in sts]
        ws = [_dot(wb[rowsl(sq, c)], stb[sq]) for sq in range(ns)]
        for sq in range(ns):
            oacc_ref[rowsl(sq, c), :] = _dot(qg[rowsl(sq, c)], stb[sq])
        vns = [(u[rowsl(sq, c)] - ws[sq]).astype(BF16) for sq in range(ns)]
        for sq in range(ns):
            vn_ref[rowsl(sq, c), :] = vns[sq]
        incs = [jnp.where(bdm, _dot_tn(kd[rowsl(sq, c)], vns[sq]), 0.0) for sq in range(ns)]
        sts = [sts[sq] * eg[(sq * cps + c + 1) * chunk - 1:(sq * cps + c + 1) * chunk, :] + incs[sq]
               for sq in range(ns)]
    if not has_init:
        st_ref[...] = sts[0]

    @pl.when(last)
    def _():
        for sq in range(ns):
            tmp_ref[sq] = sts[sq]
        for sq in range(ns):
            for h in heads:
                sout_ref[sq, h] = tmp_ref[sq, h * HD:(h + 1) * HD, h * HD:(h + 1) * HD]

    vnb = vn_ref[...]
    o = oacc_ref[...]
    for h in heads:
        o = o + jnp.where(hl == h, _dot(qkd[h], vnb), 0.0)
    z = z_ref[...]
    o = (o * lax.rsqrt(_head_ssq(o, hs) * (1.0 / HD) + EPS)) * ng_ref[...]
    o_ref[...] = o * (z * _sigmoid(z))


def _gdn(cfull, g, cw, ng, cinit, s0, n_outer, n_inner, tb, ns, cps, chunk):
    rows = cfull.shape[0]
    width = C_HEADS * HD
    has_init = s0 is not None
    nseq = n_outer * ns
    eg = np.zeros((LANES, width), np.float32)
    eb = np.zeros((LANES, width), np.float32)
    eg2 = np.zeros((LANES, C_HEADS * LANES), np.float32)
    eb2 = np.zeros((LANES, C_HEADS * LANES), np.float32)
    for h in range(C_HEADS):
        eg[8 + h, h * HD:(h + 1) * HD] = 1.0
        eb[12 + h, h * HD:(h + 1) * HD] = 1.0
        eg2[8 + h, h * LANES:(h + 1) * LANES] = 1.0
        eb2[12 + h, h * LANES:(h + 1) * LANES] = 1.0
    rowm = lambda o, i: (o * n_inner + i, 0)
    fix2 = lambda o, i: (0, 0)
    in_specs = [pl.BlockSpec((tb, 3 * width), rowm),
                pl.BlockSpec((tb, width), lambda o, i: (o * n_inner + i, 3)),
                pl.BlockSpec((tb, LANES), rowm), pl.BlockSpec((CONV_W, 3 * width), fix2),
                pl.BlockSpec((1, width), fix2), pl.BlockSpec((tb, tb), fix2),
                pl.BlockSpec((width, width), fix2), pl.BlockSpec((LANES, width), fix2),
                pl.BlockSpec((LANES, width), fix2), pl.BlockSpec((LANES, C_HEADS * LANES), fix2),
                pl.BlockSpec((LANES, C_HEADS * LANES), fix2)]
    args = [cfull, cfull, g, cw, ng, _const(_chunk_tril(tb, chunk)), _const(_head_sum_matrix(width)),
            _const(eg), _const(eb), _const(eg2), _const(eb2)]
    if has_init:
        in_specs += [pl.BlockSpec((ns, 8, 3 * width), lambda o, i: (o, 0, 0)),
                     pl.BlockSpec((ns, C_HEADS, HD, HD), lambda o, i: (o, 0, 0, 0))]
        args += [cinit, s0]
    return pl.pallas_call(
        functools.partial(_gdn_kernel, ns=ns, cps=cps, chunk=chunk, has_init=has_init),
        grid=(n_outer, n_inner),
        in_specs=in_specs,
        out_specs=[pl.BlockSpec((tb, width), rowm),
                   pl.BlockSpec((ns, C_HEADS, HD, HD), lambda o, i: (o, 0, 0, 0))],
        out_shape=[jax.ShapeDtypeStruct((rows, width), F32),
                   jax.ShapeDtypeStruct((nseq, C_HEADS, HD, HD), F32)],
        scratch_shapes=[pltpu.VMEM((width, width), F32), pltpu.VMEM((8, 3 * width), F32),
                        pltpu.VMEM((tb, width), F32), pltpu.VMEM((tb, width), BF16),
                        pltpu.VMEM((ns, width, width), F32)],
        compiler_params=_cp("arbitrary", "arbitrary"),
        name="gdn",
    )(*args)


def _post_kernel(x_ref, oa_ref, ob_ref, oc_ref, wo_ref, ln_ref, wr_ref, br_ref,
                 xn_ref, hf_ref, gate_ref):
    x = x_ref[...]
    mix = _dot(oa_ref[...].astype(BF16), wo_ref[0:256, :])
    mix = mix + _dot(ob_ref[...].astype(BF16), wo_ref[256:768, :])
    mix = mix + _dot(oc_ref[...].astype(BF16), wo_ref[768:1024, :])
    xn = x + mix
    xn_ref[...] = xn
    ms = jnp.mean(xn * xn, axis=-1, keepdims=True)
    hf = ((xn * lax.rsqrt(ms + EPS)) * ln_ref[...]).astype(BF16)
    hf_ref[...] = hf
    logits = _dot(hf, wr_ref[...]) + br_ref[...]
    lane = lax.broadcasted_iota(jnp.int32, logits.shape, 1)
    big = jnp.int32(1 << 20)
    isg = lane < N_GROUPS
    gl = jnp.where(isg, logits, NEG)
    gm = jnp.max(gl, axis=1, keepdims=True)
    gidx = jnp.min(jnp.where(isg & (gl == gm), lane, big), axis=1, keepdims=True)
    top_gp = 1.0 / jnp.sum(jnp.where(isg, jnp.exp(gl - gm), 0.0), axis=1, keepdims=True)
    eidx = lane - N_GROUPS
    ing = (eidx >= 0) & (eidx < N_EXPERTS) & (_idiv(eidx, EXPERTS_PER_GROUP) == gidx)
    el = jnp.where(ing, logits, NEG)
    em = jnp.max(el, axis=1, keepdims=True)
    ee = jnp.where(ing, jnp.exp(el - em), 0.0)
    prob = ee / jnp.sum(ee, axis=1, keepdims=True)
    p1 = jnp.max(prob, axis=1, keepdims=True)
    i1 = jnp.min(jnp.where(ing & (prob == p1), lane, big), axis=1, keepdims=True)
    rest = jnp.where(ing & (lane != i1), prob, -1.0)
    p2 = jnp.max(rest, axis=1, keepdims=True)
    i2 = jnp.min(jnp.where(ing & (lane != i1) & (rest == p2), lane, big), axis=1, keepdims=True)
    den = p1 + p2
    gsel = jnp.where(lane == i1, (top_gp * p1) / den, jnp.where(lane == i2, (top_gp * p2) / den, 0.0))
    sh_r = lax.broadcasted_iota(jnp.int32, (LANES, LANES), 0)
    sh_c = lax.broadcasted_iota(jnp.int32, (LANES, LANES), 1)
    shift = jnp.where(sh_r == sh_c + N_GROUPS, 1.0, 0.0).astype(BF16)
    gate_ref[...] = _exact_right(gsel, shift)


def _post(x, oa, ob, oc, wo, ln, wr, br, tm):
    t = x.shape[0]
    row = lambda i: (i, 0)
    fix = lambda i: (0, 0)
    return pl.pallas_call(
        _post_kernel,
        grid=(t // tm,),
        in_specs=[pl.BlockSpec((tm, D_MODEL), row), pl.BlockSpec((tm, 256), row),
                  pl.BlockSpec((tm, 512), row), pl.BlockSpec((tm, 256), row),
                  pl.BlockSpec((D_MODEL, D_MODEL), fix), pl.BlockSpec((1, D_MODEL), fix),
                  pl.BlockSpec((D_MODEL, LANES), fix), pl.BlockSpec((1, LANES), fix)],
        out_specs=[pl.BlockSpec((tm, D_MODEL), row), pl.BlockSpec((tm, D_MODEL), row),
                   pl.BlockSpec((tm, LANES), row)],
        out_shape=[jax.ShapeDtypeStruct((t, D_MODEL), F32), jax.ShapeDtypeStruct((t, D_MODEL), BF16),
                   jax.ShapeDtypeStruct((t, LANES), F32)],
        compiler_params=_cp("arbitrary"),
        name="post",
    )(x, oa, ob, oc, wo, ln, wr, br)


def _moe_kernel(h_ref, gate_ref, x_ref, wg_ref, wu_ref, wd_ref, o_ref, acc_ref):
    e = pl.program_id(1)

    @pl.when(e == 0)
    def _():
        acc_ref[...] = x_ref[...]

    h = h_ref[...]
    gates = gate_ref[...]
    lane = lax.broadcasted_iota(jnp.int32, gates.shape, 1)
    ge = jnp.sum(jnp.where(lane == e, gates, 0.0), axis=1, keepdims=True)
    gp = _dot(h, wg_ref[0])
    up = _dot(h, wu_ref[0])
    a = ((gp * _sigmoid(gp)) * up) * ge
    acc_ref[...] += _dot(a.astype(BF16), wd_ref[0])

    @pl.when(e == pl.num_programs(1) - 1)
    def _():
        o_ref[...] = acc_ref[...]


def _moe(hf, gates, xn, wg, wu, wd, tm):
    t = hf.shape[0]
    ne = wg.shape[0]
    row = lambda i, e: (i, 0)
    exp3 = lambda i, e: (e, 0, 0)
    return pl.pallas_call(
        _moe_kernel,
        grid=(t // tm, ne),
        in_specs=[pl.BlockSpec((tm, D_MODEL), row), pl.BlockSpec((tm, LANES), row),
                  pl.BlockSpec((tm, D_MODEL), row),
                  pl.BlockSpec((1, D_MODEL, D_EXPERT), exp3), pl.BlockSpec((1, D_MODEL, D_EXPERT), exp3),
                  pl.BlockSpec((1, D_EXPERT, D_MODEL), exp3)],
        out_specs=pl.BlockSpec((tm, D_MODEL), row),
        out_shape=jax.ShapeDtypeStruct((t, D_MODEL), F32),
        scratch_shapes=[pltpu.VMEM((tm, D_MODEL), F32)],
        compiler_params=_cp("arbitrary", "arbitrary"),
        name="moe",
    )(hf, gates, xn, wg, wu, wd)


def _pad_rows(a, nseq, dec):
    c = a.shape[-1]
    return jnp.pad(a.reshape(nseq, dec, c), ((0, 0), (0, SUB - dec), (0, 0))).reshape(nseq * SUB, c)


def _layer_params(l, ln_mix, w_in, hgrn_norm, fox_bf, fox_qnorm, fox_knorm, gdn_conv, gdn_a_log,
                  gdn_dt_bias, gdn_norm, w_out, ln_ffn, w_group, b_group, w_router, b_router,
                  w_gate, w_up, w_down):
    w = w_in[l]
    gates = jnp.concatenate([w[:, 2560:2568], w[:, 3592:3600]], axis=1)
    wp = jnp.concatenate([w[:, 0:2560], w[:, 2568:3592], gates,
                          jnp.zeros((D_MODEL, LANES - 16), F32)], axis=1).astype(BF16)
    p1 = jnp.zeros((1, LANES), F32).at[0, 0:8].set(fox_bf[l]).at[0, 8:12].set(gdn_dt_bias[l])
    p2 = jnp.zeros((1, LANES), F32).at[0, 8:12].set(gdn_a_log[l])
    wr = jnp.concatenate([w_group[l], w_router[l],
                          jnp.zeros((D_MODEL, LANES - N_GROUPS - N_EXPERTS), F32)], axis=1).astype(BF16)
    br = jnp.zeros((1, LANES), F32).at[0, 0:N_GROUPS].set(b_group[l])
    br = br.at[0, N_GROUPS:N_GROUPS + N_EXPERTS].set(b_router[l])
    return dict(
        ln_mix=ln_mix[l][None, :], w=wp, p1=p1, p2=p2,
        qg=jnp.tile(fox_qnorm[l], B_HEADS)[None, :], kg=jnp.tile(fox_knorm[l], B_HEADS)[None, :],
        hgrn_ng=jnp.tile(hgrn_norm[l], A_HEADS)[None, :], gdn_ng=jnp.tile(gdn_norm[l], C_HEADS)[None, :],
        conv=gdn_conv[l], w_out=w_out[l].astype(BF16), ln_ffn=ln_ffn[l][None, :], wr=wr, br=br,
        wg=w_gate[l].astype(BF16), wu=w_up[l].astype(BF16), wd=w_down[l].astype(BF16))


def kernel(x_prompt, x_sample, cache_k, cache_v, cache_logf, page_table, state_hgrn, state_gdn, state_conv,
           ln_mix, w_in, hgrn_lb, hgrn_norm, fox_bf, fox_qnorm, fox_knorm, gdn_conv, gdn_a_log, gdn_dt_bias,
           gdn_norm, w_out, ln_ffn, w_group, b_group, w_router, b_router, w_gate, w_up, w_down):
    nb, seq, _ = x_prompt.shape
    nsq, dec, _ = x_sample.shape
    depth = ln_mix.shape[0]
    n_phys, page = cache_k.shape[1], cache_k.shape[2]
    tp = nb * seq
    ts = nsq * dec
    hs512 = _const(_head_sum_matrix(512))
    ck = jnp.transpose(cache_k, (0, 1, 3, 4, 2)).reshape(depth, n_phys, 512, page)
    cv = jnp.transpose(cache_v, (0, 1, 3, 4, 2)).reshape(depth, n_phys, 512, page)
    clt = jnp.swapaxes(cache_logf, 2, 3)
    tbp = min(256, seq)
    gchunk = min(64, seq)
    seq_blk = 16

    yp = x_prompt.reshape(tp, D_MODEL)
    ys = x_sample.reshape(ts, D_MODEL)
    outs_p, outs_s = [], []
    for l in range(depth):
        P = _layer_params(l, ln_mix, w_in, hgrn_norm, fox_bf, fox_qnorm, fox_knorm, gdn_conv, gdn_a_log,
                          gdn_dt_bias, gdn_norm, w_out, ln_ffn, w_group, b_group, w_router, b_router,
                          w_gate, w_up, w_down)

        a, bq, bk, bv, c, g = _proj(yp, P["ln_mix"], P["w"], hs512, P["qg"], P["kg"], P["p1"], P["p2"],
                                    tm=min(256, tp))
        qa, ka = _foxprep(bq, bk, g, nb, tm=min(256, seq))
        ob = _fox(qa, ka, bv, nb, tq=min(256, seq))
        oa, hst = _hgrn(a, hgrn_lb, P["hgrn_ng"], None, l, nb, seq // tbp, tbp, 1, tbp // SUB, SUB)
        oc, gst = _gdn(c, g, P["conv"], P["gdn_ng"], None, None, nb, seq // tbp, tbp, 1,
                       tbp // gchunk, gchunk)
        xn, hf, gates = _post(yp, oa, ob, oc, P["w_out"], P["ln_ffn"], P["wr"], P["br"], tm=min(256, tp))
        yp_new = _moe(hf, gates, xn, P["wg"], P["wu"], P["wd"], tm=min(512, tp))
        outs_p.append((bk.reshape(nb, seq, B_HEADS, HD), bv.reshape(nb, seq, B_HEADS, HD),
                       g[:, 0:8].reshape(nb, seq, B_HEADS), hst, gst,
                       c[:, 0:768].reshape(nb, seq, 768)[:, seq - (CONV_W - 1):, :]))
        yp = yp_new

        a, bq, bk, bv, c, g = _proj(ys, P["ln_mix"], P["w"], hs512, P["qg"], P["kg"], P["p1"], P["p2"],
                                    tm=min(256, ts))
        knew = jnp.pad(jnp.swapaxes(bk.reshape(nsq, dec, 512), 1, 2), ((0, 0), (0, 0), (0, page - dec)))
        vnew = jnp.pad(jnp.swapaxes(bv.reshape(nsq, dec, 512), 1, 2), ((0, 0), (0, 0), (0, page - dec)))
        lfn = jnp.pad(jnp.swapaxes(g[:, 0:8].reshape(nsq, dec, B_HEADS), 1, 2),
                      ((0, 0), (0, 0), (0, page - dec)))
        ob = _foxdec(l, page_table, bq.reshape(nsq, dec, 512), knew, vnew, lfn, ck, cv, clt,
                     pp=min(8, page_table.shape[1])).reshape(ts, 512)
        n_outer = nsq // seq_blk
        oa, hst = _hgrn(_pad_rows(a, nsq, dec), hgrn_lb, P["hgrn_ng"], state_hgrn[l], l,
                        n_outer, 1, seq_blk * SUB, seq_blk, 1, dec)
        cinit = jnp.pad(state_conv[l], ((0, 0), (8 - (CONV_W - 1), 0), (0, 0)))
        oc, gst = _gdn(_pad_rows(c, nsq, dec), _pad_rows(g, nsq, dec), P["conv"], P["gdn_ng"], cinit,
                       state_gdn[l], n_outer, 1, seq_blk * SUB, seq_blk, 1, SUB)
        oa = oa.reshape(nsq, SUB, 256)[:, :dec].reshape(ts, 256)
        oc = oc.reshape(nsq, SUB, 256)[:, :dec].reshape(ts, 256)
        xn, hf, gates = _post(ys, oa, ob, oc, P["w_out"], P["ln_ffn"], P["wr"], P["br"], tm=min(256, ts))
        ys_new = _moe(hf, gates, xn, P["wg"], P["wu"], P["wd"], tm=min(512, ts))
        xpad = jnp.concatenate([state_conv[l], c[:, 0:768].reshape(nsq, dec, 768)], axis=1)
        outs_s.append((bk.reshape(nsq, dec, B_HEADS, HD), bv.reshape(nsq, dec, B_HEADS, HD),
                       g[:, 0:8].reshape(nsq, dec, B_HEADS), hst, gst, xpad[:, -(CONV_W - 1):, :]))
        ys = ys_new

    stack = lambda outs, j: jnp.stack([o[j] for o in outs], axis=0)
    return (yp.reshape(nb, seq, D_MODEL), ys.reshape(nsq, dec, D_MODEL),
            *[stack(outs_p, j) for j in range(6)], *[stack(outs_s, j) for j in range(6)])
```

```python
import functools
import math

import numpy as np
import jax
import jax.numpy as jnp
from jax import lax
from jax.experimental import pallas as pl
from jax.experimental.pallas import tpu as pltpu

F32 = jnp.float32
BF16 = jnp.bfloat16
EPS = 1e-6
NEG = -1e30

D_MODEL = 1024
HD = 64
A_HEADS = 4
B_HEADS = 8
C_HEADS = 4
CONV_W = 4
N_GROUPS = 4
EXPERTS_PER_GROUP = 8
N_EXPERTS = N_GROUPS * EXPERTS_PER_GROUP
D_EXPERT = D_MODEL // 4
SUB = 16
LANES = 128
VMEM_LIMIT = 56 * 1024 * 1024


def _cp(*sem):
    return pltpu.CompilerParams(dimension_semantics=sem, vmem_limit_bytes=VMEM_LIMIT)


def _dot(a, b):
    return jnp.dot(a, b, preferred_element_type=F32)


def _dot_nt(a, b):
    return lax.dot_general(a, b, (((1,), (1,)), ((), ())), preferred_element_type=F32)


def _dot_tn(a, b):
    return lax.dot_general(a, b, (((0,), (0,)), ((), ())), preferred_element_type=F32)


def _split3(x):
    hi = x.astype(BF16)
    r = x - hi.astype(F32)
    mid = r.astype(BF16)
    lo = (r - mid.astype(F32)).astype(BF16)
    return hi, mid, lo


def _exact_left(m, x):
    hi, mid, lo = _split3(x)
    return (_dot(m, hi) + _dot(m, mid)) + _dot(m, lo)


def _exact_right(x, m):
    hi, mid, lo = _split3(x)
    return (_dot(hi, m) + _dot(mid, m)) + _dot(lo, m)


def _idiv(x, n):
    return jnp.right_shift(x, int(math.log2(n)))


def _imod(x, n):
    return jnp.bitwise_and(x, n - 1)


def _sigmoid(x):
    return 1.0 / (1.0 + jnp.exp(-x))


def _softplus(z):
    return jnp.maximum(z, 0.0) + jnp.log(1.0 + jnp.exp(-jnp.abs(z)))


def _head_ssq(z, hs):
    zz = z * z
    hi = zz.astype(BF16)
    lo = (zz - hi.astype(F32)).astype(BF16)
    return _dot(hi, hs) + _dot(lo, hs)


def _const(a, dtype=BF16):
    return jnp.asarray(a, dtype=dtype)


def _head_sum_matrix(width):
    i = np.arange(width)
    return (i[:, None] // HD == i[None, :] // HD).astype(np.float32)


def _chunk_tril(n, c):
    i = np.arange(n)
    return ((i[:, None] // c == i[None, :] // c) & (i[None, :] <= i[:, None])).astype(np.float32)


PROJ_COLS = 3712


def _proj_kernel(x_ref, ln_ref, w_ref, hs_ref, qg_ref, kg_ref, p1_ref, p2_ref,
                 a_ref, bq_ref, bk_ref, bv_ref, c_ref, g_ref):
    x = x_ref[...]
    ms = jnp.mean(x * x, axis=-1, keepdims=True)
    hn = ((x * lax.rsqrt(ms + EPS)) * ln_ref[...]).astype(BF16)
    a_ref[...] = _dot(hn, w_ref[:, 0:1024])
    hs = hs_ref[...]
    q = _dot(hn, w_ref[:, 1024:1536])
    bq_ref[...] = ((q * lax.rsqrt(_head_ssq(q, hs) * (1.0 / HD) + EPS)) * qg_ref[...]) * (HD ** -0.5)
    k = _dot(hn, w_ref[:, 1536:2048])
    bk_ref[...] = (k * lax.rsqrt(_head_ssq(k, hs) * (1.0 / HD) + EPS)) * kg_ref[...]
    bv_ref[...] = _dot(hn, w_ref[:, 2048:2560])
    c_ref[...] = _dot(hn, w_ref[:, 2560:3584])
    gr = _dot(hn, w_ref[:, 3584:3712])
    lane = lax.broadcasted_iota(jnp.int32, gr.shape, 1)
    z = gr + p1_ref[...]
    sp = _softplus(z)
    logsig = jnp.minimum(z, 0.0) - jnp.log(1.0 + jnp.exp(-jnp.abs(z)))
    glog = -jnp.exp(p2_ref[...]) * sp
    beta = _sigmoid(gr)
    g_ref[...] = jnp.where(lane < 8, logsig,
                           jnp.where(lane < 12, glog, jnp.where(lane < 16, beta, 0.0)))


def _proj(x, ln, w, hs512, qg, kg, p1, p2, tm):
    t = x.shape[0]
    row = lambda i: (i, 0)
    fix = lambda i: (0, 0)
    outs = [jax.ShapeDtypeStruct((t, n), F32) for n in (1024, 512, 512, 512, 1024, 128)]
    return pl.pallas_call(
        _proj_kernel,
        grid=(t // tm,),
        in_specs=[pl.BlockSpec((tm, D_MODEL), row), pl.BlockSpec((1, D_MODEL), fix),
                  pl.BlockSpec((D_MODEL, PROJ_COLS), fix), pl.BlockSpec((512, 512), fix),
                  pl.BlockSpec((1, 512), fix), pl.BlockSpec((1, 512), fix),
                  pl.BlockSpec((1, LANES), fix), pl.BlockSpec((1, LANES), fix)],
        out_specs=[pl.BlockSpec((tm, n), row) for n in (1024, 512, 512, 512, 1024, 128)],
        out_shape=outs,
        compiler_params=_cp("arbitrary"),
        name="proj",
    )(x, ln, w, hs512, qg, kg, p1, p2)


def _foxprep_kernel(bq_ref, bk_ref, g_ref, tril_ref, pq_ref, pcq_ref, pck_ref, oq_ref, ok_ref,
                    qa_ref, ka_ref, carry_ref):
    i = pl.program_id(1)

    @pl.when(i == 0)
    def _():
        carry_ref[...] = jnp.zeros_like(carry_ref)

    g = g_ref[...]
    c = _exact_left(tril_ref[...], g) + carry_ref[...]
    carry_ref[...] = c[-1:, :]
    hi, mid, lo = _split3(c)
    pq = pq_ref[...]
    qa = _dot(bq_ref[...].astype(BF16), pq) + oq_ref[...]
    ka = _dot(bk_ref[...].astype(BF16), pq) + ok_ref[...]
    for j, part in enumerate((hi, mid, lo)):
        qa = qa + _dot(part, pcq_ref[j])
        ka = ka - _dot(part, pck_ref[j])
    qa_ref[...] = qa.astype(BF16)
    ka_ref[...] = ka.astype(BF16)


def _foxprep(bq, bk, g, nb, tm):
    t = bq.shape[0]
    nblk = t // nb // tm
    pq = np.zeros((512, 1024), np.float32)
    for h in range(B_HEADS):
        pq[h * HD + np.arange(HD), h * LANES + np.arange(HD)] = 1.0
    pcq = np.zeros((3, LANES, 1024), np.float32)
    pck = np.zeros((3, LANES, 1024), np.float32)
    oq = np.zeros((1, 1024), np.float32)
    ok = np.zeros((1, 1024), np.float32)
    for h in range(B_HEADS):
        for j in range(3):
            pcq[j, h, h * LANES + HD + j] = 1.0
            pck[j, h, h * LANES + HD + 3 + j] = 1.0
            oq[0, h * LANES + HD + 3 + j] = 1.0
            ok[0, h * LANES + HD + j] = 1.0
    row = lambda b, i: (b * nblk + i, 0)
    fix2 = lambda b, i: (0, 0)
    fix3 = lambda b, i: (0, 0, 0)
    return pl.pallas_call(
        _foxprep_kernel,
        grid=(nb, nblk),
        in_specs=[pl.BlockSpec((tm, 512), row), pl.BlockSpec((tm, 512), row),
                  pl.BlockSpec((tm, LANES), row), pl.BlockSpec((tm, tm), fix2),
                  pl.BlockSpec((512, 1024), fix2), pl.BlockSpec((3, LANES, 1024), fix3),
                  pl.BlockSpec((3, LANES, 1024), fix3), pl.BlockSpec((1, 1024), fix2),
                  pl.BlockSpec((1, 1024), fix2)],
        out_specs=[pl.BlockSpec((tm, 1024), row), pl.BlockSpec((tm, 1024), row)],
        out_shape=[jax.ShapeDtypeStruct((t, 1024), BF16)] * 2,
        scratch_shapes=[pltpu.VMEM((1, LANES), F32)],
        compiler_params=_cp("arbitrary", "arbitrary"),
        name="foxprep",
    )(bq, bk, g, _const(_chunk_tril(tm, tm)), _const(pq), _const(pcq), _const(pck),
      _const(oq, F32), _const(ok, F32))


def _fox_kernel(q_ref, k_ref, v_ref, o_ref, m_ref, l_ref, acc_ref, *, tq):
    qi = pl.program_id(1)
    ki = pl.program_id(2)

    @pl.when(ki == 0)
    def _():
        m_ref[...] = jnp.full_like(m_ref, NEG)
        l_ref[...] = jnp.zeros_like(l_ref)
        acc_ref[...] = jnp.zeros_like(acc_ref)

    @pl.when(ki <= qi)
    def _():
        rows = lax.broadcasted_iota(jnp.int32, (tq, tq), 0)
        cols = lax.broadcasted_iota(jnp.int32, (tq, tq), 1)
        keep = (cols <= rows) | (ki < qi)
        lane = lax.broadcasted_iota(jnp.int32, (tq, LANES), 1)
        left = lane < HD
        for pr in range(B_HEADS // 2):
            vp = v_ref[:, pr * LANES:(pr + 1) * LANES]
            upd = None
            alphas = []
            for hh in range(2):
                h = 2 * pr + hh
                s = _dot_nt(q_ref[:, h * LANES:(h + 1) * LANES], k_ref[:, h * LANES:(h + 1) * LANES])
                s = jnp.where(keep, s, NEG)
                m_old = m_ref[h]
                m_new = jnp.maximum(m_old, jnp.max(s, axis=1, keepdims=True))
                p = jnp.exp(s - m_new[:, 0:1])
                alpha = jnp.exp(m_old - m_new)
                l_ref[h] = alpha * l_ref[h] + jnp.sum(p, axis=1, keepdims=True)
                m_ref[h] = m_new
                vm = jnp.where((lane < HD) == (hh == 0), vp, 0.0).astype(BF16)
                pv = _dot(p.astype(BF16), vm)
                upd = pv if upd is None else upd + pv
                alphas.append(alpha)
            a = jnp.where(left, alphas[0], alphas[1])
            acc_ref[:, pr * LANES:(pr + 1) * LANES] = a * acc_ref[:, pr * LANES:(pr + 1) * LANES] + upd

    @pl.when(ki == qi)
    def _():
        lane = lax.broadcasted_iota(jnp.int32, (tq, LANES), 1)
        for pr in range(B_HEADS // 2):
            l = jnp.where(lane < HD, l_ref[2 * pr], l_ref[2 * pr + 1])
            o_ref[:, pr * LANES:(pr + 1) * LANES] = acc_ref[:, pr * LANES:(pr + 1) * LANES] / l


def _fox(qa, ka, bv, nb, tq):
    t = qa.shape[0]
    nblk = t // nb // tq
    return pl.pallas_call(
        functools.partial(_fox_kernel, tq=tq),
        grid=(nb, nblk, nblk),
        in_specs=[pl.BlockSpec((tq, 1024), lambda b, qi, ki: (b * nblk + qi, 0)),
                  pl.BlockSpec((tq, 1024), lambda b, qi, ki: (b * nblk + jnp.minimum(ki, qi), 0)),
                  pl.BlockSpec((tq, 512), lambda b, qi, ki: (b * nblk + jnp.minimum(ki, qi), 0))],
        out_specs=pl.BlockSpec((tq, 512), lambda b, qi, ki: (b * nblk + qi, 0)),
        out_shape=jax.ShapeDtypeStruct((t, 512), F32),
        scratch_shapes=[pltpu.VMEM((B_HEADS, tq, LANES), F32), pltpu.VMEM((B_HEADS, tq, LANES), F32),
                        pltpu.VMEM((tq, 512), F32)],
        compiler_params=_cp("arbitrary", "arbitrary", "arbitrary"),
        name="fox",
    )(qa, ka, bv)


def _foxdec_kernel(pt_ref, q_ref, kn_ref, vn_ref, lfn_ref, cum_ref, *rest, pp, dec):
    kp = rest[0:pp]
    vp = rest[pp:2 * pp]
    lp = rest[2 * pp:3 * pp]
    o_ref = rest[3 * pp]
    qr_ref, m_ref, l_ref, r_ref, acc_ref = rest[3 * pp + 1:]
    step = pl.program_id(1)
    nrow = dec * B_HEADS
    rowi = lax.broadcasted_iota(jnp.int32, (nrow, 512), 0)
    coli = lax.broadcasted_iota(jnp.int32, (nrow, 512), 1)
    bd = _idiv(coli, HD) == _imod(rowi, B_HEADS)

    @pl.when(step == 0)
    def _():
        q = q_ref[0]
        qrows = jnp.concatenate(
            [jnp.broadcast_to(q[t:t + 1, :], (B_HEADS, 512)) for t in range(dec)], axis=0)
        qr_ref[...] = jnp.where(bd, qrows, 0.0).astype(BF16)
        m_ref[...] = jnp.full_like(m_ref, NEG)
        l_ref[...] = jnp.zeros_like(l_ref)
        r_ref[...] = jnp.zeros_like(r_ref)
        acc_ref[...] = jnp.zeros_like(acc_ref)

    def pages(krefs, vrefs, lrefs, causal):
        n = len(krefs)
        qr = qr_ref[...]
        ss = [_dot(qr, kr().astype(BF16)) for kr in krefs]
        cums = [_exact_right(lr(), cum_ref[...]) for lr in lrefs]
        r = r_ref[...]
        logits = []
        for u in range(n):
            w = jnp.concatenate([cums[u][:, 0:LANES]] * dec, axis=0)
            logit = ss[u] - (r + w)
            if causal:
                key = lax.broadcasted_iota(jnp.int32, (nrow, LANES), 1)
                trow = _idiv(lax.broadcasted_iota(jnp.int32, (nrow, LANES), 0), B_HEADS)
                logit = jnp.where(key <= trow, logit, NEG)
            logits.append(logit)
            r = r + jnp.concatenate([cums[u][:, LANES:2 * LANES]] * dec, axis=0)
        r_ref[...] = r
        m_old = m_ref[...]
        m_new = m_old
        for lg in logits:
            m_new = jnp.maximum(m_new, jnp.max(lg, axis=1, keepdims=True))
        ps = [jnp.exp(lg - m_new) for lg in logits]
        alpha = jnp.exp(m_old - m_new)
        lsum = jnp.sum(ps[0], axis=1, keepdims=True)
        for p in ps[1:]:
            lsum = lsum + jnp.sum(p, axis=1, keepdims=True)
        l_ref[...] = alpha * l_ref[...] + lsum
        m_ref[...] = m_new
        pv = _dot_nt(ps[0].astype(BF16), vrefs[0]().astype(BF16))
        for u in range(1, n):
            pv = pv + _dot_nt(ps[u].astype(BF16), vrefs[u]().astype(BF16))
        acc_ref[...] = alpha[:, 0:1] * acc_ref[...] + pv

    pages([lambda u=u: kp[u][...] for u in range(pp)], [lambda u=u: vp[u][...] for u in range(pp)],
          [lambda u=u: lp[u][...] for u in range(pp)], False)

    @pl.when(step == pl.num_programs(1) - 1)
    def _():
        pages([lambda: kn_ref[0]], [lambda: vn_ref[0]], [lambda: lfn_ref[0]], True)
        res = jnp.where(bd, acc_ref[...] / l_ref[:, 0:1], 0.0)
        for t in range(dec):
            o_ref[0, t:t + 1, :] = jnp.sum(res[t * B_HEADS:(t + 1) * B_HEADS], axis=0, keepdims=True)


def _foxdec(layer, page_table, q, knew, vnew, lfn_t, cache_k, cache_v, cache_lft, pp):
    ns, dec, _ = q.shape
    n_pages = page_table.shape[1]
    page = cache_k.shape[3]
    nsteps = n_pages // pp
    nrow = dec * B_HEADS
    cum = np.concatenate([np.triu(np.ones((page, page), np.float32)), np.ones((page, page), np.float32)], axis=1)
    seq3 = lambda s, p, pt: (s, 0, 0)
    fix2 = lambda s, p, pt: (0, 0)

    def cache_map(u):
        return lambda s, p, pt: (layer, pt[s, p * pp + u], 0, 0)

    in_specs = [pl.BlockSpec((1, dec, 512), seq3), pl.BlockSpec((1, 512, page), seq3),
                pl.BlockSpec((1, 512, page), seq3), pl.BlockSpec((1, B_HEADS, page), seq3),
                pl.BlockSpec((page, 2 * page), fix2)]
    in_specs += [pl.BlockSpec((None, None, 512, page), cache_map(u)) for u in range(pp)]
    in_specs += [pl.BlockSpec((None, None, 512, page), cache_map(u)) for u in range(pp)]
    in_specs += [pl.BlockSpec((None, None, B_HEADS, page), cache_map(u)) for u in range(pp)]
    grid_spec = pltpu.PrefetchScalarGridSpec(
        num_scalar_prefetch=1,
        grid=(ns, nsteps),
        in_specs=in_specs,
        out_specs=pl.BlockSpec((1, dec, 512), seq3),
        scratch_shapes=[pltpu.VMEM((nrow, 512), BF16), pltpu.VMEM((nrow, LANES), F32),
                        pltpu.VMEM((nrow, LANES), F32), pltpu.VMEM((nrow, LANES), F32),
                        pltpu.VMEM((nrow, 512), F32)],
    )
    return pl.pallas_call(
        functools.partial(_foxdec_kernel, pp=pp, dec=dec),
        grid_spec=grid_spec,
        out_shape=jax.ShapeDtypeStruct((ns, dec, 512), F32),
        compiler_params=_cp("arbitrary", "arbitrary"),
        name="foxdec",
    )(page_table, q, knew, vnew, lfn_t, _const(cum),
      *([cache_k] * pp), *([cache_v] * pp), *([cache_lft] * pp))


def _hgrn_kernel(*refs, layer, ns, cps, valid, has_init):
    (q_ref, f_ref, v_ref, gate_ref, lbraw_ref, ng_ref, tril_ref, rs_ref, tile_ref, hs_ref) = refs[:10]
    pos = 10
    s0_ref = None
    if has_init:
        s0_ref = refs[pos]
        pos += 1
    o_ref, sout_ref, st_ref, oin_ref, oint_ref, tmp_ref = refs[pos:pos + 6]
    i = pl.program_id(1)
    tb = q_ref.shape[0]
    nchunk = tb // SUB
    width = A_HEADS * HD

    lbraw = lbraw_ref[...]
    e = jnp.exp(lbraw - jnp.max(lbraw, axis=0, keepdims=True))
    prob = e / jnp.sum(e, axis=0, keepdims=True)
    lb = jnp.zeros((1, width), F32)
    for d in range(1, layer + 1):
        lb = lb + prob[d:d + 1, :]

    fr = f_ref[...]
    qr = q_ref[...]
    vr = v_ref[...]
    logf = jnp.log(lb + (1.0 - lb) * _sigmoid(fr))
    key = (1.0 - lb) * _sigmoid(-fr)
    if valid < SUB:
        rowv = _imod(lax.broadcasted_iota(jnp.int32, (tb, width), 0), SUB) < valid
        logf = jnp.where(rowv, logf, 0.0)
        key = jnp.where(rowv, key, 0.0)
    q = (qr * _sigmoid(qr)) * (HD ** -0.5)
    g = _exact_left(tril_ref[...], logf)
    g3 = g.reshape(nchunk, SUB, width)
    q3 = q.reshape(nchunk, SUB, width)
    k3 = key.reshape(nchunk, SUB, width)

    att = jnp.zeros((tb, A_HEADS * LANES), F32)
    for s in range(SUB):
        d = jnp.minimum(g3 - g3[:, s:s + 1, :], 0.0)
        p = (q3 * jnp.exp(d)) * k3[:, s:s + 1, :]
        att = att + _dot(p.reshape(tb, width).astype(BF16), rs_ref[s])

    rows = lax.broadcasted_iota(jnp.int32, (tb, tb), 0)
    cols = lax.broadcasted_iota(jnp.int32, (tb, tb), 1)
    causal = (_idiv(rows, SUB) == _idiv(cols, SUB)) & (cols <= rows)
    lane = lax.broadcasted_iota(jnp.int32, (tb, LANES), 1)
    for pr in range(2):
        vpair = vr[:, pr * LANES:(pr + 1) * LANES]
        acc = None
        for hh in range(2):
            h = 2 * pr + hh
            full = _dot(att[:, h * LANES:(h + 1) * LANES].astype(BF16), tile_ref[...])
            full = jnp.where(causal, full, 0.0).astype(BF16)
            vm = jnp.where((lane < HD) == (hh == 0), vpair, 0.0).astype(BF16)
            term = _dot(full, vm)
            acc = term if acc is None else acc + term
        oin_ref[:, pr * LANES:(pr + 1) * LANES] = acc

    gend3 = g3[:, SUB - 1:SUB, :]
    kd = (k3 * jnp.exp(gend3 - g3)).reshape(tb, width).astype(BF16)
    eg = jnp.exp(g)
    qg = (q * eg).astype(BF16)
    r128 = lax.broadcasted_iota(jnp.int32, (LANES, LANES), 0)
    c128 = lax.broadcasted_iota(jnp.int32, (LANES, LANES), 1)
    bdm = _idiv(r128, HD) == _idiv(c128, HD)
    vb = vr.astype(BF16)
    last = i == pl.num_programs(1) - 1
    units = [(sq, pr) for sq in range(ns) for pr in range(2)]
    rowsl = lambda sq, c: slice((sq * cps + c) * SUB, (sq * cps + c + 1) * SUB)
    lanesl = lambda pr: slice(pr * LANES, (pr + 1) * LANES)
    incs = {}
    for sq, pr in units:
        for c in range(cps):
            u = _dot_tn(vb[rowsl(sq, c), lanesl(pr)], kd[rowsl(sq, c), lanesl(pr)])
            incs[sq, pr, c] = jnp.where(bdm, u, 0.0)
    if has_init:
        tmp_ref[...] = jnp.zeros_like(tmp_ref)
        for sq, pr in units:
            tmp_ref[2 * sq + pr, 0:HD, 0:HD] = s0_ref[sq, 2 * pr]
            tmp_ref[2 * sq + pr, HD:LANES, HD:LANES] = s0_ref[sq, 2 * pr + 1]
    seen = {}
    final = {}
    for sq, pr in units:
        if has_init:
            st = tmp_ref[2 * sq + pr].T
        else:
            st = jnp.where(i == 0, 0.0, st_ref[pr])
        for c in range(cps):
            seen[sq, pr, c] = st.astype(BF16)
            r_end = (sq * cps + c + 1) * SUB
            st = st * eg[r_end - 1:r_end, lanesl(pr)] + incs[sq, pr, c]
        if not has_init:
            st_ref[pr] = st
        final[sq, pr] = st
    for sq, pr in units:
        for c in range(cps):
            oint_ref[rowsl(sq, c), lanesl(pr)] = _dot_nt(qg[rowsl(sq, c), lanesl(pr)], seen[sq, pr, c])

    @pl.when(last)
    def _():
        for sq, pr in units:
            tmp_ref[2 * sq + pr] = final[sq, pr].T
        for sq, pr in units:
            sout_ref[sq, 2 * pr] = tmp_ref[2 * sq + pr, 0:HD, 0:HD]
            sout_ref[sq, 2 * pr + 1] = tmp_ref[2 * sq + pr, HD:LANES, HD:LANES]

    o = oin_ref[...] + oint_ref[...]
    gr = gate_ref[...]
    o = (o * lax.rsqrt(_head_ssq(o, hs_ref[...]) * (1.0 / HD) + EPS)) * ng_ref[...]
    o_ref[...] = o * (gr * _sigmoid(gr))


def _hgrn(a, lbraw, ng, s0, layer, n_outer, n_inner, tb, ns, cps, valid):
    rows = a.shape[0]
    width = A_HEADS * HD
    has_init = s0 is not None
    nseq = n_outer * ns
    rs = np.zeros((SUB, width, A_HEADS * LANES), np.float32)
    for s in range(SUB):
        for h in range(A_HEADS):
            rs[s, h * HD + np.arange(HD), h * LANES + s] = 1.0
    tile = np.zeros((LANES, tb), np.float32)
    for s in range(SUB):
        tile[s, np.arange(tb // SUB) * SUB + s] = 1.0

    def col(j):
        return lambda o, i: (o * n_inner + i, j)

    fix2 = lambda o, i: (0, 0)
    fix3 = lambda o, i: (0, 0, 0)
    in_specs = [pl.BlockSpec((tb, width), col(j)) for j in range(4)]
    in_specs += [pl.BlockSpec(lbraw.shape, fix2), pl.BlockSpec((1, width), fix2),
                 pl.BlockSpec((tb, tb), fix2), pl.BlockSpec((SUB, width, A_HEADS * LANES), fix3),
                 pl.BlockSpec((LANES, tb), fix2), pl.BlockSpec((width, width), fix2)]
    args = [a, a, a, a, lbraw, ng, _const(_chunk_tril(tb, SUB)), _const(rs), _const(tile),
            _const(_head_sum_matrix(width))]
    if has_init:
        in_specs.append(pl.BlockSpec((ns, A_HEADS, HD, HD), lambda o, i: (o, 0, 0, 0)))
        args.append(s0)
    return pl.pallas_call(
        functools.partial(_hgrn_kernel, layer=layer, ns=ns, cps=cps, valid=valid, has_init=has_init),
        grid=(n_outer, n_inner),
        in_specs=in_specs,
        out_specs=[pl.BlockSpec((tb, width), lambda o, i: (o * n_inner + i, 0)),
                   pl.BlockSpec((ns, A_HEADS, HD, HD), lambda o, i: (o, 0, 0, 0))],
        out_shape=[jax.ShapeDtypeStruct((rows, width), F32),
                   jax.ShapeDtypeStruct((nseq, A_HEADS, HD, HD), F32)],
        scratch_shapes=[pltpu.VMEM((2, LANES, LANES), F32), pltpu.VMEM((tb, width), F32),
                        pltpu.VMEM((tb, width), F32), pltpu.VMEM((2 * ns, LANES, LANES), F32)],
        compiler_params=_cp("arbitrary", "arbitrary"),
        name="hgrn",
    )(*args)


def _gdn_kernel(*refs, ns, cps, chunk, has_init):
    (x_ref, z_ref, g_ref, cw_ref, ng_ref, tril_ref, hs_ref, eg_ref, eb_ref, eg2_ref, eb2_ref) = refs[:11]
    pos = 11
    cinit_ref = s0_ref = None
    if has_init:
        cinit_ref, s0_ref = refs[pos:pos + 2]
        pos += 2
    o_ref, sout_ref, st_ref, cv_ref, oacc_ref, vn_ref, tmp_ref = refs[pos:pos + 7]
    i = pl.program_id(1)
    tb = x_ref.shape[0]
    width = C_HEADS * HD
    rows_seq = cps * chunk
    last = i == pl.num_programs(1) - 1

    x = x_ref[...]
    cw = cw_ref[...]
    acts = []
    for sq in range(ns):
        xs = x[sq * rows_seq:(sq + 1) * rows_seq]
        if has_init:
            prev = cinit_ref[sq]
        else:
            prev = jnp.where(i == 0, 0.0, cv_ref[...])
        xc = jnp.concatenate([prev, xs], axis=0)
        conv = xc * cw[CONV_W - 1:CONV_W, :]
        for j in range(1, CONV_W):
            conv = conv + pltpu.roll(xc, j, 0) * cw[CONV_W - 1 - j:CONV_W - j, :]
        acts.append(conv[8:])
        if not has_init:
            cv_ref[...] = xs[rows_seq - 8:]
    conv = acts[0] if ns == 1 else jnp.concatenate(acts, axis=0)
    act = conv * _sigmoid(conv)
    hs = hs_ref[...]
    q = act[:, 0:width]
    k = act[:, width:2 * width]
    v = act[:, 2 * width:3 * width]
    q = (q * lax.rsqrt(_head_ssq(q, hs) + EPS)) * (HD ** -0.5)
    k = k * lax.rsqrt(_head_ssq(k, hs) + EPS)

    gt = g_ref[...]
    gc = _exact_left(tril_ref[...], gt)
    gexp = _exact_right(gc, eg_ref[...])
    bexp = _exact_right(gt, eb_ref[...])
    gcol = _exact_right(gc, eg2_ref[...])
    bcol = _exact_right(gt, eb2_ref[...])
    nchunk = tb // chunk
    gexp3 = gexp.reshape(nchunk, chunk, width)
    gend3 = gexp3[:, chunk - 1:chunk, :]
    eg = jnp.exp(gexp)
    rhs = jnp.concatenate([v * bexp, (k * bexp) * eg], axis=1).astype(BF16)
    qg = (q * eg).astype(BF16)
    kd = (k.reshape(nchunk, chunk, width) * jnp.exp(gend3 - gexp3)).reshape(tb, width).astype(BF16)
    kb = k.astype(BF16)
    lane = lax.broadcasted_iota(jnp.int32, (tb, width), 1)
    kmask = [jnp.where(_idiv(lane, HD) == h, k, 0.0).astype(BF16) for h in range(C_HEADS)]
    qmask = [jnp.where(_idiv(lane, HD) == h, q, 0.0).astype(BF16) for h in range(C_HEADS)]

    rc = lax.broadcasted_iota(jnp.int32, (tb, tb), 0)
    cc = lax.broadcasted_iota(jnp.int32, (tb, tb), 1)
    same = _idiv(rc, chunk) == _idiv(cc, chunk)
    eye = rc == cc
    incl = same & (cc <= rc)
    strict = same & (cc < rc)
    hl = _idiv(lane, HD)
    r256 = lax.broadcasted_iota(jnp.int32, (width, width), 0)
    c256 = lax.broadcasted_iota(jnp.int32, (width, width), 1)
    bdm = _idiv(r256, HD) == _idiv(c256, HD)
    nstage = int(math.log2(chunk))
    heads = range(C_HEADS)
    reps = tb // LANES
    gct = gc.T
    decay, xm, tm, qkd = [], [], [], []
    for h in heads:
        g_t = jnp.concatenate([gcol[:, h * LANES:(h + 1) * LANES]] * reps, axis=1)
        decay.append(jnp.exp(jnp.minimum(g_t - gct[8 + h:9 + h, :], 0.0)))
    kk = [_dot_nt(kmask[h], kb) for h in heads]
    qk = [_dot_nt(qmask[h], kb) for h in heads]
    for h in heads:
        b_t = jnp.concatenate([bcol[:, h * LANES:(h + 1) * LANES]] * reps, axis=1)
        x0 = -jnp.where(strict, (b_t * kk[h]) * decay[h], 0.0)
        xm.append(x0)
        tm.append(jnp.where(eye, 1.0, 0.0) + x0)
        qkd.append(jnp.where(incl, qk[h] * decay[h], 0.0).astype(BF16))
    for _ in range(nstage - 1):
        xb = [x.astype(BF16) for x in xm]
        xm = [_dot(xb[h], xb[h]) for h in heads]
        tm = [tm[h] + _dot(tm[h].astype(BF16), xm[h].astype(BF16)) for h in heads]
    sol = [_dot(tm[h].astype(BF16), rhs) for h in heads]
    u = jnp.zeros((tb, width), F32)
    w = jnp.zeros((tb, width), F32)
    for h in heads:
        u = u + jnp.where(hl == h, sol[h][:, 0:width], 0.0)
        w = w + jnp.where(hl == h, sol[h][:, width:2 * width], 0.0)
    wb = w.astype(BF16)

    if has_init:
        tmp_ref[...] = jnp.zeros_like(tmp_ref)
        for sq in range(ns):
            for h in heads:
                tmp_ref[sq, h * HD:(h + 1) * HD, h * HD:(h + 1) * HD] = s0_ref[sq, h]
        sts = [tmp_ref[sq] for sq in range(ns)]
    else:
        sts = [jnp.where(i == 0, 0.0, st_ref[...])]
    rowsl = lambda sq, c: slice((sq * cps + c) * chunk, (sq * cps + c + 1) * chunk)
    for c in range(cps):
        stb = [st.astype(BF16) for st in sts]
        ws = [_dot(wb[rowsl(sq, c)], stb[sq]) for sq in range(ns)]
        for sq in range(ns):
            oacc_ref[rowsl(sq, c), :] = _dot(qg[rowsl(sq, c)], stb[sq])
        vns = [(u[rowsl(sq, c)] - ws[sq]).astype(BF16) for sq in range(ns)]
        for sq in range(ns):
            vn_ref[rowsl(sq, c), :] = vns[sq]
        incs = [jnp.where(bdm, _dot_tn(kd[rowsl(sq, c)], vns[sq]), 0.0) for sq in range(ns)]
        sts = [sts[sq] * eg[(sq * cps + c + 1) * chunk - 1:(sq * cps + c + 1) * chunk, :] + incs[sq]
               for sq in range(ns)]
    if not has_init:
        st_ref[...] = sts[0]

    @pl.when(last)
    def _():
        for sq in range(ns):
            tmp_ref[sq] = sts[sq]
        for sq in range(ns):
            for h in heads:
                sout_ref[sq, h] = tmp_ref[sq, h * HD:(h + 1) * HD, h * HD:(h + 1) * HD]

    vnb = vn_ref[...]
    o = oacc_ref[...]
    for h in heads:
        o = o + jnp.where(hl == h, _dot(qkd[h], vnb), 0.0)
    z = z_ref[...]
    o = (o * lax.rsqrt(_head_ssq(o, hs) * (1.0 / HD) + EPS)) * ng_ref[...]
    o_ref[...] = o * (z * _sigmoid(z))


def _gdn(cfull, g, cw, ng, cinit, s0, n_outer, n_inner, tb, ns, cps, chunk):
    rows = cfull.shape[0]
    width = C_HEADS * HD
    has_init = s0 is not None
    nseq = n_outer * ns
    eg = np.zeros((LANES, width), np.float32)
    eb = np.zeros((LANES, width), np.float32)
    eg2 = np.zeros((LANES, C_HEADS * LANES), np.float32)
    eb2 = np.zeros((LANES, C_HEADS * LANES), np.float32)
    for h in range(C_HEADS):
        eg[8 + h, h * HD:(h + 1) * HD] = 1.0
        eb[12 + h, h * HD:(h + 1) * HD] = 1.0
        eg2[8 + h, h * LANES:(h + 1) * LANES] = 1.0
        eb2[12 + h, h * LANES:(h + 1) * LANES] = 1.0
    rowm = lambda o, i: (o * n_inner + i, 0)
    fix2 = lambda o, i: (0, 0)
    in_specs = [pl.BlockSpec((tb, 3 * width), rowm),
                pl.BlockSpec((tb, width), lambda o, i: (o * n_inner + i, 3)),
                pl.BlockSpec((tb, LANES), rowm), pl.BlockSpec((CONV_W, 3 * width), fix2),
                pl.BlockSpec((1, width), fix2), pl.BlockSpec((tb, tb), fix2),
                pl.BlockSpec((width, width), fix2), pl.BlockSpec((LANES, width), fix2),
                pl.BlockSpec((LANES, width), fix2), pl.BlockSpec((LANES, C_HEADS * LANES), fix2),
                pl.BlockSpec((LANES, C_HEADS * LANES), fix2)]
    args = [cfull, cfull, g, cw, ng, _const(_chunk_tril(tb, chunk)), _const(_head_sum_matrix(width)),
            _const(eg), _const(eb), _const(eg2), _const(eb2)]
    if has_init:
        in_specs += [pl.BlockSpec((ns, 8, 3 * width), lambda o, i: (o, 0, 0)),
                     pl.BlockSpec((ns, C_HEADS, HD, HD), lambda o, i: (o, 0, 0, 0))]
        args += [cinit, s0]
    return pl.pallas_call(
        functools.partial(_gdn_kernel, ns=ns, cps=cps, chunk=chunk, has_init=has_init),
        grid=(n_outer, n_inner),
        in_specs=in_specs,
        out_specs=[pl.BlockSpec((tb, width), rowm),
                   pl.BlockSpec((ns, C_HEADS, HD, HD), lambda o, i: (o, 0, 0, 0))],
        out_shape=[jax.ShapeDtypeStruct((rows, width), F32),
                   jax.ShapeDtypeStruct((nseq, C_HEADS, HD, HD), F32)],
        scratch_shapes=[pltpu.VMEM((width, width), F32), pltpu.VMEM((8, 3 * width), F32),
                        pltpu.VMEM((tb, width), F32), pltpu.VMEM((tb, width), BF16),
                        pltpu.VMEM((ns, width, width), F32)],
        compiler_params=_cp("arbitrary", "arbitrary"),
        name="gdn",
    )(*args)


def _post_kernel(x_ref, oa_ref, ob_ref, oc_ref, wo_ref, ln_ref, wr_ref, br_ref,
                 xn_ref, hf_ref, gate_ref, sel_ref):
    x = x_ref[...]
    mix = _dot(oa_ref[...].astype(BF16), wo_ref[0:256, :])
    mix = mix + _dot(ob_ref[...].astype(BF16), wo_ref[256:768, :])
    mix = mix + _dot(oc_ref[...].astype(BF16), wo_ref[768:1024, :])
    xn = x + mix
    xn_ref[...] = xn
    ms = jnp.mean(xn * xn, axis=-1, keepdims=True)
    hf = (xn * lax.rsqrt(ms + EPS)) * ln_ref[...]
    hf_ref[...] = hf
    logits = _dot(hf.astype(BF16), wr_ref[...]) + br_ref[...]
    lane = lax.broadcasted_iota(jnp.int32, logits.shape, 1)
    big = jnp.int32(1 << 20)
    isg = (lane >= N_EXPERTS) & (lane < N_EXPERTS + N_GROUPS)
    gl = jnp.where(isg, logits, NEG)
    gm = jnp.max(gl, axis=1, keepdims=True)
    gidx = jnp.min(jnp.where(isg & (gl == gm), lane, big), axis=1, keepdims=True) - N_EXPERTS
    top_gp = 1.0 / jnp.sum(jnp.where(isg, jnp.exp(gl - gm), 0.0), axis=1, keepdims=True)
    ing = (lane < N_EXPERTS) & (_idiv(lane, EXPERTS_PER_GROUP) == gidx)
    el = jnp.where(ing, logits, NEG)
    em = jnp.max(el, axis=1, keepdims=True)
    ee = jnp.where(ing, jnp.exp(el - em), 0.0)
    prob = ee / jnp.sum(ee, axis=1, keepdims=True)
    p1 = jnp.max(prob, axis=1, keepdims=True)
    i1 = jnp.min(jnp.where(ing & (prob == p1), lane, big), axis=1, keepdims=True)
    rest = jnp.where(ing & (lane != i1), prob, -1.0)
    p2 = jnp.max(rest, axis=1, keepdims=True)
    i2 = jnp.min(jnp.where(ing & (lane != i1) & (rest == p2), lane, big), axis=1, keepdims=True)
    den = p1 + p2
    gate_ref[...] = jnp.where(lane == i1, (top_gp * p1) / den,
                              jnp.where(lane == i2, (top_gp * p2) / den, 0.0))
    sel_ref[...] = jnp.where((lane == i1) | (lane == i2), 1.0, 0.0)


def _post(x, oa, ob, oc, wo, ln, wr, br, tm):
    t = x.shape[0]
    row = lambda i: (i, 0)
    fix = lambda i: (0, 0)
    return pl.pallas_call(
        _post_kernel,
        grid=(t // tm,),
        in_specs=[pl.BlockSpec((tm, D_MODEL), row), pl.BlockSpec((tm, 256), row),
                  pl.BlockSpec((tm, 512), row), pl.BlockSpec((tm, 256), row),
                  pl.BlockSpec((D_MODEL, D_MODEL), fix), pl.BlockSpec((1, D_MODEL), fix),
                  pl.BlockSpec((D_MODEL, LANES), fix), pl.BlockSpec((1, LANES), fix)],
        out_specs=[pl.BlockSpec((tm, D_MODEL), row), pl.BlockSpec((tm, D_MODEL), row),
                   pl.BlockSpec((tm, LANES), row), pl.BlockSpec((tm, LANES), row)],
        out_shape=[jax.ShapeDtypeStruct((t, D_MODEL), F32), jax.ShapeDtypeStruct((t, D_MODEL), F32),
                   jax.ShapeDtypeStruct((t, LANES), F32), jax.ShapeDtypeStruct((t, LANES), F32)],
        compiler_params=_cp("arbitrary"),
        name="post",
    )(x, oa, ob, oc, wo, ln, wr, br)


MOE_TM = 256


def _moe_rank_kernel(sel_ref, tril_ref, rank_ref, cnt_ref, carry_ref):
    @pl.when(pl.program_id(0) == 0)
    def _():
        carry_ref[...] = jnp.zeros_like(carry_ref)

    sel = sel_ref[...]
    rank_ref[...] = _dot(tril_ref[...], sel.astype(BF16)) + carry_ref[...]
    carry_ref[...] = carry_ref[...] + jnp.sum(sel, axis=0, keepdims=True)
    cnt_ref[...] = carry_ref[...]


def _moe_rank(sel, tm):
    t = sel.shape[0]
    strict = np.tril(np.ones((tm, tm), np.float32), -1)
    return pl.pallas_call(
        _moe_rank_kernel,
        grid=(t // tm,),
        in_specs=[pl.BlockSpec((tm, LANES), lambda i: (i, 0)), pl.BlockSpec((tm, tm), lambda i: (0, 0))],
        out_specs=[pl.BlockSpec((tm, LANES), lambda i: (i, 0)), pl.BlockSpec((1, LANES), lambda i: (0, 0))],
        out_shape=[jax.ShapeDtypeStruct((t, LANES), F32), jax.ShapeDtypeStruct((1, LANES), F32)],
        scratch_shapes=[pltpu.VMEM((1, LANES), F32)],
        compiler_params=_cp("arbitrary"),
        name="moe_rank",
    )(sel, _const(strict))


def _moe_dest_kernel(sel_ref, gate_ref, rank_ref, off_ref, info_ref):
    on = sel_ref[...] > 0.0
    gates = gate_ref[...]
    dest = off_ref[...] + rank_ref[...]
    lane = lax.broadcasted_iota(jnp.int32, gates.shape, 1)
    la = jnp.min(jnp.where(on, lane, LANES), axis=1, keepdims=True)
    lb = jnp.max(jnp.where(on, lane, -1), axis=1, keepdims=True)
    pick = lambda v, l: jnp.sum(jnp.where(lane == l, v, 0.0), axis=1, keepdims=True)
    info_ref[...] = jnp.where(lane == 0, pick(dest, la),
                              jnp.where(lane == 1, pick(dest, lb),
                                        jnp.where(lane == 2, pick(gates, la),
                                                  jnp.where(lane == 3, pick(gates, lb), 0.0))))


def _moe_dest(sel, gates, rank, off, tm):
    t = sel.shape[0]
    row = lambda i: (i, 0)
    return pl.pallas_call(
        _moe_dest_kernel,
        grid=(t // tm,),
        in_specs=[pl.BlockSpec((tm, LANES), row)] * 3 + [pl.BlockSpec((1, LANES), lambda i: (0, 0))],
        out_specs=pl.BlockSpec((tm, LANES), row),
        out_shape=jax.ShapeDtypeStruct((t, LANES), F32),
        compiler_params=_cp("arbitrary"),
        name="moe_dest",
    )(sel, gates, rank, off)


def _row_copy(src_ref, src_row, dst_ref, dst_row, sem):
    return pltpu.make_async_copy(src_ref.at[pl.ds(src_row, 1), :], dst_ref.at[pl.ds(dst_row, 1), :], sem)


def _moe_dispatch_kernel(da_ref, db_ref, hf_ref, xs_in_ref, xs_ref, sem):
    del xs_in_ref
    tm = hf_ref.shape[0]

    def issue(r, carry):
        _row_copy(hf_ref, r, xs_ref, da_ref[0, 0, r], sem.at[0]).start()
        _row_copy(hf_ref, r, xs_ref, db_ref[0, 0, r], sem.at[0]).start()
        return carry

    lax.fori_loop(0, tm, issue, 0)

    def drain(r, carry):
        _row_copy(hf_ref, 0, xs_ref, 0, sem.at[0]).wait()
        _row_copy(hf_ref, 0, xs_ref, 0, sem.at[0]).wait()
        return carry

    lax.fori_loop(0, tm, drain, 0)


def _moe_dispatch(hf, da, db, nrows, tm):
    t = hf.shape[0]
    idx = pl.BlockSpec((1, 1, tm), lambda i: (i, 0, 0), memory_space=pltpu.SMEM)
    return pl.pallas_call(
        _moe_dispatch_kernel,
        grid=(t // tm,),
        in_specs=[idx, idx, pl.BlockSpec((tm, D_MODEL), lambda i: (i, 0)),
                  pl.BlockSpec(memory_space=pl.ANY)],
        out_specs=pl.BlockSpec(memory_space=pl.ANY),
        out_shape=jax.ShapeDtypeStruct((nrows, D_MODEL), F32),
        scratch_shapes=[pltpu.SemaphoreType.DMA((1,))],
        input_output_aliases={3: 0},
        compiler_params=_cp("arbitrary"),
        name="moe_dispatch",
    )(da, db, hf, jnp.zeros((nrows, D_MODEL), F32))


def _moe_ffn_kernel(texp_ref, nval_ref, x_ref, wg_ref, wu_ref, wd_ref, y_ref):
    del texp_ref
    i = pl.program_id(0)

    @pl.when(i < nval_ref[0])
    def _():
        x = x_ref[...].astype(BF16)
        gp = _dot(x, wg_ref[0])
        up = _dot(x, wu_ref[0])
        y_ref[...] = _dot(((gp * _sigmoid(gp)) * up).astype(BF16), wd_ref[0])

    @pl.when(i >= nval_ref[0])
    def _():
        y_ref[...] = jnp.zeros_like(y_ref)


def _moe_ffn(xs, texp, nval, wg, wu, wd):
    nrows = xs.shape[0]
    tile = lambda i, te, nv: (jnp.minimum(i, nv[0] - 1), 0)
    wmap = lambda i, te, nv: (te[i], 0, 0)
    grid_spec = pltpu.PrefetchScalarGridSpec(
        num_scalar_prefetch=2,
        grid=(nrows // MOE_TM,),
        in_specs=[pl.BlockSpec((MOE_TM, D_MODEL), tile),
                  pl.BlockSpec((1, D_MODEL, D_EXPERT), wmap), pl.BlockSpec((1, D_MODEL, D_EXPERT), wmap),
                  pl.BlockSpec((1, D_EXPERT, D_MODEL), wmap)],
        out_specs=pl.BlockSpec((MOE_TM, D_MODEL), lambda i, te, nv: (i, 0)),
    )
    return pl.pallas_call(
        _moe_ffn_kernel,
        grid_spec=grid_spec,
        out_shape=jax.ShapeDtypeStruct((nrows, D_MODEL), F32),
        compiler_params=_cp("arbitrary"),
        name="moe_ffn",
    )(texp, nval, xs, wg, wu, wd)


def _moe_combine_kernel(da_ref, db_ref, info_ref, x_ref, ys_ref, o_ref, buf_ref, sem):
    tm = x_ref.shape[0]

    def issue(r, carry):
        _row_copy(ys_ref, da_ref[0, 0, r], buf_ref.at[0], r, sem.at[0]).start()
        _row_copy(ys_ref, db_ref[0, 0, r], buf_ref.at[1], r, sem.at[0]).start()
        return carry

    lax.fori_loop(0, tm, issue, 0)

    def drain(r, carry):
        _row_copy(ys_ref, 0, buf_ref.at[0], 0, sem.at[0]).wait()
        _row_copy(ys_ref, 0, buf_ref.at[1], 0, sem.at[0]).wait()
        return carry

    lax.fori_loop(0, tm, drain, 0)
    info = info_ref[...]
    o_ref[...] = (x_ref[...] + info[:, 2:3] * buf_ref[0]) + info[:, 3:4] * buf_ref[1]


def _moe_combine(xn, info, da, db, ys, tm):
    t = xn.shape[0]
    idx = pl.BlockSpec((1, 1, tm), lambda i: (i, 0, 0), memory_space=pltpu.SMEM)
    return pl.pallas_call(
        _moe_combine_kernel,
        grid=(t // tm,),
        in_specs=[idx, idx, pl.BlockSpec((tm, LANES), lambda i: (i, 0)),
                  pl.BlockSpec((tm, D_MODEL), lambda i: (i, 0)), pl.BlockSpec(memory_space=pl.ANY)],
        out_specs=pl.BlockSpec((tm, D_MODEL), lambda i: (i, 0)),
        out_shape=jax.ShapeDtypeStruct((t, D_MODEL), F32),
        scratch_shapes=[pltpu.VMEM((2, tm, D_MODEL), F32), pltpu.SemaphoreType.DMA((1,))],
        compiler_params=_cp("arbitrary"),
        name="moe_combine",
    )(da, db, info, xn, ys)


def _moe(hf, gates, sel, xn, wg, wu, wd, tm):
    t = hf.shape[0]
    nrows = -(-(2 * t + N_EXPERTS * (MOE_TM - 1)) // MOE_TM) * MOE_TM
    ntile = nrows // MOE_TM
    rank, cnt = _moe_rank(sel, tm)
    cnt = cnt[0, :N_EXPERTS].astype(jnp.int32)
    padded = (cnt + (MOE_TM - 1)) // MOE_TM * MOE_TM
    end = jnp.cumsum(padded)
    off = jnp.zeros((1, LANES), F32).at[0, :N_EXPERTS].set((end - padded).astype(F32))
    nval = (end[-1] // MOE_TM).astype(jnp.int32)
    tile_start = jnp.arange(ntile, dtype=jnp.int32) * MOE_TM
    texp = jnp.sum((end[None, :] <= tile_start[:, None]).astype(jnp.int32), axis=1)
    texp = jnp.minimum(texp, N_EXPERTS - 1)
    texp = jnp.where(jnp.arange(ntile) < nval, texp, texp[jnp.maximum(nval - 1, 0)])
    info = _moe_dest(sel, gates, rank, off, tm)
    da = info[:, 0].astype(jnp.int32).reshape(t // tm, 1, tm)
    db = info[:, 1].astype(jnp.int32).reshape(t // tm, 1, tm)
    xs = _moe_dispatch(hf, da, db, nrows, tm)
    ys = _moe_ffn(xs, texp, nval.reshape(1), wg, wu, wd)
    return _moe_combine(xn, info, da, db, ys, tm)


def _pad_rows(a, nseq, dec):
    c = a.shape[-1]
    return jnp.pad(a.reshape(nseq, dec, c), ((0, 0), (0, SUB - dec), (0, 0))).reshape(nseq * SUB, c)


def _layer_params(l, ln_mix, w_in, hgrn_norm, fox_bf, fox_qnorm, fox_knorm, gdn_conv, gdn_a_log,
                  gdn_dt_bias, gdn_norm, w_out, ln_ffn, w_group, b_group, w_router, b_router,
                  w_gate, w_up, w_down):
    w = w_in[l]
    gates = jnp.concatenate([w[:, 2560:2568], w[:, 3592:3600]], axis=1)
    wp = jnp.concatenate([w[:, 0:2560], w[:, 2568:3592], gates,
                          jnp.zeros((D_MODEL, LANES - 16), F32)], axis=1).astype(BF16)
    p1 = jnp.zeros((1, LANES), F32).at[0, 0:8].set(fox_bf[l]).at[0, 8:12].set(gdn_dt_bias[l])
    p2 = jnp.zeros((1, LANES), F32).at[0, 8:12].set(gdn_a_log[l])
    wr = jnp.concatenate([w_router[l], w_group[l],
                          jnp.zeros((D_MODEL, LANES - N_GROUPS - N_EXPERTS), F32)], axis=1).astype(BF16)
    br = jnp.zeros((1, LANES), F32).at[0, 0:N_EXPERTS].set(b_router[l])
    br = br.at[0, N_EXPERTS:N_EXPERTS + N_GROUPS].set(b_group[l])
    return dict(
        ln_mix=ln_mix[l][None, :], w=wp, p1=p1, p2=p2,
        qg=jnp.tile(fox_qnorm[l], B_HEADS)[None, :], kg=jnp.tile(fox_knorm[l], B_HEADS)[None, :],
        hgrn_ng=jnp.tile(hgrn_norm[l], A_HEADS)[None, :], gdn_ng=jnp.tile(gdn_norm[l], C_HEADS)[None, :],
        conv=gdn_conv[l], w_out=w_out[l].astype(BF16), ln_ffn=ln_ffn[l][None, :], wr=wr, br=br,
        wg=w_gate[l].astype(BF16), wu=w_up[l].astype(BF16), wd=w_down[l].astype(BF16))


def kernel(x_prompt, x_sample, cache_k, cache_v, cache_logf, page_table, state_hgrn, state_gdn, state_conv,
           ln_mix, w_in, hgrn_lb, hgrn_norm, fox_bf, fox_qnorm, fox_knorm, gdn_conv, gdn_a_log, gdn_dt_bias,
           gdn_norm, w_out, ln_ffn, w_group, b_group, w_router, b_router, w_gate, w_up, w_down):
    nb, seq, _ = x_prompt.shape
    nsq, dec, _ = x_sample.shape
    depth = ln_mix.shape[0]
    n_phys, page = cache_k.shape[1], cache_k.shape[2]
    tp = nb * seq
    ts = nsq * dec
    hs512 = _const(_head_sum_matrix(512))
    ck = jnp.transpose(cache_k, (0, 1, 3, 4, 2)).reshape(depth, n_phys, 512, page)
    cv = jnp.transpose(cache_v, (0, 1, 3, 4, 2)).reshape(depth, n_phys, 512, page)
    clt = jnp.swapaxes(cache_logf, 2, 3)
    tbp = min(256, seq)
    gchunk = min(64, seq)
    seq_blk = 16

    yp = x_prompt.reshape(tp, D_MODEL)
    ys = x_sample.reshape(ts, D_MODEL)
    outs_p, outs_s = [], []
    for l in range(depth):
        P = _layer_params(l, ln_mix, w_in, hgrn_norm, fox_bf, fox_qnorm, fox_knorm, gdn_conv, gdn_a_log,
                          gdn_dt_bias, gdn_norm, w_out, ln_ffn, w_group, b_group, w_router, b_router,
                          w_gate, w_up, w_down)

        a, bq, bk, bv, c, g = _proj(yp, P["ln_mix"], P["w"], hs512, P["qg"], P["kg"], P["p1"], P["p2"],
                                    tm=min(256, tp))
        qa, ka = _foxprep(bq, bk, g, nb, tm=min(256, seq))
        ob = _fox(qa, ka, bv, nb, tq=min(256, seq))
        oa, hst = _hgrn(a, hgrn_lb, P["hgrn_ng"], None, l, nb, seq // tbp, tbp, 1, tbp // SUB, SUB)
        oc, gst = _gdn(c, g, P["conv"], P["gdn_ng"], None, None, nb, seq // tbp, tbp, 1,
                       tbp // gchunk, gchunk)
        xn, hf, gates, sel = _post(yp, oa, ob, oc, P["w_out"], P["ln_ffn"], P["wr"], P["br"],
                                   tm=min(256, tp))
        yp_new = _moe(hf, gates, sel, xn, P["wg"], P["wu"], P["wd"], tm=min(256, tp))
        outs_p.append((bk.reshape(nb, seq, B_HEADS, HD), bv.reshape(nb, seq, B_HEADS, HD),
                       g[:, 0:8].reshape(nb, seq, B_HEADS), hst, gst,
                       c[:, 0:768].reshape(nb, seq, 768)[:, seq - (CONV_W - 1):, :]))
        yp = yp_new

        a, bq, bk, bv, c, g = _proj(ys, P["ln_mix"], P["w"], hs512, P["qg"], P["kg"], P["p1"], P["p2"],
                                    tm=min(256, ts))
        knew = jnp.pad(jnp.swapaxes(bk.reshape(nsq, dec, 512), 1, 2), ((0, 0), (0, 0), (0, page - dec)))
        vnew = jnp.pad(jnp.swapaxes(bv.reshape(nsq, dec, 512), 1, 2), ((0, 0), (0, 0), (0, page - dec)))
        lfn = jnp.pad(jnp.swapaxes(g[:, 0:8].reshape(nsq, dec, B_HEADS), 1, 2),
                      ((0, 0), (0, 0), (0, page - dec)))
        ob = _foxdec(l, page_table, bq.reshape(nsq, dec, 512), knew, vnew, lfn, ck, cv, clt,
                     pp=min(8, page_table.shape[1])).reshape(ts, 512)
        n_outer = nsq // seq_blk
        oa, hst = _hgrn(_pad_rows(a, nsq, dec), hgrn_lb, P["hgrn_ng"], state_hgrn[l], l,
                        n_outer, 1, seq_blk * SUB, seq_blk, 1, dec)
        cinit = jnp.pad(state_conv[l], ((0, 0), (8 - (CONV_W - 1), 0), (0, 0)))
        oc, gst = _gdn(_pad_rows(c, nsq, dec), _pad_rows(g, nsq, dec), P["conv"], P["gdn_ng"], cinit,
                       state_gdn[l], n_outer, 1, seq_blk * SUB, seq_blk, 1, SUB)
        oa = oa.reshape(nsq, SUB, 256)[:, :dec].reshape(ts, 256)
        oc = oc.reshape(nsq, SUB, 256)[:, :dec].reshape(ts, 256)
        xn, hf, gates, sel = _post(ys, oa, ob, oc, P["w_out"], P["ln_ffn"], P["wr"], P["br"],
                                   tm=min(256, ts))
        ys_new = _moe(hf, gates, sel, xn, P["wg"], P["wu"], P["wd"], tm=min(256, ts))
        xpad = jnp.concatenate([state_conv[l], c[:, 0:768].reshape(nsq, dec, 768)], axis=1)
        outs_s.append((bk.reshape(nsq, dec, B_HEADS, HD), bv.reshape(nsq, dec, B_HEADS, HD),
                       g[:, 0:8].reshape(nsq, dec, B_HEADS), hst, gst, xpad[:, -(CONV_W - 1):, :]))
        ys = ys_new

    stack = lambda outs, j: jnp.stack([o[j] for o in outs], axis=0)
    return (yp.reshape(nb, seq, D_MODEL), ys.reshape(nsq, dec, D_MODEL),
            *[stack(outs_p, j) for j in range(6)], *[stack(outs_s, j) for j in range(6)])
```

```python
import functools
import math

import numpy as np
import jax
import jax.numpy as jnp
from jax import lax
from jax.experimental import pallas as pl
from jax.experimental.pallas import tpu as pltpu

F32 = jnp.float32
BF16 = jnp.bfloat16
EPS = 1e-6
NEG = -1e30

D_MODEL = 1024
HD = 64
A_HEADS = 4
B_HEADS = 8
C_HEADS = 4
CONV_W = 4
N_GROUPS = 4
EXPERTS_PER_GROUP = 8
N_EXPERTS = N_GROUPS * EXPERTS_PER_GROUP
D_EXPERT = D_MODEL // 4
SUB = 16
LANES = 128
VMEM_LIMIT = 56 * 1024 * 1024


def _cp(*sem):
    return pltpu.CompilerParams(dimension_semantics=sem, vmem_limit_bytes=VMEM_LIMIT)


def _dot(a, b):
    return jnp.dot(a, b, preferred_element_type=F32)


def _dot_nt(a, b):
    return lax.dot_general(a, b, (((1,), (1,)), ((), ())), preferred_element_type=F32)


def _dot_tn(a, b):
    return lax.dot_general(a, b, (((0,), (0,)), ((), ())), preferred_element_type=F32)


def _split3(x):
    hi = x.astype(BF16)
    r = x - hi.astype(F32)
    mid = r.astype(BF16)
    lo = (r - mid.astype(F32)).astype(BF16)
    return hi, mid, lo


def _exact_left(m, x):
    hi, mid, lo = _split3(x)
    return (_dot(m, hi) + _dot(m, mid)) + _dot(m, lo)


def _exact_right(x, m):
    hi, mid, lo = _split3(x)
    return (_dot(hi, m) + _dot(mid, m)) + _dot(lo, m)


def _idiv(x, n):
    return jnp.right_shift(x, int(math.log2(n)))


def _imod(x, n):
    return jnp.bitwise_and(x, n - 1)


def _sigmoid(x):
    return 1.0 / (1.0 + jnp.exp(-x))


def _softplus(z):
    return jnp.maximum(z, 0.0) + jnp.log(1.0 + jnp.exp(-jnp.abs(z)))


def _head_ssq(z, hs):
    zz = z * z
    hi = zz.astype(BF16)
    lo = (zz - hi.astype(F32)).astype(BF16)
    return _dot(hi, hs) + _dot(lo, hs)


def _const(a, dtype=BF16):
    return jnp.asarray(a, dtype=dtype)


def _head_sum_matrix(width):
    i = np.arange(width)
    return (i[:, None] // HD == i[None, :] // HD).astype(np.float32)


def _chunk_tril(n, c):
    i = np.arange(n)
    return ((i[:, None] // c == i[None, :] // c) & (i[None, :] <= i[:, None])).astype(np.float32)


PROJ_COLS = 3712


def _proj_kernel(x_ref, ln_ref, w_ref, hs_ref, qg_ref, kg_ref, p1_ref, p2_ref,
                 a_ref, bq_ref, bk_ref, bv_ref, c_ref, g_ref):
    x = x_ref[...]
    ms = jnp.mean(x * x, axis=-1, keepdims=True)
    hn = ((x * lax.rsqrt(ms + EPS)) * ln_ref[...]).astype(BF16)
    a_ref[...] = _dot(hn, w_ref[:, 0:1024])
    hs = hs_ref[...]
    q = _dot(hn, w_ref[:, 1024:1536])
    bq_ref[...] = ((q * lax.rsqrt(_head_ssq(q, hs) * (1.0 / HD) + EPS)) * qg_ref[...]) * (HD ** -0.5)
    k = _dot(hn, w_ref[:, 1536:2048])
    bk_ref[...] = (k * lax.rsqrt(_head_ssq(k, hs) * (1.0 / HD) + EPS)) * kg_ref[...]
    bv_ref[...] = _dot(hn, w_ref[:, 2048:2560])
    c_ref[...] = _dot(hn, w_ref[:, 2560:3584])
    gr = _dot(hn, w_ref[:, 3584:3712])
    lane = lax.broadcasted_iota(jnp.int32, gr.shape, 1)
    z = gr + p1_ref[...]
    sp = _softplus(z)
    logsig = jnp.minimum(z, 0.0) - jnp.log(1.0 + jnp.exp(-jnp.abs(z)))
    glog = -jnp.exp(p2_ref[...]) * sp
    beta = _sigmoid(gr)
    g_ref[...] = jnp.where(lane < 8, logsig,
                           jnp.where(lane < 12, glog, jnp.where(lane < 16, beta, 0.0)))


def _proj(x, ln, w, hs512, qg, kg, p1, p2, tm):
    t = x.shape[0]
    row = lambda i: (i, 0)
    fix = lambda i: (0, 0)
    outs = [jax.ShapeDtypeStruct((t, n), F32) for n in (1024, 512, 512, 512, 1024, 128)]
    return pl.pallas_call(
        _proj_kernel,
        grid=(t // tm,),
        in_specs=[pl.BlockSpec((tm, D_MODEL), row), pl.BlockSpec((1, D_MODEL), fix),
                  pl.BlockSpec((D_MODEL, PROJ_COLS), fix), pl.BlockSpec((512, 512), fix),
                  pl.BlockSpec((1, 512), fix), pl.BlockSpec((1, 512), fix),
                  pl.BlockSpec((1, LANES), fix), pl.BlockSpec((1, LANES), fix)],
        out_specs=[pl.BlockSpec((tm, n), row) for n in (1024, 512, 512, 512, 1024, 128)],
        out_shape=outs,
        compiler_params=_cp("arbitrary"),
        name="proj",
    )(x, ln, w, hs512, qg, kg, p1, p2)


def _foxprep_kernel(bq_ref, bk_ref, g_ref, tril_ref, pq_ref, pcq_ref, pck_ref, oq_ref, ok_ref,
                    qa_ref, ka_ref, carry_ref):
    i = pl.program_id(1)

    @pl.when(i == 0)
    def _():
        carry_ref[...] = jnp.zeros_like(carry_ref)

    g = g_ref[...]
    c = _exact_left(tril_ref[...], g) + carry_ref[...]
    carry_ref[...] = c[-1:, :]
    hi, mid, lo = _split3(c)
    pq = pq_ref[...]
    qa = _dot(bq_ref[...].astype(BF16), pq) + oq_ref[...]
    ka = _dot(bk_ref[...].astype(BF16), pq) + ok_ref[...]
    for j, part in enumerate((hi, mid, lo)):
        qa = qa + _dot(part, pcq_ref[j])
        ka = ka - _dot(part, pck_ref[j])
    qa_ref[...] = qa.astype(BF16)
    ka_ref[...] = ka.astype(BF16)


def _foxprep(bq, bk, g, nb, tm):
    t = bq.shape[0]
    nblk = t // nb // tm
    pq = np.zeros((512, 1024), np.float32)
    for h in range(B_HEADS):
        pq[h * HD + np.arange(HD), h * LANES + np.arange(HD)] = 1.0
    pcq = np.zeros((3, LANES, 1024), np.float32)
    pck = np.zeros((3, LANES, 1024), np.float32)
    oq = np.zeros((1, 1024), np.float32)
    ok = np.zeros((1, 1024), np.float32)
    for h in range(B_HEADS):
        for j in range(3):
            pcq[j, h, h * LANES + HD + j] = 1.0
            pck[j, h, h * LANES + HD + 3 + j] = 1.0
            oq[0, h * LANES + HD + 3 + j] = 1.0
            ok[0, h * LANES + HD + j] = 1.0
    row = lambda b, i: (b * nblk + i, 0)
    fix2 = lambda b, i: (0, 0)
    fix3 = lambda b, i: (0, 0, 0)
    return pl.pallas_call(
        _foxprep_kernel,
        grid=(nb, nblk),
        in_specs=[pl.BlockSpec((tm, 512), row), pl.BlockSpec((tm, 512), row),
                  pl.BlockSpec((tm, LANES), row), pl.BlockSpec((tm, tm), fix2),
                  pl.BlockSpec((512, 1024), fix2), pl.BlockSpec((3, LANES, 1024), fix3),
                  pl.BlockSpec((3, LANES, 1024), fix3), pl.BlockSpec((1, 1024), fix2),
                  pl.BlockSpec((1, 1024), fix2)],
        out_specs=[pl.BlockSpec((tm, 1024), row), pl.BlockSpec((tm, 1024), row)],
        out_shape=[jax.ShapeDtypeStruct((t, 1024), BF16)] * 2,
        scratch_shapes=[pltpu.VMEM((1, LANES), F32)],
        compiler_params=_cp("arbitrary", "arbitrary"),
        name="foxprep",
    )(bq, bk, g, _const(_chunk_tril(tm, tm)), _const(pq), _const(pcq), _const(pck),
      _const(oq, F32), _const(ok, F32))


def _fox_kernel(qt_ref, kt_ref, q_ref, k_ref, v_ref, o_ref, m_ref, l_ref, acc_ref, *, tq):
    j = pl.program_id(1)
    qi = qt_ref[j]
    ki = kt_ref[j]

    @pl.when(ki == 0)
    def _():
        m_ref[...] = jnp.full_like(m_ref, NEG)
        l_ref[...] = jnp.zeros_like(l_ref)
        acc_ref[...] = jnp.zeros_like(acc_ref)

    heads = range(B_HEADS)
    lane = lax.broadcasted_iota(jnp.int32, (tq, LANES), 1)
    left = lane < HD

    def update(masked):
        if masked:
            rows = lax.broadcasted_iota(jnp.int32, (tq, tq), 0)
            cols = lax.broadcasted_iota(jnp.int32, (tq, tq), 1)
            keep = cols <= rows
        for pr in range(B_HEADS // 2):
            pair = (2 * pr, 2 * pr + 1)
            ss = [_dot_nt(q_ref[:, h * LANES:(h + 1) * LANES], k_ref[:, h * LANES:(h + 1) * LANES])
                  for h in pair]
            if masked:
                ss = [jnp.where(keep, s, NEG) for s in ss]
            ps, alphas = [], []
            for s, h in zip(ss, pair):
                m_old = m_ref[h]
                m_new = jnp.maximum(m_old, jnp.max(s, axis=1, keepdims=True))
                p = jnp.exp(s - m_new[:, 0:1])
                alpha = jnp.exp(m_old - m_new)
                l_ref[h] = alpha * l_ref[h] + jnp.sum(p, axis=1, keepdims=True)
                m_ref[h] = m_new
                ps.append(p.astype(BF16))
                alphas.append(alpha)
            vp = v_ref[:, pr * LANES:(pr + 1) * LANES]
            upd = _dot(ps[0], jnp.where(left, vp, 0.0).astype(BF16))
            upd = upd + _dot(ps[1], jnp.where(left, 0.0, vp).astype(BF16))
            a = jnp.where(left, alphas[0], alphas[1])
            acc_ref[:, pr * LANES:(pr + 1) * LANES] = a * acc_ref[:, pr * LANES:(pr + 1) * LANES] + upd

    @pl.when(ki < qi)
    def _():
        update(False)

    @pl.when(ki == qi)
    def _():
        update(True)
        for pr in range(B_HEADS // 2):
            l = jnp.where(left, l_ref[2 * pr], l_ref[2 * pr + 1])
            o_ref[:, pr * LANES:(pr + 1) * LANES] = acc_ref[:, pr * LANES:(pr + 1) * LANES] / l


def _fox(qa, ka, bv, nb, tq):
    t = qa.shape[0]
    nblk = t // nb // tq
    qi_tab = np.array([q for q in range(nblk) for _ in range(q + 1)], np.int32)
    ki_tab = np.array([k for q in range(nblk) for k in range(q + 1)], np.int32)
    grid_spec = pltpu.PrefetchScalarGridSpec(
        num_scalar_prefetch=2,
        grid=(nb, len(qi_tab)),
        in_specs=[pl.BlockSpec((tq, 1024), lambda b, j, qt, kt: (b * nblk + qt[j], 0)),
                  pl.BlockSpec((tq, 1024), lambda b, j, qt, kt: (b * nblk + kt[j], 0)),
                  pl.BlockSpec((tq, 512), lambda b, j, qt, kt: (b * nblk + kt[j], 0))],
        out_specs=pl.BlockSpec((tq, 512), lambda b, j, qt, kt: (b * nblk + qt[j], 0)),
        scratch_shapes=[pltpu.VMEM((B_HEADS, tq, LANES), F32), pltpu.VMEM((B_HEADS, tq, LANES), F32),
                        pltpu.VMEM((tq, 512), F32)],
    )
    return pl.pallas_call(
        functools.partial(_fox_kernel, tq=tq),
        grid_spec=grid_spec,
        out_shape=jax.ShapeDtypeStruct((t, 512), F32),
        compiler_params=_cp("arbitrary", "arbitrary"),
        name="fox",
    )(jnp.asarray(qi_tab), jnp.asarray(ki_tab), qa, ka, bv)


def _foxdec_kernel(pt_ref, q_ref, kn_ref, vn_ref, lfn_ref, cum_ref, *rest, pp, dec):
    kp = rest[0:pp]
    vp = rest[pp:2 * pp]
    lp = rest[2 * pp:3 * pp]
    o_ref = rest[3 * pp]
    qr_ref, m_ref, l_ref, r_ref, acc_ref = rest[3 * pp + 1:]
    step = pl.program_id(1)
    nrow = dec * B_HEADS
    rowi = lax.broadcasted_iota(jnp.int32, (nrow, 512), 0)
    coli = lax.broadcasted_iota(jnp.int32, (nrow, 512), 1)
    bd = _idiv(coli, HD) == _imod(rowi, B_HEADS)

    @pl.when(step == 0)
    def _():
        q = q_ref[0]
        qrows = jnp.concatenate(
            [jnp.broadcast_to(q[t:t + 1, :], (B_HEADS, 512)) for t in range(dec)], axis=0)
        qr_ref[...] = jnp.where(bd, qrows, 0.0).astype(BF16)
        m_ref[...] = jnp.full_like(m_ref, NEG)
        l_ref[...] = jnp.zeros_like(l_ref)
        r_ref[...] = jnp.zeros_like(r_ref)
        acc_ref[...] = jnp.zeros_like(acc_ref)

    def pages(krefs, vrefs, lrefs, causal):
        n = len(krefs)
        qr = qr_ref[...]
        ss = [_dot(qr, kr().astype(BF16)) for kr in krefs]
        cums = [_exact_right(lr(), cum_ref[...]) for lr in lrefs]
        r = r_ref[...]
        logits = []
        for u in range(n):
            w = jnp.concatenate([cums[u][:, 0:LANES]] * dec, axis=0)
            logit = ss[u] - (r + w)
            if causal:
                key = lax.broadcasted_iota(jnp.int32, (nrow, LANES), 1)
                trow = _idiv(lax.broadcasted_iota(jnp.int32, (nrow, LANES), 0), B_HEADS)
                logit = jnp.where(key <= trow, logit, NEG)
            logits.append(logit)
            r = r + jnp.concatenate([cums[u][:, LANES:2 * LANES]] * dec, axis=0)
        r_ref[...] = r
        m_old = m_ref[...]
        m_new = m_old
        for lg in logits:
            m_new = jnp.maximum(m_new, jnp.max(lg, axis=1, keepdims=True))
        ps = [jnp.exp(lg - m_new) for lg in logits]
        alpha = jnp.exp(m_old - m_new)
        lsum = jnp.sum(ps[0], axis=1, keepdims=True)
        for p in ps[1:]:
            lsum = lsum + jnp.sum(p, axis=1, keepdims=True)
        l_ref[...] = alpha * l_ref[...] + lsum
        m_ref[...] = m_new
        pv = _dot_nt(ps[0].astype(BF16), vrefs[0]().astype(BF16))
        for u in range(1, n):
            pv = pv + _dot_nt(ps[u].astype(BF16), vrefs[u]().astype(BF16))
        acc_ref[...] = alpha[:, 0:1] * acc_ref[...] + pv

    pages([lambda u=u: kp[u][...] for u in range(pp)], [lambda u=u: vp[u][...] for u in range(pp)],
          [lambda u=u: lp[u][...] for u in range(pp)], False)

    @pl.when(step == pl.num_programs(1) - 1)
    def _():
        pages([lambda: kn_ref[0]], [lambda: vn_ref[0]], [lambda: lfn_ref[0]], True)
        res = jnp.where(bd, acc_ref[...] / l_ref[:, 0:1], 0.0)
        for t in range(dec):
            o_ref[0, t:t + 1, :] = jnp.sum(res[t * B_HEADS:(t + 1) * B_HEADS], axis=0, keepdims=True)


def _foxdec(layer, page_table, q, knew, vnew, lfn_t, cache_k, cache_v, cache_lft, pp):
    ns, dec, _ = q.shape
    n_pages = page_table.shape[1]
    page = cache_k.shape[3]
    nsteps = n_pages // pp
    nrow = dec * B_HEADS
    cum = np.concatenate([np.triu(np.ones((page, page), np.float32)), np.ones((page, page), np.float32)], axis=1)
    seq3 = lambda s, p, pt: (s, 0, 0)
    fix2 = lambda s, p, pt: (0, 0)

    def cache_map(u):
        return lambda s, p, pt: (layer, pt[s, p * pp + u], 0, 0)

    in_specs = [pl.BlockSpec((1, dec, 512), seq3), pl.BlockSpec((1, 512, page), seq3),
                pl.BlockSpec((1, 512, page), seq3), pl.BlockSpec((1, B_HEADS, page), seq3),
                pl.BlockSpec((page, 2 * page), fix2)]
    in_specs += [pl.BlockSpec((None, None, 512, page), cache_map(u)) for u in range(pp)]
    in_specs += [pl.BlockSpec((None, None, 512, page), cache_map(u)) for u in range(pp)]
    in_specs += [pl.BlockSpec((None, None, B_HEADS, page), cache_map(u)) for u in range(pp)]
    grid_spec = pltpu.PrefetchScalarGridSpec(
        num_scalar_prefetch=1,
        grid=(ns, nsteps),
        in_specs=in_specs,
        out_specs=pl.BlockSpec((1, dec, 512), seq3),
        scratch_shapes=[pltpu.VMEM((nrow, 512), BF16), pltpu.VMEM((nrow, LANES), F32),
                        pltpu.VMEM((nrow, LANES), F32), pltpu.VMEM((nrow, LANES), F32),
                        pltpu.VMEM((nrow, 512), F32)],
    )
    return pl.pallas_call(
        functools.partial(_foxdec_kernel, pp=pp, dec=dec),
        grid_spec=grid_spec,
        out_shape=jax.ShapeDtypeStruct((ns, dec, 512), F32),
        compiler_params=_cp("arbitrary", "arbitrary"),
        name="foxdec",
    )(page_table, q, knew, vnew, lfn_t, _const(cum),
      *([cache_k] * pp), *([cache_v] * pp), *([cache_lft] * pp))


def _hgrn_kernel(*refs, layer, ns, cps, valid, has_init):
    (q_ref, f_ref, v_ref, gate_ref, lbraw_ref, ng_ref, tril_ref, rs_ref, tile_ref, hs_ref) = refs[:10]
    pos = 10
    s0_ref = None
    if has_init:
        s0_ref = refs[pos]
        pos += 1
    o_ref, sout_ref, st_ref, oin_ref, oint_ref, tmp_ref = refs[pos:pos + 6]
    i = pl.program_id(1)
    tb = q_ref.shape[0]
    nchunk = tb // SUB
    width = A_HEADS * HD

    lbraw = lbraw_ref[...]
    e = jnp.exp(lbraw - jnp.max(lbraw, axis=0, keepdims=True))
    prob = e / jnp.sum(e, axis=0, keepdims=True)
    lb = jnp.zeros((1, width), F32)
    for d in range(1, layer + 1):
        lb = lb + prob[d:d + 1, :]

    fr = f_ref[...]
    qr = q_ref[...]
    vr = v_ref[...]
    logf = jnp.log(lb + (1.0 - lb) * _sigmoid(fr))
    key = (1.0 - lb) * _sigmoid(-fr)
    if valid < SUB:
        rowv = _imod(lax.broadcasted_iota(jnp.int32, (tb, width), 0), SUB) < valid
        logf = jnp.where(rowv, logf, 0.0)
        key = jnp.where(rowv, key, 0.0)
    q = (qr * _sigmoid(qr)) * (HD ** -0.5)
    g = _exact_left(tril_ref[...], logf)
    g3 = g.reshape(nchunk, SUB, width)
    q3 = q.reshape(nchunk, SUB, width)
    k3 = key.reshape(nchunk, SUB, width)

    att = jnp.zeros((tb, LANES), F32)
    for s in range(SUB):
        d = jnp.minimum(g3 - g3[:, s:s + 1, :], 0.0)
        p = (q3 * jnp.exp(d)) * k3[:, s:s + 1, :]
        att = att + _dot(p.reshape(tb, width).astype(BF16), rs_ref[s])
    attb = att.astype(BF16)

    rows = lax.broadcasted_iota(jnp.int32, (tb, tb), 0)
    cols = lax.broadcasted_iota(jnp.int32, (tb, tb), 1)
    causal = (_idiv(rows, SUB) == _idiv(cols, SUB)) & (cols <= rows)
    lane = lax.broadcasted_iota(jnp.int32, (tb, LANES), 1)
    for pr in range(2):
        vpair = vr[:, pr * LANES:(pr + 1) * LANES]
        acc = None
        for hh in range(2):
            h = 2 * pr + hh
            full = _dot(attb, tile_ref[h])
            full = jnp.where(causal, full, 0.0).astype(BF16)
            vm = jnp.where((lane < HD) == (hh == 0), vpair, 0.0).astype(BF16)
            term = _dot(full, vm)
            acc = term if acc is None else acc + term
        oin_ref[:, pr * LANES:(pr + 1) * LANES] = acc

    gend3 = g3[:, SUB - 1:SUB, :]
    kd = (k3 * jnp.exp(gend3 - g3)).reshape(tb, width).astype(BF16)
    eg = jnp.exp(g)
    qg = (q * eg).astype(BF16)
    r128 = lax.broadcasted_iota(jnp.int32, (LANES, LANES), 0)
    c128 = lax.broadcasted_iota(jnp.int32, (LANES, LANES), 1)
    bdm = _idiv(r128, HD) == _idiv(c128, HD)
    vb = vr.astype(BF16)
    last = i == pl.num_programs(1) - 1
    units = [(sq, pr) for sq in range(ns) for pr in range(2)]
    rowsl = lambda sq, c: slice((sq * cps + c) * SUB, (sq * cps + c + 1) * SUB)
    lanesl = lambda pr: slice(pr * LANES, (pr + 1) * LANES)
    incs = {}
    for sq, pr in units:
        for c in range(cps):
            u = _dot_tn(vb[rowsl(sq, c), lanesl(pr)], kd[rowsl(sq, c), lanesl(pr)])
            incs[sq, pr, c] = jnp.where(bdm, u, 0.0)
    if has_init:
        tmp_ref[...] = jnp.zeros_like(tmp_ref)
        for sq, pr in units:
            tmp_ref[2 * sq + pr, 0:HD, 0:HD] = s0_ref[sq, 2 * pr]
            tmp_ref[2 * sq + pr, HD:LANES, HD:LANES] = s0_ref[sq, 2 * pr + 1]
    seen = {}
    final = {}
    for sq, pr in units:
        if has_init:
            st = tmp_ref[2 * sq + pr].T
        else:
            st = jnp.where(i == 0, 0.0, st_ref[pr])
        for c in range(cps):
            seen[sq, pr, c] = st.astype(BF16)
            r_end = (sq * cps + c + 1) * SUB
            st = st * eg[r_end - 1:r_end, lanesl(pr)] + incs[sq, pr, c]
        if not has_init:
            st_ref[pr] = st
        final[sq, pr] = st
    for sq, pr in units:
        for c in range(cps):
            oint_ref[rowsl(sq, c), lanesl(pr)] = _dot_nt(qg[rowsl(sq, c), lanesl(pr)], seen[sq, pr, c])

    @pl.when(last)
    def _():
        for sq, pr in units:
            tmp_ref[2 * sq + pr] = final[sq, pr].T
        for sq, pr in units:
            sout_ref[sq, 2 * pr] = tmp_ref[2 * sq + pr, 0:HD, 0:HD]
            sout_ref[sq, 2 * pr + 1] = tmp_ref[2 * sq + pr, HD:LANES, HD:LANES]

    o = oin_ref[...] + oint_ref[...]
    gr = gate_ref[...]
    o = (o * lax.rsqrt(_head_ssq(o, hs_ref[...]) * (1.0 / HD) + EPS)) * ng_ref[...]
    o_ref[...] = o * (gr * _sigmoid(gr))


def _hgrn(a, lbraw, ng, s0, layer, n_outer, n_inner, tb, ns, cps, valid):
    rows = a.shape[0]
    width = A_HEADS * HD
    has_init = s0 is not None
    nseq = n_outer * ns
    rs = np.zeros((SUB, width, LANES), np.float32)
    tile = np.zeros((A_HEADS, LANES, tb), np.float32)
    for s in range(SUB):
        for h in range(A_HEADS):
            rs[s, h * HD + np.arange(HD), h * SUB + s] = 1.0
            tile[h, h * SUB + s, np.arange(tb // SUB) * SUB + s] = 1.0

    def col(j):
        return lambda o, i: (o * n_inner + i, j)

    fix2 = lambda o, i: (0, 0)
    fix3 = lambda o, i: (0, 0, 0)
    in_specs = [pl.BlockSpec((tb, width), col(j)) for j in range(4)]
    in_specs += [pl.BlockSpec(lbraw.shape, fix2), pl.BlockSpec((1, width), fix2),
                 pl.BlockSpec((tb, tb), fix2), pl.BlockSpec((SUB, width, LANES), fix3),
                 pl.BlockSpec((A_HEADS, LANES, tb), fix3), pl.BlockSpec((width, width), fix2)]
    args = [a, a, a, a, lbraw, ng, _const(_chunk_tril(tb, SUB)), _const(rs), _const(tile),
            _const(_head_sum_matrix(width))]
    if has_init:
        in_specs.append(pl.BlockSpec((ns, A_HEADS, HD, HD), lambda o, i: (o, 0, 0, 0)))
        args.append(s0)
    return pl.pallas_call(
        functools.partial(_hgrn_kernel, layer=layer, ns=ns, cps=cps, valid=valid, has_init=has_init),
        grid=(n_outer, n_inner),
        in_specs=in_specs,
        out_specs=[pl.BlockSpec((tb, width), lambda o, i: (o * n_inner + i, 0)),
                   pl.BlockSpec((ns, A_HEADS, HD, HD), lambda o, i: (o, 0, 0, 0))],
        out_shape=[jax.ShapeDtypeStruct((rows, width), F32),
                   jax.ShapeDtypeStruct((nseq, A_HEADS, HD, HD), F32)],
        scratch_shapes=[pltpu.VMEM((2, LANES, LANES), F32), pltpu.VMEM((tb, width), F32),
                        pltpu.VMEM((tb, width), F32), pltpu.VMEM((2 * ns, LANES, LANES), F32)],
        compiler_params=_cp("arbitrary", "arbitrary"),
        name="hgrn",
    )(*args)


def _gdn_kernel(*refs, ns, cps, chunk, has_init):
    (x_ref, z_ref, g_ref, cw_ref, ng_ref, tril_ref, hs_ref, eg_ref, eb_ref, eg2_ref, eb2_ref) = refs[:11]
    pos = 11
    cinit_ref = s0_ref = None
    if has_init:
        cinit_ref, s0_ref = refs[pos:pos + 2]
        pos += 2
    o_ref, sout_ref, st_ref, cv_ref, oacc_ref, vn_ref, tmp_ref = refs[pos:pos + 7]
    i = pl.program_id(1)
    tb = x_ref.shape[0]
    width = C_HEADS * HD
    rows_seq = cps * chunk
    last = i == pl.num_programs(1) - 1

    x = x_ref[...]
    cw = cw_ref[...]
    acts = []
    for sq in range(ns):
        xs = x[sq * rows_seq:(sq + 1) * rows_seq]
        if has_init:
            prev = cinit_ref[sq]
        else:
            prev = jnp.where(i == 0, 0.0, cv_ref[...])
        xc = jnp.concatenate([prev, xs], axis=0)
        conv = xc * cw[CONV_W - 1:CONV_W, :]
        for j in range(1, CONV_W):
            conv = conv + pltpu.roll(xc, j, 0) * cw[CONV_W - 1 - j:CONV_W - j, :]
        acts.append(conv[8:])
        if not has_init:
            cv_ref[...] = xs[rows_seq - 8:]
    conv = acts[0] if ns == 1 else jnp.concatenate(acts, axis=0)
    act = conv * _sigmoid(conv)
    hs = hs_ref[...]
    q = act[:, 0:width]
    k = act[:, width:2 * width]
    v = act[:, 2 * width:3 * width]
    q = (q * lax.rsqrt(_head_ssq(q, hs) + EPS)) * (HD ** -0.5)
    k = k * lax.rsqrt(_head_ssq(k, hs) + EPS)

    gt = g_ref[...]
    gc = _exact_left(tril_ref[...], gt)
    gexp = _exact_right(gc, eg_ref[...])
    bexp = _exact_right(gt, eb_ref[...])
    gcol = _exact_right(gc, eg2_ref[...])
    bcol = _exact_right(gt, eb2_ref[...])
    nchunk = tb // chunk
    gexp3 = gexp.reshape(nchunk, chunk, width)
    gend3 = gexp3[:, chunk - 1:chunk, :]
    eg = jnp.exp(gexp)
    rhs = jnp.concatenate([v * bexp, (k * bexp) * eg], axis=1).astype(BF16)
    qg = (q * eg).astype(BF16)
    kd = (k.reshape(nchunk, chunk, width) * jnp.exp(gend3 - gexp3)).reshape(tb, width).astype(BF16)
    kb = k.astype(BF16)
    lane = lax.broadcasted_iota(jnp.int32, (tb, width), 1)
    kmask = [jnp.where(_idiv(lane, HD) == h, k, 0.0).astype(BF16) for h in range(C_HEADS)]
    qmask = [jnp.where(_idiv(lane, HD) == h, q, 0.0).astype(BF16) for h in range(C_HEADS)]

    rc = lax.broadcasted_iota(jnp.int32, (tb, tb), 0)
    cc = lax.broadcasted_iota(jnp.int32, (tb, tb), 1)
    same = _idiv(rc, chunk) == _idiv(cc, chunk)
    eye = rc == cc
    incl = same & (cc <= rc)
    strict = same & (cc < rc)
    hl = _idiv(lane, HD)
    r256 = lax.broadcasted_iota(jnp.int32, (width, width), 0)
    c256 = lax.broadcasted_iota(jnp.int32, (width, width), 1)
    bdm = _idiv(r256, HD) == _idiv(c256, HD)
    nstage = int(math.log2(chunk))
    heads = range(C_HEADS)
    reps = tb // LANES
    gct = gc.T
    decay, xm, tm, qkd = [], [], [], []
    for h in heads:
        g_t = jnp.concatenate([gcol[:, h * LANES:(h + 1) * LANES]] * reps, axis=1)
        decay.append(jnp.exp(jnp.minimum(g_t - gct[8 + h:9 + h, :], 0.0)))
    kk = [_dot_nt(kmask[h], kb) for h in heads]
    qk = [_dot_nt(qmask[h], kb) for h in heads]
    for h in heads:
        b_t = jnp.concatenate([bcol[:, h * LANES:(h + 1) * LANES]] * reps, axis=1)
        x0 = -jnp.where(strict, (b_t * kk[h]) * decay[h], 0.0)
        xm.append(x0)
        tm.append(jnp.where(eye, 1.0, 0.0) + x0)
        qkd.append(jnp.where(incl, qk[h] * decay[h], 0.0).astype(BF16))
    for _ in range(nstage - 1):
        xb = [x.astype(BF16) for x in xm]
        xm = [_dot(xb[h], xb[h]) for h in heads]
        tm = [tm[h] + _dot(tm[h].astype(BF16), xm[h].astype(BF16)) for h in heads]
    sol = [_dot(tm[h].astype(BF16), rhs) for h in heads]
    u = jnp.zeros((tb, width), F32)
    w = jnp.zeros((tb, width), F32)
    for h in heads:
        u = u + jnp.where(hl == h, sol[h][:, 0:width], 0.0)
        w = w + jnp.where(hl == h, sol[h][:, width:2 * width], 0.0)
    wb = w.astype(BF16)

    if has_init:
        tmp_ref[...] = jnp.zeros_like(tmp_ref)
        for sq in range(ns):
            for h in heads:
                tmp_ref[sq, h * HD:(h + 1) * HD, h * HD:(h + 1) * HD] = s0_ref[sq, h]
        sts = [tmp_ref[sq] for sq in range(ns)]
    else:
        sts = [jnp.where(i == 0, 0.0, st_ref[...])]
    rowsl = lambda sq, c: slice((sq * cps + c) * chunk, (sq * cps + c + 1) * chunk)
    for c in range(cps):
        stb = [st.astype(BF16) for st in sts]
        ws = [_dot(wb[rowsl(sq, c)], stb[sq]) for sq in range(ns)]
        for sq in range(ns):
            oacc_ref[rowsl(sq, c), :] = _dot(qg[rowsl(sq, c)], stb[sq])
        vns = [(u[rowsl(sq, c)] - ws[sq]).astype(BF16) for sq in range(ns)]
        for sq in range(ns):
            vn_ref[rowsl(sq, c), :] = vns[sq]
        incs = [jnp.where(bdm, _dot_tn(kd[rowsl(sq, c)], vns[sq]), 0.0) for sq in range(ns)]
        sts = [sts[sq] * eg[(sq * cps + c + 1) * chunk - 1:(sq * cps + c + 1) * chunk, :] + incs[sq]
               for sq in range(ns)]
    if not has_init:
        st_ref[...] = sts[0]

    @pl.when(last)
    def _():
        for sq in range(ns):
            tmp_ref[sq] = sts[sq]
        for sq in range(ns):
            for h in heads:
                sout_ref[sq, h] = tmp_ref[sq, h * HD:(h + 1) * HD, h * HD:(h + 1) * HD]

    vnb = vn_ref[...]
    o = oacc_ref[...]
    for h in heads:
        o = o + jnp.where(hl == h, _dot(qkd[h], vnb), 0.0)
    z = z_ref[...]
    o = (o * lax.rsqrt(_head_ssq(o, hs) * (1.0 / HD) + EPS)) * ng_ref[...]
    o_ref[...] = o * (z * _sigmoid(z))


def _gdn(cfull, g, cw, ng, cinit, s0, n_outer, n_inner, tb, ns, cps, chunk):
    rows = cfull.shape[0]
    width = C_HEADS * HD
    has_init = s0 is not None
    nseq = n_outer * ns
    eg = np.zeros((LANES, width), np.float32)
    eb = np.zeros((LANES, width), np.float32)
    eg2 = np.zeros((LANES, C_HEADS * LANES), np.float32)
    eb2 = np.zeros((LANES, C_HEADS * LANES), np.float32)
    for h in range(C_HEADS):
        eg[8 + h, h * HD:(h + 1) * HD] = 1.0
        eb[12 + h, h * HD:(h + 1) * HD] = 1.0
        eg2[8 + h, h * LANES:(h + 1) * LANES] = 1.0
        eb2[12 + h, h * LANES:(h + 1) * LANES] = 1.0
    rowm = lambda o, i: (o * n_inner + i, 0)
    fix2 = lambda o, i: (0, 0)
    in_specs = [pl.BlockSpec((tb, 3 * width), rowm),
                pl.BlockSpec((tb, width), lambda o, i: (o * n_inner + i, 3)),
                pl.BlockSpec((tb, LANES), rowm), pl.BlockSpec((CONV_W, 3 * width), fix2),
                pl.BlockSpec((1, width), fix2), pl.BlockSpec((tb, tb), fix2),
                pl.BlockSpec((width, width), fix2), pl.BlockSpec((LANES, width), fix2),
                pl.BlockSpec((LANES, width), fix2), pl.BlockSpec((LANES, C_HEADS * LANES), fix2),
                pl.BlockSpec((LANES, C_HEADS * LANES), fix2)]
    args = [cfull, cfull, g, cw, ng, _const(_chunk_tril(tb, chunk)), _const(_head_sum_matrix(width)),
            _const(eg), _const(eb), _const(eg2), _const(eb2)]
    if has_init:
        in_specs += [pl.BlockSpec((ns, 8, 3 * width), lambda o, i: (o, 0, 0)),
                     pl.BlockSpec((ns, C_HEADS, HD, HD), lambda o, i: (o, 0, 0, 0))]
        args += [cinit, s0]
    return pl.pallas_call(
        functools.partial(_gdn_kernel, ns=ns, cps=cps, chunk=chunk, has_init=has_init),
        grid=(n_outer, n_inner),
        in_specs=in_specs,
        out_specs=[pl.BlockSpec((tb, width), rowm),
                   pl.BlockSpec((ns, C_HEADS, HD, HD), lambda o, i: (o, 0, 0, 0))],
        out_shape=[jax.ShapeDtypeStruct((rows, width), F32),
                   jax.ShapeDtypeStruct((nseq, C_HEADS, HD, HD), F32)],
        scratch_shapes=[pltpu.VMEM((width, width), F32), pltpu.VMEM((8, 3 * width), F32),
                        pltpu.VMEM((tb, width), F32), pltpu.VMEM((tb, width), BF16),
                        pltpu.VMEM((ns, width, width), F32)],
        compiler_params=_cp("arbitrary", "arbitrary"),
        name="gdn",
    )(*args)


def _post_kernel(x_ref, oa_ref, ob_ref, oc_ref, wo_ref, ln_ref, wr_ref, br_ref,
                 xn_ref, hf_ref, gate_ref, sel_ref):
    x = x_ref[...]
    mix = _dot(oa_ref[...].astype(BF16), wo_ref[0:256, :])
    mix = mix + _dot(ob_ref[...].astype(BF16), wo_ref[256:768, :])
    mix = mix + _dot(oc_ref[...].astype(BF16), wo_ref[768:1024, :])
    xn = x + mix
    xn_ref[...] = xn
    ms = jnp.mean(xn * xn, axis=-1, keepdims=True)
    hf = (xn * lax.rsqrt(ms + EPS)) * ln_ref[...]
    hf_ref[...] = hf
    logits = _dot(hf.astype(BF16), wr_ref[...]) + br_ref[...]
    lane = lax.broadcasted_iota(jnp.int32, logits.shape, 1)
    big = jnp.int32(1 << 20)
    isg = (lane >= N_EXPERTS) & (lane < N_EXPERTS + N_GROUPS)
    gl = jnp.where(isg, logits, NEG)
    gm = jnp.max(gl, axis=1, keepdims=True)
    gidx = jnp.min(jnp.where(isg & (gl == gm), lane, big), axis=1, keepdims=True) - N_EXPERTS
    top_gp = 1.0 / jnp.sum(jnp.where(isg, jnp.exp(gl - gm), 0.0), axis=1, keepdims=True)
    ing = (lane < N_EXPERTS) & (_idiv(lane, EXPERTS_PER_GROUP) == gidx)
    el = jnp.where(ing, logits, NEG)
    em = jnp.max(el, axis=1, keepdims=True)
    ee = jnp.where(ing, jnp.exp(el - em), 0.0)
    prob = ee / jnp.sum(ee, axis=1, keepdims=True)
    p1 = jnp.max(prob, axis=1, keepdims=True)
    i1 = jnp.min(jnp.where(ing & (prob == p1), lane, big), axis=1, keepdims=True)
    rest = jnp.where(ing & (lane != i1), prob, -1.0)
    p2 = jnp.max(rest, axis=1, keepdims=True)
    i2 = jnp.min(jnp.where(ing & (lane != i1) & (rest == p2), lane, big), axis=1, keepdims=True)
    den = p1 + p2
    gate_ref[...] = jnp.where(lane == i1, (top_gp * p1) / den,
                              jnp.where(lane == i2, (top_gp * p2) / den, 0.0))
    sel_ref[...] = jnp.where((lane == i1) | (lane == i2), 1.0, 0.0)


def _post(x, oa, ob, oc, wo, ln, wr, br, tm):
    t = x.shape[0]
    row = lambda i: (i, 0)
    fix = lambda i: (0, 0)
    return pl.pallas_call(
        _post_kernel,
        grid=(t // tm,),
        in_specs=[pl.BlockSpec((tm, D_MODEL), row), pl.BlockSpec((tm, 256), row),
                  pl.BlockSpec((tm, 512), row), pl.BlockSpec((tm, 256), row),
                  pl.BlockSpec((D_MODEL, D_MODEL), fix), pl.BlockSpec((1, D_MODEL), fix),
                  pl.BlockSpec((D_MODEL, LANES), fix), pl.BlockSpec((1, LANES), fix)],
        out_specs=[pl.BlockSpec((tm, D_MODEL), row), pl.BlockSpec((tm, D_MODEL), row),
                   pl.BlockSpec((tm, LANES), row), pl.BlockSpec((tm, LANES), row)],
        out_shape=[jax.ShapeDtypeStruct((t, D_MODEL), F32), jax.ShapeDtypeStruct((t, D_MODEL), F32),
                   jax.ShapeDtypeStruct((t, LANES), F32), jax.ShapeDtypeStruct((t, LANES), F32)],
        compiler_params=_cp("arbitrary"),
        name="post",
    )(x, oa, ob, oc, wo, ln, wr, br)


MOE_TM = 256


def _moe_rank_kernel(sel_ref, tril_ref, rank_ref, cnt_ref, carry_ref):
    @pl.when(pl.program_id(0) == 0)
    def _():
        carry_ref[...] = jnp.zeros_like(carry_ref)

    sel = sel_ref[...]
    rank_ref[...] = _dot(tril_ref[...], sel.astype(BF16)) + carry_ref[...]
    carry_ref[...] = carry_ref[...] + jnp.sum(sel, axis=0, keepdims=True)
    cnt_ref[...] = carry_ref[...]


def _moe_rank(sel, tm):
    t = sel.shape[0]
    strict = np.tril(np.ones((tm, tm), np.float32), -1)
    return pl.pallas_call(
        _moe_rank_kernel,
        grid=(t // tm,),
        in_specs=[pl.BlockSpec((tm, LANES), lambda i: (i, 0)), pl.BlockSpec((tm, tm), lambda i: (0, 0))],
        out_specs=[pl.BlockSpec((tm, LANES), lambda i: (i, 0)), pl.BlockSpec((1, LANES), lambda i: (0, 0))],
        out_shape=[jax.ShapeDtypeStruct((t, LANES), F32), jax.ShapeDtypeStruct((1, LANES), F32)],
        scratch_shapes=[pltpu.VMEM((1, LANES), F32)],
        compiler_params=_cp("arbitrary"),
        name="moe_rank",
    )(sel, _const(strict))


def _moe_dest_kernel(sel_ref, gate_ref, rank_ref, off_ref, info_ref):
    on = sel_ref[...] > 0.0
    gates = gate_ref[...]
    dest = off_ref[...] + rank_ref[...]
    lane = lax.broadcasted_iota(jnp.int32, gates.shape, 1)
    la = jnp.min(jnp.where(on, lane, LANES), axis=1, keepdims=True)
    lb = jnp.max(jnp.where(on, lane, -1), axis=1, keepdims=True)
    pick = lambda v, l: jnp.sum(jnp.where(lane == l, v, 0.0), axis=1, keepdims=True)
    info_ref[...] = jnp.where(lane == 0, pick(dest, la),
                              jnp.where(lane == 1, pick(dest, lb),
                                        jnp.where(lane == 2, pick(gates, la),
                                                  jnp.where(lane == 3, pick(gates, lb), 0.0))))


def _moe_dest(sel, gates, rank, off, tm):
    t = sel.shape[0]
    row = lambda i: (i, 0)
    return pl.pallas_call(
        _moe_dest_kernel,
        grid=(t // tm,),
        in_specs=[pl.BlockSpec((tm, LANES), row)] * 3 + [pl.BlockSpec((1, LANES), lambda i: (0, 0))],
        out_specs=pl.BlockSpec((tm, LANES), row),
        out_shape=jax.ShapeDtypeStruct((t, LANES), F32),
        compiler_params=_cp("arbitrary"),
        name="moe_dest",
    )(sel, gates, rank, off)


def _row_copy(src_ref, src_row, dst_ref, dst_row, sem):
    return pltpu.make_async_copy(src_ref.at[pl.ds(src_row, 1), :], dst_ref.at[pl.ds(dst_row, 1), :], sem)


def _rows_wait(src_ref, dst_ref, nrows, sem):
    pltpu.make_async_copy(src_ref.at[pl.ds(0, nrows), :], dst_ref.at[pl.ds(0, nrows), :], sem).wait()


def _moe_dispatch_kernel(da_ref, db_ref, hf_ref, xs_in_ref, xs_ref, sem):
    del xs_in_ref
    tm = da_ref.shape[2]
    i = pl.program_id(0)
    slot = i % 2
    base = i * tm

    def issue(r, carry):
        _row_copy(hf_ref, base + r, xs_ref, da_ref[0, 0, r], sem.at[slot]).start()
        _row_copy(hf_ref, base + r, xs_ref, db_ref[0, 0, r], sem.at[slot]).start()
        return carry

    lax.fori_loop(0, tm, issue, 0, unroll=4)

    @pl.when(i > 0)
    def _():
        _rows_wait(hf_ref, xs_ref, tm, sem.at[1 - slot])
        _rows_wait(hf_ref, xs_ref, tm, sem.at[1 - slot])

    @pl.when(i == pl.num_programs(0) - 1)
    def _():
        _rows_wait(hf_ref, xs_ref, tm, sem.at[slot])
        _rows_wait(hf_ref, xs_ref, tm, sem.at[slot])


def _moe_dispatch(hf, da, db, nrows, tm):
    t = hf.shape[0]
    idx = pl.BlockSpec((1, 1, tm), lambda i: (i, 0, 0), memory_space=pltpu.SMEM)
    return pl.pallas_call(
        _moe_dispatch_kernel,
        grid=(t // tm,),
        in_specs=[idx, idx, pl.BlockSpec(memory_space=pl.ANY),
                  pl.BlockSpec(memory_space=pl.ANY)],
        out_specs=pl.BlockSpec(memory_space=pl.ANY),
        out_shape=jax.ShapeDtypeStruct((nrows, D_MODEL), F32),
        scratch_shapes=[pltpu.SemaphoreType.DMA((2,))],
        input_output_aliases={3: 0},
        compiler_params=_cp("arbitrary"),
        name="moe_dispatch",
    )(da, db, hf, jnp.zeros((nrows, D_MODEL), F32))


def _moe_ffn_kernel(texp_ref, nval_ref, x_ref, wg_ref, wu_ref, wd_ref, y_ref):
    del texp_ref
    i = pl.program_id(0)

    @pl.when(i < nval_ref[0])
    def _():
        x = x_ref[...].astype(BF16)
        gp = _dot(x, wg_ref[0])
        up = _dot(x, wu_ref[0])
        y_ref[...] = _dot(((gp * _sigmoid(gp)) * up).astype(BF16), wd_ref[0])

    @pl.when(i >= nval_ref[0])
    def _():
        y_ref[...] = jnp.zeros_like(y_ref)


def _moe_ffn(xs, texp, nval, wg, wu, wd):
    nrows = xs.shape[0]
    tile = lambda i, te, nv: (jnp.minimum(i, nv[0] - 1), 0)
    wmap = lambda i, te, nv: (te[i], 0, 0)
    grid_spec = pltpu.PrefetchScalarGridSpec(
        num_scalar_prefetch=2,
        grid=(nrows // MOE_TM,),
        in_specs=[pl.BlockSpec((MOE_TM, D_MODEL), tile),
                  pl.BlockSpec((1, D_MODEL, D_EXPERT), wmap), pl.BlockSpec((1, D_MODEL, D_EXPERT), wmap),
                  pl.BlockSpec((1, D_EXPERT, D_MODEL), wmap)],
        out_specs=pl.BlockSpec((MOE_TM, D_MODEL), lambda i, te, nv: (i, 0)),
    )
    return pl.pallas_call(
        _moe_ffn_kernel,
        grid_spec=grid_spec,
        out_shape=jax.ShapeDtypeStruct((nrows, D_MODEL), F32),
        compiler_params=_cp("arbitrary"),
        name="moe_ffn",
    )(texp, nval, xs, wg, wu, wd)


def _moe_combine_kernel(da_ref, db_ref, dan_ref, dbn_ref, info_ref, x_ref, ys_ref, o_ref, buf_ref, sem):
    tm = x_ref.shape[0]
    i = pl.program_id(0)
    n = pl.num_programs(0)
    slot = i % 2

    def gather(ia_ref, ib_ref, s):
        def issue(r, carry):
            _row_copy(ys_ref, ia_ref[0, 0, r], buf_ref.at[s, 0], r, sem.at[s]).start()
            _row_copy(ys_ref, ib_ref[0, 0, r], buf_ref.at[s, 1], r, sem.at[s]).start()
            return carry

        lax.fori_loop(0, tm, issue, 0, unroll=4)

    @pl.when(i == 0)
    def _():
        gather(da_ref, db_ref, 0)

    @pl.when(i + 1 < n)
    def _():
        gather(dan_ref, dbn_ref, 1 - slot)

    _rows_wait(ys_ref, buf_ref.at[slot, 0], tm, sem.at[slot])
    _rows_wait(ys_ref, buf_ref.at[slot, 1], tm, sem.at[slot])
    info = info_ref[...]
    o_ref[...] = (x_ref[...] + info[:, 2:3] * buf_ref[slot, 0]) + info[:, 3:4] * buf_ref[slot, 1]


def _moe_combine(xn, info, da, db, ys, tm):
    t = xn.shape[0]
    n = t // tm
    idx = pl.BlockSpec((1, 1, tm), lambda i: (i, 0, 0), memory_space=pltpu.SMEM)
    nxt = pl.BlockSpec((1, 1, tm), lambda i: (jnp.minimum(i + 1, n - 1), 0, 0), memory_space=pltpu.SMEM)
    return pl.pallas_call(
        _moe_combine_kernel,
        grid=(n,),
        in_specs=[idx, idx, nxt, nxt, pl.BlockSpec((tm, LANES), lambda i: (i, 0)),
                  pl.BlockSpec((tm, D_MODEL), lambda i: (i, 0)), pl.BlockSpec(memory_space=pl.ANY)],
        out_specs=pl.BlockSpec((tm, D_MODEL), lambda i: (i, 0)),
        out_shape=jax.ShapeDtypeStruct((t, D_MODEL), F32),
        scratch_shapes=[pltpu.VMEM((2, 2, tm, D_MODEL), F32), pltpu.SemaphoreType.DMA((2,))],
        compiler_params=_cp("arbitrary"),
        name="moe_combine",
    )(da, db, da, db, info, xn, ys)


def _moe(hf, gates, sel, xn, wg, wu, wd, tm):
    t = hf.shape[0]
    nrows = -(-(2 * t + N_EXPERTS * (MOE_TM - 1)) // MOE_TM) * MOE_TM
    ntile = nrows // MOE_TM
    rank, cnt = _moe_rank(sel, tm)
    cnt = cnt[0, :N_EXPERTS].astype(jnp.int32)
    padded = (cnt + (MOE_TM - 1)) // MOE_TM * MOE_TM
    end = jnp.cumsum(padded)
    off = jnp.zeros((1, LANES), F32).at[0, :N_EXPERTS].set((end - padded).astype(F32))
    nval = (end[-1] // MOE_TM).astype(jnp.int32)
    tile_start = jnp.arange(ntile, dtype=jnp.int32) * MOE_TM
    texp = jnp.sum((end[None, :] <= tile_start[:, None]).astype(jnp.int32), axis=1)
    texp = jnp.minimum(texp, N_EXPERTS - 1)
    texp = jnp.where(jnp.arange(ntile) < nval, texp, texp[jnp.maximum(nval - 1, 0)])
    info = _moe_dest(sel, gates, rank, off, tm)
    da = info[:, 0].astype(jnp.int32).reshape(t // tm, 1, tm)
    db = info[:, 1].astype(jnp.int32).reshape(t // tm, 1, tm)
    xs = _moe_dispatch(hf, da, db, nrows, tm)
    ys = _moe_ffn(xs, texp, nval.reshape(1), wg, wu, wd)
    return _moe_combine(xn, info, da, db, ys, tm)


def _pad_rows(a, nseq, dec):
    c = a.shape[-1]
    return jnp.pad(a.reshape(nseq, dec, c), ((0, 0), (0, SUB - dec), (0, 0))).reshape(nseq * SUB, c)


def _layer_params(l, ln_mix, w_in, hgrn_norm, fox_bf, fox_qnorm, fox_knorm, gdn_conv, gdn_a_log,
                  gdn_dt_bias, gdn_norm, w_out, ln_ffn, w_group, b_group, w_router, b_router,
                  w_gate, w_up, w_down):
    w = w_in[l]
    gates = jnp.concatenate([w[:, 2560:2568], w[:, 3592:3600]], axis=1)
    wp = jnp.concatenate([w[:, 0:2560], w[:, 2568:3592], gates,
                          jnp.zeros((D_MODEL, LANES - 16), F32)], axis=1).astype(BF16)
    p1 = jnp.zeros((1, LANES), F32).at[0, 0:8].set(fox_bf[l]).at[0, 8:12].set(gdn_dt_bias[l])
    p2 = jnp.zeros((1, LANES), F32).at[0, 8:12].set(gdn_a_log[l])
    wr = jnp.concatenate([w_router[l], w_group[l],
                          jnp.zeros((D_MODEL, LANES - N_GROUPS - N_EXPERTS), F32)], axis=1).astype(BF16)
    br = jnp.zeros((1, LANES), F32).at[0, 0:N_EXPERTS].set(b_router[l])
    br = br.at[0, N_EXPERTS:N_EXPERTS + N_GROUPS].set(b_group[l])
    return dict(
        ln_mix=ln_mix[l][None, :], w=wp, p1=p1, p2=p2,
        qg=jnp.tile(fox_qnorm[l], B_HEADS)[None, :], kg=jnp.tile(fox_knorm[l], B_HEADS)[None, :],
        hgrn_ng=jnp.tile(hgrn_norm[l], A_HEADS)[None, :], gdn_ng=jnp.tile(gdn_norm[l], C_HEADS)[None, :],
        conv=gdn_conv[l], w_out=w_out[l].astype(BF16), ln_ffn=ln_ffn[l][None, :], wr=wr, br=br,
        wg=w_gate[l].astype(BF16), wu=w_up[l].astype(BF16), wd=w_down[l].astype(BF16))


def kernel(x_prompt, x_sample, cache_k, cache_v, cache_logf, page_table, state_hgrn, state_gdn, state_conv,
           ln_mix, w_in, hgrn_lb, hgrn_norm, fox_bf, fox_qnorm, fox_knorm, gdn_conv, gdn_a_log, gdn_dt_bias,
           gdn_norm, w_out, ln_ffn, w_group, b_group, w_router, b_router, w_gate, w_up, w_down):
    nb, seq, _ = x_prompt.shape
    nsq, dec, _ = x_sample.shape
    depth = ln_mix.shape[0]
    n_phys, page = cache_k.shape[1], cache_k.shape[2]
    tp = nb * seq
    ts = nsq * dec
    hs512 = _const(_head_sum_matrix(512))
    ck = jnp.transpose(cache_k, (0, 1, 3, 4, 2)).reshape(depth, n_phys, 512, page)
    cv = jnp.transpose(cache_v, (0, 1, 3, 4, 2)).reshape(depth, n_phys, 512, page)
    clt = jnp.swapaxes(cache_logf, 2, 3)
    tbp = min(256, seq)
    gchunk = min(64, seq)
    seq_blk = 16

    yp = x_prompt.reshape(tp, D_MODEL)
    ys = x_sample.reshape(ts, D_MODEL)
    outs_p, outs_s = [], []
    for l in range(depth):
        P = _layer_params(l, ln_mix, w_in, hgrn_norm, fox_bf, fox_qnorm, fox_knorm, gdn_conv, gdn_a_log,
                          gdn_dt_bias, gdn_norm, w_out, ln_ffn, w_group, b_group, w_router, b_router,
                          w_gate, w_up, w_down)

        a, bq, bk, bv, c, g = _proj(yp, P["ln_mix"], P["w"], hs512, P["qg"], P["kg"], P["p1"], P["p2"],
                                    tm=min(256, tp))
        qa, ka = _foxprep(bq, bk, g, nb, tm=min(256, seq))
        ob = _fox(qa, ka, bv, nb, tq=min(256, seq))
        oa, hst = _hgrn(a, hgrn_lb, P["hgrn_ng"], None, l, nb, seq // tbp, tbp, 1, tbp // SUB, SUB)
        oc, gst = _gdn(c, g, P["conv"], P["gdn_ng"], None, None, nb, seq // tbp, tbp, 1,
                       tbp // gchunk, gchunk)
        xn, hf, gates, sel = _post(yp, oa, ob, oc, P["w_out"], P["ln_ffn"], P["wr"], P["br"],
                                   tm=min(256, tp))
        yp_new = _moe(hf, gates, sel, xn, P["wg"], P["wu"], P["wd"], tm=min(256, tp))
        outs_p.append((bk.reshape(nb, seq, B_HEADS, HD), bv.reshape(nb, seq, B_HEADS, HD),
                       g[:, 0:8].reshape(nb, seq, B_HEADS), hst, gst,
                       c[:, 0:768].reshape(nb, seq, 768)[:, seq - (CONV_W - 1):, :]))
        yp = yp_new

        a, bq, bk, bv, c, g = _proj(ys, P["ln_mix"], P["w"], hs512, P["qg"], P["kg"], P["p1"], P["p2"],
                                    tm=min(256, ts))
        knew = jnp.pad(jnp.swapaxes(bk.reshape(nsq, dec, 512), 1, 2), ((0, 0), (0, 0), (0, page - dec)))
        vnew = jnp.pad(jnp.swapaxes(bv.reshape(nsq, dec, 512), 1, 2), ((0, 0), (0, 0), (0, page - dec)))
        lfn = jnp.pad(jnp.swapaxes(g[:, 0:8].reshape(nsq, dec, B_HEADS), 1, 2),
                      ((0, 0), (0, 0), (0, page - dec)))
        ob = _foxdec(l, page_table, bq.reshape(nsq, dec, 512), knew, vnew, lfn, ck, cv, clt,
                     pp=min(8, page_table.shape[1])).reshape(ts, 512)
        n_outer = nsq // seq_blk
        oa, hst = _hgrn(_pad_rows(a, nsq, dec), hgrn_lb, P["hgrn_ng"], state_hgrn[l], l,
                        n_outer, 1, seq_blk * SUB, seq_blk, 1, dec)
        cinit = jnp.pad(state_conv[l], ((0, 0), (8 - (CONV_W - 1), 0), (0, 0)))
        oc, gst = _gdn(_pad_rows(c, nsq, dec), _pad_rows(g, nsq, dec), P["conv"], P["gdn_ng"], cinit,
                       state_gdn[l], n_outer, 1, seq_blk * SUB, seq_blk, 1, SUB)
        oa = oa.reshape(nsq, SUB, 256)[:, :dec].reshape(ts, 256)
        oc = oc.reshape(nsq, SUB, 256)[:, :dec].reshape(ts, 256)
        xn, hf, gates, sel = _post(ys, oa, ob, oc, P["w_out"], P["ln_ffn"], P["wr"], P["br"],
                                   tm=min(256, ts))
        ys_new = _moe(hf, gates, sel, xn, P["wg"], P["wu"], P["wd"], tm=min(256, ts))
        xpad = jnp.concatenate([state_conv[l], c[:, 0:768].reshape(nsq, dec, 768)], axis=1)
        outs_s.append((bk.reshape(nsq, dec, B_HEADS, HD), bv.reshape(nsq, dec, B_HEADS, HD),
                       g[:, 0:8].reshape(nsq, dec, B_HEADS), hst, gst, xpad[:, -(CONV_W - 1):, :]))
        ys = ys_new

    stack = lambda outs, j: jnp.stack([o[j] for o in outs], axis=0)
    return (yp.reshape(nb, seq, D_MODEL), ys.reshape(nsq, dec, D_MODEL),
            *[stack(outs_p, j) for j in range(6)], *[stack(outs_s, j) for j in range(6)])
```

```python
import functools
import math

import numpy as np
import jax
import jax.numpy as jnp
from jax import lax
from jax.experimental import pallas as pl
from jax.experimental.pallas import tpu as pltpu

F32 = jnp.float32
BF16 = jnp.bfloat16
EPS = 1e-6
NEG = -1e30

D_MODEL = 1024
HD = 64
A_HEADS = 4
B_HEADS = 8
C_HEADS = 4
CONV_W = 4
N_GROUPS = 4
EXPERTS_PER_GROUP = 8
N_EXPERTS = N_GROUPS * EXPERTS_PER_GROUP
D_EXPERT = D_MODEL // 4
SUB = 16
LANES = 128
VMEM_LIMIT = 56 * 1024 * 1024


def _cp(*sem):
    return pltpu.CompilerParams(dimension_semantics=sem, vmem_limit_bytes=VMEM_LIMIT)


def _dot(a, b):
    return jnp.dot(a, b, preferred_element_type=F32)


def _dot_nt(a, b):
    return lax.dot_general(a, b, (((1,), (1,)), ((), ())), preferred_element_type=F32)


def _dot_tn(a, b):
    return lax.dot_general(a, b, (((0,), (0,)), ((), ())), preferred_element_type=F32)


def _split3(x):
    hi = x.astype(BF16)
    r = x - hi.astype(F32)
    mid = r.astype(BF16)
    lo = (r - mid.astype(F32)).astype(BF16)
    return hi, mid, lo


def _exact_left(m, x):
    hi, mid, lo = _split3(x)
    return (_dot(m, hi) + _dot(m, mid)) + _dot(m, lo)


def _exact_right(x, m):
    hi, mid, lo = _split3(x)
    return (_dot(hi, m) + _dot(mid, m)) + _dot(lo, m)


def _idiv(x, n):
    return jnp.right_shift(x, int(math.log2(n)))


def _imod(x, n):
    return jnp.bitwise_and(x, n - 1)


def _sigmoid(x):
    return 1.0 / (1.0 + jnp.exp(-x))


def _softplus(z):
    return jnp.maximum(z, 0.0) + jnp.log(1.0 + jnp.exp(-jnp.abs(z)))


def _head_ssq(z, hs):
    zz = z * z
    hi = zz.astype(BF16)
    lo = (zz - hi.astype(F32)).astype(BF16)
    return _dot(hi, hs) + _dot(lo, hs)


def _const(a, dtype=BF16):
    return jnp.asarray(a, dtype=dtype)


def _head_sum_matrix(width):
    i = np.arange(width)
    return (i[:, None] // HD == i[None, :] // HD).astype(np.float32)


def _chunk_tril(n, c):
    i = np.arange(n)
    return ((i[:, None] // c == i[None, :] // c) & (i[None, :] <= i[:, None])).astype(np.float32)


PROJ_COLS = 3712


def _proj_kernel(x_ref, ln_ref, w_ref, hs_ref, qg_ref, kg_ref, p1_ref, p2_ref,
                 a_ref, bq_ref, bk_ref, bv_ref, c_ref, g_ref):
    x = x_ref[...]
    ms = jnp.mean(x * x, axis=-1, keepdims=True)
    hn = ((x * lax.rsqrt(ms + EPS)) * ln_ref[...]).astype(BF16)
    a_ref[...] = _dot(hn, w_ref[:, 0:1024])
    hs = hs_ref[...]
    q = _dot(hn, w_ref[:, 1024:1536])
    bq_ref[...] = ((q * lax.rsqrt(_head_ssq(q, hs) * (1.0 / HD) + EPS)) * qg_ref[...]) * (HD ** -0.5)
    k = _dot(hn, w_ref[:, 1536:2048])
    bk_ref[...] = (k * lax.rsqrt(_head_ssq(k, hs) * (1.0 / HD) + EPS)) * kg_ref[...]
    bv_ref[...] = _dot(hn, w_ref[:, 2048:2560])
    c_ref[...] = _dot(hn, w_ref[:, 2560:3584])
    gr = _dot(hn, w_ref[:, 3584:3712])
    lane = lax.broadcasted_iota(jnp.int32, gr.shape, 1)
    z = gr + p1_ref[...]
    sp = _softplus(z)
    logsig = jnp.minimum(z, 0.0) - jnp.log(1.0 + jnp.exp(-jnp.abs(z)))
    glog = -jnp.exp(p2_ref[...]) * sp
    beta = _sigmoid(gr)
    g_ref[...] = jnp.where(lane < 8, logsig,
                           jnp.where(lane < 12, glog, jnp.where(lane < 16, beta, 0.0)))


def _proj(x, ln, w, hs512, qg, kg, p1, p2, tm):
    t = x.shape[0]
    row = lambda i: (i, 0)
    fix = lambda i: (0, 0)
    outs = [jax.ShapeDtypeStruct((t, n), F32) for n in (1024, 512, 512, 512, 1024, 128)]
    return pl.pallas_call(
        _proj_kernel,
        grid=(t // tm,),
        in_specs=[pl.BlockSpec((tm, D_MODEL), row), pl.BlockSpec((1, D_MODEL), fix),
                  pl.BlockSpec((D_MODEL, PROJ_COLS), fix), pl.BlockSpec((512, 512), fix),
                  pl.BlockSpec((1, 512), fix), pl.BlockSpec((1, 512), fix),
                  pl.BlockSpec((1, LANES), fix), pl.BlockSpec((1, LANES), fix)],
        out_specs=[pl.BlockSpec((tm, n), row) for n in (1024, 512, 512, 512, 1024, 128)],
        out_shape=outs,
        compiler_params=_cp("arbitrary"),
        name="proj",
    )(x, ln, w, hs512, qg, kg, p1, p2)


def _foxprep_kernel(bq_ref, bk_ref, g_ref, tril_ref, pq_ref, pcq_ref, pck_ref, oq_ref, ok_ref,
                    qa_ref, ka_ref, carry_ref):
    i = pl.program_id(1)

    @pl.when(i == 0)
    def _():
        carry_ref[...] = jnp.zeros_like(carry_ref)

    g = g_ref[...]
    c = _exact_left(tril_ref[...], g) + carry_ref[...]
    carry_ref[...] = c[-1:, :]
    hi, mid, lo = _split3(c)
    pq = pq_ref[...]
    qa = _dot(bq_ref[...].astype(BF16), pq) + oq_ref[...]
    ka = _dot(bk_ref[...].astype(BF16), pq) + ok_ref[...]
    for j, part in enumerate((hi, mid, lo)):
        qa = qa + _dot(part, pcq_ref[j])
        ka = ka - _dot(part, pck_ref[j])
    qa_ref[...] = qa.astype(BF16)
    ka_ref[...] = ka.astype(BF16)


def _foxprep(bq, bk, g, nb, tm):
    t = bq.shape[0]
    nblk = t // nb // tm
    pq = np.zeros((512, 1024), np.float32)
    for h in range(B_HEADS):
        pq[h * HD + np.arange(HD), h * LANES + np.arange(HD)] = 1.0
    pcq = np.zeros((3, LANES, 1024), np.float32)
    pck = np.zeros((3, LANES, 1024), np.float32)
    oq = np.zeros((1, 1024), np.float32)
    ok = np.zeros((1, 1024), np.float32)
    for h in range(B_HEADS):
        for j in range(3):
            pcq[j, h, h * LANES + HD + j] = 1.0
            pck[j, h, h * LANES + HD + 3 + j] = 1.0
            oq[0, h * LANES + HD + 3 + j] = 1.0
            ok[0, h * LANES + HD + j] = 1.0
    row = lambda b, i: (b * nblk + i, 0)
    fix2 = lambda b, i: (0, 0)
    fix3 = lambda b, i: (0, 0, 0)
    return pl.pallas_call(
        _foxprep_kernel,
        grid=(nb, nblk),
        in_specs=[pl.BlockSpec((tm, 512), row), pl.BlockSpec((tm, 512), row),
                  pl.BlockSpec((tm, LANES), row), pl.BlockSpec((tm, tm), fix2),
                  pl.BlockSpec((512, 1024), fix2), pl.BlockSpec((3, LANES, 1024), fix3),
                  pl.BlockSpec((3, LANES, 1024), fix3), pl.BlockSpec((1, 1024), fix2),
                  pl.BlockSpec((1, 1024), fix2)],
        out_specs=[pl.BlockSpec((tm, 1024), row), pl.BlockSpec((tm, 1024), row)],
        out_shape=[jax.ShapeDtypeStruct((t, 1024), BF16)] * 2,
        scratch_shapes=[pltpu.VMEM((1, LANES), F32)],
        compiler_params=_cp("arbitrary", "arbitrary"),
        name="foxprep",
    )(bq, bk, g, _const(_chunk_tril(tm, tm)), _const(pq), _const(pcq), _const(pck),
      _const(oq, F32), _const(ok, F32))


def _fox_kernel(qt_ref, kt_ref, q_ref, k_ref, v_ref, o_ref, m_ref, l_ref, acc_ref, *, tq):
    j = pl.program_id(1)
    qi = qt_ref[j]
    ki = kt_ref[j]

    @pl.when(ki == 0)
    def _():
        m_ref[...] = jnp.full_like(m_ref, NEG)
        l_ref[...] = jnp.zeros_like(l_ref)
        acc_ref[...] = jnp.zeros_like(acc_ref)

    heads = range(B_HEADS)
    lane = lax.broadcasted_iota(jnp.int32, (tq, LANES), 1)
    left = lane < HD

    def update(masked):
        if masked:
            rows = lax.broadcasted_iota(jnp.int32, (tq, tq), 0)
            cols = lax.broadcasted_iota(jnp.int32, (tq, tq), 1)
            keep = cols <= rows
        for pr in range(B_HEADS // 2):
            pair = (2 * pr, 2 * pr + 1)
            ss = [_dot_nt(q_ref[:, h * LANES:(h + 1) * LANES], k_ref[:, h * LANES:(h + 1) * LANES])
                  for h in pair]
            if masked:
                ss = [jnp.where(keep, s, NEG) for s in ss]
            ps, alphas = [], []
            for s, h in zip(ss, pair):
                m_old = m_ref[h]
                m_new = jnp.maximum(m_old, jnp.max(s, axis=1, keepdims=True))
                p = jnp.exp(s - m_new[:, 0:1])
                alpha = jnp.exp(m_old - m_new)
                l_ref[h] = alpha * l_ref[h] + jnp.sum(p, axis=1, keepdims=True)
                m_ref[h] = m_new
                ps.append(p.astype(BF16))
                alphas.append(alpha)
            vp = v_ref[:, pr * LANES:(pr + 1) * LANES]
            upd = _dot(ps[0], jnp.where(left, vp, 0.0).astype(BF16))
            upd = upd + _dot(ps[1], jnp.where(left, 0.0, vp).astype(BF16))
            a = jnp.where(left, alphas[0], alphas[1])
            acc_ref[:, pr * LANES:(pr + 1) * LANES] = a * acc_ref[:, pr * LANES:(pr + 1) * LANES] + upd

    @pl.when(ki < qi)
    def _():
        update(False)

    @pl.when(ki == qi)
    def _():
        update(True)
        for pr in range(B_HEADS // 2):
            l = jnp.where(left, l_ref[2 * pr], l_ref[2 * pr + 1])
            o_ref[:, pr * LANES:(pr + 1) * LANES] = acc_ref[:, pr * LANES:(pr + 1) * LANES] / l


def _fox(qa, ka, bv, nb, tq):
    t = qa.shape[0]
    nblk = t // nb // tq
    qi_tab = np.array([q for q in range(nblk) for _ in range(q + 1)], np.int32)
    ki_tab = np.array([k for q in range(nblk) for k in range(q + 1)], np.int32)
    grid_spec = pltpu.PrefetchScalarGridSpec(
        num_scalar_prefetch=2,
        grid=(nb, len(qi_tab)),
        in_specs=[pl.BlockSpec((tq, 1024), lambda b, j, qt, kt: (b * nblk + qt[j], 0)),
                  pl.BlockSpec((tq, 1024), lambda b, j, qt, kt: (b * nblk + kt[j], 0)),
                  pl.BlockSpec((tq, 512), lambda b, j, qt, kt: (b * nblk + kt[j], 0))],
        out_specs=pl.BlockSpec((tq, 512), lambda b, j, qt, kt: (b * nblk + qt[j], 0)),
        scratch_shapes=[pltpu.VMEM((B_HEADS, tq, LANES), F32), pltpu.VMEM((B_HEADS, tq, LANES), F32),
                        pltpu.VMEM((tq, 512), F32)],
    )
    return pl.pallas_call(
        functools.partial(_fox_kernel, tq=tq),
        grid_spec=grid_spec,
        out_shape=jax.ShapeDtypeStruct((t, 512), F32),
        compiler_params=_cp("arbitrary", "arbitrary"),
        name="fox",
    )(jnp.asarray(qi_tab), jnp.asarray(ki_tab), qa, ka, bv)


def _foxdec_kernel(pt_ref, q_ref, kn_ref, vn_ref, lfn_ref, cum_ref, wide_ref, *rest, pp, dec):
    kp = rest[0:pp]
    vp = rest[pp:2 * pp]
    lp = rest[2 * pp:3 * pp]
    o_ref = rest[3 * pp]
    qr_ref, m_ref, l_ref, r_ref, acc_ref = rest[3 * pp + 1:]
    step = pl.program_id(1)
    nrow = dec * B_HEADS
    rowi = lax.broadcasted_iota(jnp.int32, (nrow, 512), 0)
    coli = lax.broadcasted_iota(jnp.int32, (nrow, 512), 1)
    bd = _idiv(coli, HD) == _imod(rowi, B_HEADS)

    @pl.when(step == 0)
    def _():
        q = q_ref[0]
        qrows = jnp.concatenate(
            [jnp.broadcast_to(q[t:t + 1, :], (B_HEADS, 512)) for t in range(dec)], axis=0)
        qr_ref[...] = jnp.where(bd, qrows, 0.0).astype(BF16)
        m_ref[...] = jnp.full_like(m_ref, NEG)
        l_ref[...] = jnp.zeros_like(l_ref)
        r_ref[...] = jnp.zeros_like(r_ref)
        acc_ref[...] = jnp.zeros_like(acc_ref)

    def pages(krefs, vrefs, lrefs, causal):
        n = len(krefs)
        qr = qr_ref[...]
        ss = [_dot(qr, kr().astype(BF16)) for kr in krefs]
        cums = [_exact_right(lr(), cum_ref[...]) for lr in lrefs]
        r = r_ref[...]
        logits = []
        for u in range(n):
            w = jnp.concatenate([cums[u][:, 0:LANES]] * dec, axis=0)
            logit = ss[u] - (r + w)
            if causal:
                key = lax.broadcasted_iota(jnp.int32, (nrow, LANES), 1)
                trow = _idiv(lax.broadcasted_iota(jnp.int32, (nrow, LANES), 0), B_HEADS)
                logit = jnp.where(key <= trow, logit, NEG)
            logits.append(logit)
            r = r + jnp.concatenate([cums[u][:, LANES:2 * LANES]] * dec, axis=0)
        r_ref[...] = r
        m_old = m_ref[...]
        m_new = m_old
        for lg in logits:
            m_new = jnp.maximum(m_new, jnp.max(lg, axis=1, keepdims=True))
        ps = [jnp.exp(lg - m_new) for lg in logits]
        alpha = jnp.exp(m_old - m_new)
        lsum = jnp.sum(ps[0], axis=1, keepdims=True)
        for p in ps[1:]:
            lsum = lsum + jnp.sum(p, axis=1, keepdims=True)
        l_ref[...] = alpha * l_ref[...] + lsum
        m_ref[...] = m_new
        pv = _dot_nt(ps[0].astype(BF16), vrefs[0]().astype(BF16))
        for u in range(1, n):
            pv = pv + _dot_nt(ps[u].astype(BF16), vrefs[u]().astype(BF16))
        acc_ref[...] = alpha[:, 0:1] * acc_ref[...] + pv

    pages([lambda u=u: kp[u][...] for u in range(pp)], [lambda u=u: vp[u][...] for u in range(pp)],
          [lambda u=u: lp[u][...] for u in range(pp)], False)

    @pl.when(step == pl.num_programs(1) - 1)
    def _():
        wide = wide_ref[...]
        knew = _dot(kn_ref[0].astype(BF16), wide)
        vnew = _dot(vn_ref[0].astype(BF16), wide)
        lfnew = _exact_right(lfn_ref[0], wide)
        pages([lambda: knew], [lambda: vnew], [lambda: lfnew], True)
        res = jnp.where(bd, acc_ref[...] / l_ref[:, 0:1], 0.0)
        for t in range(dec):
            o_ref[0, t:t + 1, :] = jnp.sum(res[t * B_HEADS:(t + 1) * B_HEADS], axis=0, keepdims=True)


def _foxdec(layer, page_table, q, knew, vnew, lfn_t, cache_k, cache_v, cache_lft, pp):
    ns, dec, _ = q.shape
    n_pages = page_table.shape[1]
    page = cache_k.shape[3]
    nsteps = n_pages // pp
    nrow = dec * B_HEADS
    cum = np.concatenate([np.triu(np.ones((page, page), np.float32)), np.ones((page, page), np.float32)], axis=1)
    seq3 = lambda s, p, pt: (s, 0, 0)
    fix2 = lambda s, p, pt: (0, 0)

    def cache_map(u):
        return lambda s, p, pt: (layer, pt[s, p * pp + u], 0, 0)

    wide = np.eye(SUB, page, dtype=np.float32)
    in_specs = [pl.BlockSpec((1, dec, 512), seq3), pl.BlockSpec((1, 512, SUB), seq3),
                pl.BlockSpec((1, 512, SUB), seq3), pl.BlockSpec((1, B_HEADS, SUB), seq3),
                pl.BlockSpec((page, 2 * page), fix2), pl.BlockSpec((SUB, page), fix2)]
    in_specs += [pl.BlockSpec((None, None, 512, page), cache_map(u)) for u in range(pp)]
    in_specs += [pl.BlockSpec((None, None, 512, page), cache_map(u)) for u in range(pp)]
    in_specs += [pl.BlockSpec((None, None, B_HEADS, page), cache_map(u)) for u in range(pp)]
    grid_spec = pltpu.PrefetchScalarGridSpec(
        num_scalar_prefetch=1,
        grid=(ns, nsteps),
        in_specs=in_specs,
        out_specs=pl.BlockSpec((1, dec, 512), seq3),
        scratch_shapes=[pltpu.VMEM((nrow, 512), BF16), pltpu.VMEM((nrow, LANES), F32),
                        pltpu.VMEM((nrow, LANES), F32), pltpu.VMEM((nrow, LANES), F32),
                        pltpu.VMEM((nrow, 512), F32)],
    )
    return pl.pallas_call(
        functools.partial(_foxdec_kernel, pp=pp, dec=dec),
        grid_spec=grid_spec,
        out_shape=jax.ShapeDtypeStruct((ns, dec, 512), F32),
        compiler_params=_cp("arbitrary", "arbitrary"),
        name="foxdec",
    )(page_table, q, knew, vnew, lfn_t, _const(cum), _const(wide),
      *([cache_k] * pp), *([cache_v] * pp), *([cache_lft] * pp))


def _hgrn_kernel(*refs, layer, ns, cps, valid, has_init):
    (q_ref, f_ref, v_ref, gate_ref, lbraw_ref, ng_ref, tril_ref, rs_ref, tile_ref, hs_ref) = refs[:10]
    pos = 10
    s0_ref = None
    if has_init:
        s0_ref = refs[pos]
        pos += 1
    o_ref, sout_ref, st_ref, oin_ref, oint_ref, tmp_ref = refs[pos:pos + 6]
    i = pl.program_id(1)
    tb = q_ref.shape[0]
    nchunk = tb // SUB
    width = A_HEADS * HD

    lbraw = lbraw_ref[...]
    e = jnp.exp(lbraw - jnp.max(lbraw, axis=0, keepdims=True))
    prob = e / jnp.sum(e, axis=0, keepdims=True)
    lb = jnp.zeros((1, width), F32)
    for d in range(1, layer + 1):
        lb = lb + prob[d:d + 1, :]

    fr = f_ref[...]
    qr = q_ref[...]
    vr = v_ref[...]
    logf = jnp.log(lb + (1.0 - lb) * _sigmoid(fr))
    key = (1.0 - lb) * _sigmoid(-fr)
    if valid < SUB:
        rowv = _imod(lax.broadcasted_iota(jnp.int32, (tb, width), 0), SUB) < valid
        logf = jnp.where(rowv, logf, 0.0)
        key = jnp.where(rowv, key, 0.0)
    q = (qr * _sigmoid(qr)) * (HD ** -0.5)
    g = _exact_left(tril_ref[...], logf)
    g3 = g.reshape(nchunk, SUB, width)
    q3 = q.reshape(nchunk, SUB, width)
    k3 = key.reshape(nchunk, SUB, width)

    att = jnp.zeros((tb, LANES), F32)
    for s in range(SUB):
        d = jnp.minimum(g3 - g3[:, s:s + 1, :], 0.0)
        p = (q3 * jnp.exp(d)) * k3[:, s:s + 1, :]
        att = att + _dot(p.reshape(tb, width).astype(BF16), rs_ref[s])
    attb = att.astype(BF16)

    rows = lax.broadcasted_iota(jnp.int32, (tb, tb), 0)
    cols = lax.broadcasted_iota(jnp.int32, (tb, tb), 1)
    causal = (_idiv(rows, SUB) == _idiv(cols, SUB)) & (cols <= rows)
    lane = lax.broadcasted_iota(jnp.int32, (tb, LANES), 1)
    for pr in range(2):
        vpair = vr[:, pr * LANES:(pr + 1) * LANES]
        acc = None
        for hh in range(2):
            h = 2 * pr + hh
            full = _dot(attb, tile_ref[h])
            full = jnp.where(causal, full, 0.0).astype(BF16)
            vm = jnp.where((lane < HD) == (hh == 0), vpair, 0.0).astype(BF16)
            term = _dot(full, vm)
            acc = term if acc is None else acc + term
        oin_ref[:, pr * LANES:(pr + 1) * LANES] = acc

    gend3 = g3[:, SUB - 1:SUB, :]
    kd = (k3 * jnp.exp(gend3 - g3)).reshape(tb, width).astype(BF16)
    eg = jnp.exp(g)
    qg = (q * eg).astype(BF16)
    r128 = lax.broadcasted_iota(jnp.int32, (LANES, LANES), 0)
    c128 = lax.broadcasted_iota(jnp.int32, (LANES, LANES), 1)
    bdm = _idiv(r128, HD) == _idiv(c128, HD)
    vb = vr.astype(BF16)
    last = i == pl.num_programs(1) - 1
    units = [(sq, pr) for sq in range(ns) for pr in range(2)]
    rowsl = lambda sq, c: slice((sq * cps + c) * SUB, (sq * cps + c + 1) * SUB)
    lanesl = lambda pr: slice(pr * LANES, (pr + 1) * LANES)
    incs = {}
    for sq, pr in units:
        for c in range(cps):
            u = _dot_tn(vb[rowsl(sq, c), lanesl(pr)], kd[rowsl(sq, c), lanesl(pr)])
            incs[sq, pr, c] = jnp.where(bdm, u, 0.0)
    if has_init:
        tmp_ref[...] = jnp.zeros_like(tmp_ref)
        for sq, pr in units:
            tmp_ref[2 * sq + pr, 0:HD, 0:HD] = s0_ref[sq, 2 * pr]
            tmp_ref[2 * sq + pr, HD:LANES, HD:LANES] = s0_ref[sq, 2 * pr + 1]
    seen = {}
    final = {}
    for sq, pr in units:
        if has_init:
            st = tmp_ref[2 * sq + pr].T
        else:
            st = jnp.where(i == 0, 0.0, st_ref[pr])
        for c in range(cps):
            seen[sq, pr, c] = st.astype(BF16)
            r_end = (sq * cps + c + 1) * SUB
            st = st * eg[r_end - 1:r_end, lanesl(pr)] + incs[sq, pr, c]
        if not has_init:
            st_ref[pr] = st
        final[sq, pr] = st
    for sq, pr in units:
        for c in range(cps):
            oint_ref[rowsl(sq, c), lanesl(pr)] = _dot_nt(qg[rowsl(sq, c), lanesl(pr)], seen[sq, pr, c])

    @pl.when(last)
    def _():
        for sq, pr in units:
            tmp_ref[2 * sq + pr] = final[sq, pr].T
        for sq, pr in units:
            sout_ref[sq, 2 * pr] = tmp_ref[2 * sq + pr, 0:HD, 0:HD]
            sout_ref[sq, 2 * pr + 1] = tmp_ref[2 * sq + pr, HD:LANES, HD:LANES]

    o = oin_ref[...] + oint_ref[...]
    gr = gate_ref[...]
    o = (o * lax.rsqrt(_head_ssq(o, hs_ref[...]) * (1.0 / HD) + EPS)) * ng_ref[...]
    o_ref[...] = o * (gr * _sigmoid(gr))


def _hgrn(a, lbraw, ng, s0, layer, n_outer, n_inner, tb, ns, cps, valid):
    rows = a.shape[0]
    width = A_HEADS * HD
    has_init = s0 is not None
    nseq = n_outer * ns
    rs = np.zeros((SUB, width, LANES), np.float32)
    tile = np.zeros((A_HEADS, LANES, tb), np.float32)
    for s in range(SUB):
        for h in range(A_HEADS):
            rs[s, h * HD + np.arange(HD), h * SUB + s] = 1.0
            tile[h, h * SUB + s, np.arange(tb // SUB) * SUB + s] = 1.0

    def col(j):
        return lambda o, i: (o * n_inner + i, j)

    fix2 = lambda o, i: (0, 0)
    fix3 = lambda o, i: (0, 0, 0)
    in_specs = [pl.BlockSpec((tb, width), col(j)) for j in range(4)]
    in_specs += [pl.BlockSpec(lbraw.shape, fix2), pl.BlockSpec((1, width), fix2),
                 pl.BlockSpec((tb, tb), fix2), pl.BlockSpec((SUB, width, LANES), fix3),
                 pl.BlockSpec((A_HEADS, LANES, tb), fix3), pl.BlockSpec((width, width), fix2)]
    args = [a, a, a, a, lbraw, ng, _const(_chunk_tril(tb, SUB)), _const(rs), _const(tile),
            _const(_head_sum_matrix(width))]
    if has_init:
        in_specs.append(pl.BlockSpec((ns, A_HEADS, HD, HD), lambda o, i: (o, 0, 0, 0)))
        args.append(s0)
    return pl.pallas_call(
        functools.partial(_hgrn_kernel, layer=layer, ns=ns, cps=cps, valid=valid, has_init=has_init),
        grid=(n_outer, n_inner),
        in_specs=in_specs,
        out_specs=[pl.BlockSpec((tb, width), lambda o, i: (o * n_inner + i, 0)),
                   pl.BlockSpec((ns, A_HEADS, HD, HD), lambda o, i: (o, 0, 0, 0))],
        out_shape=[jax.ShapeDtypeStruct((rows, width), F32),
                   jax.ShapeDtypeStruct((nseq, A_HEADS, HD, HD), F32)],
        scratch_shapes=[pltpu.VMEM((2, LANES, LANES), F32), pltpu.VMEM((tb, width), F32),
                        pltpu.VMEM((tb, width), F32), pltpu.VMEM((2 * ns, LANES, LANES), F32)],
        compiler_params=_cp("arbitrary", "arbitrary"),
        name="hgrn",
    )(*args)


def _gdn_kernel(*refs, ns, cps, chunk, has_init):
    (x_ref, z_ref, g_ref, cw_ref, ng_ref, tril_ref, hs_ref, eg_ref, eb_ref, eg2_ref, eb2_ref) = refs[:11]
    pos = 11
    cinit_ref = s0_ref = None
    if has_init:
        cinit_ref, s0_ref = refs[pos:pos + 2]
        pos += 2
    o_ref, sout_ref, st_ref, cv_ref, oacc_ref, vn_ref, tmp_ref = refs[pos:pos + 7]
    i = pl.program_id(1)
    tb = x_ref.shape[0]
    width = C_HEADS * HD
    rows_seq = cps * chunk
    last = i == pl.num_programs(1) - 1

    x = x_ref[...]
    cw = cw_ref[...]
    acts = []
    for sq in range(ns):
        xs = x[sq * rows_seq:(sq + 1) * rows_seq]
        if has_init:
            prev = cinit_ref[sq]
        else:
            prev = jnp.where(i == 0, 0.0, cv_ref[...])
        xc = jnp.concatenate([prev, xs], axis=0)
        conv = xc * cw[CONV_W - 1:CONV_W, :]
        for j in range(1, CONV_W):
            conv = conv + pltpu.roll(xc, j, 0) * cw[CONV_W - 1 - j:CONV_W - j, :]
        acts.append(conv[8:])
        if not has_init:
            cv_ref[...] = xs[rows_seq - 8:]
    conv = acts[0] if ns == 1 else jnp.concatenate(acts, axis=0)
    act = conv * _sigmoid(conv)
    hs = hs_ref[...]
    q = act[:, 0:width]
    k = act[:, width:2 * width]
    v = act[:, 2 * width:3 * width]
    q = (q * lax.rsqrt(_head_ssq(q, hs) + EPS)) * (HD ** -0.5)
    k = k * lax.rsqrt(_head_ssq(k, hs) + EPS)

    gt = g_ref[...]
    gc = _exact_left(tril_ref[...], gt)
    gexp = _exact_right(gc, eg_ref[...])
    bexp = _exact_right(gt, eb_ref[...])
    gcol = _exact_right(gc, eg2_ref[...])
    bcol = _exact_right(gt, eb2_ref[...])
    nchunk = tb // chunk
    gexp3 = gexp.reshape(nchunk, chunk, width)
    gend3 = gexp3[:, chunk - 1:chunk, :]
    eg = jnp.exp(gexp)
    rhs = jnp.concatenate([v * bexp, (k * bexp) * eg], axis=1).astype(BF16)
    qg = (q * eg).astype(BF16)
    kd = (k.reshape(nchunk, chunk, width) * jnp.exp(gend3 - gexp3)).reshape(tb, width).astype(BF16)
    kb = k.astype(BF16)
    lane = lax.broadcasted_iota(jnp.int32, (tb, width), 1)
    kmask = [jnp.where(_idiv(lane, HD) == h, k, 0.0).astype(BF16) for h in range(C_HEADS)]
    qmask = [jnp.where(_idiv(lane, HD) == h, q, 0.0).astype(BF16) for h in range(C_HEADS)]

    rc = lax.broadcasted_iota(jnp.int32, (tb, tb), 0)
    cc = lax.broadcasted_iota(jnp.int32, (tb, tb), 1)
    same = _idiv(rc, chunk) == _idiv(cc, chunk)
    eye = rc == cc
    incl = same & (cc <= rc)
    strict = same & (cc < rc)
    hl = _idiv(lane, HD)
    r256 = lax.broadcasted_iota(jnp.int32, (width, width), 0)
    c256 = lax.broadcasted_iota(jnp.int32, (width, width), 1)
    bdm = _idiv(r256, HD) == _idiv(c256, HD)
    nstage = int(math.log2(chunk))
    heads = range(C_HEADS)
    reps = tb // LANES
    gct = gc.T
    decay, xm, tm, qkd = [], [], [], []
    for h in heads:
        g_t = jnp.concatenate([gcol[:, h * LANES:(h + 1) * LANES]] * reps, axis=1)
        decay.append(jnp.exp(jnp.minimum(g_t - gct[8 + h:9 + h, :], 0.0)))
    kk = [_dot_nt(kmask[h], kb) for h in heads]
    qk = [_dot_nt(qmask[h], kb) for h in heads]
    for h in heads:
        b_t = jnp.concatenate([bcol[:, h * LANES:(h + 1) * LANES]] * reps, axis=1)
        x0 = -jnp.where(strict, (b_t * kk[h]) * decay[h], 0.0)
        xm.append(x0)
        tm.append(jnp.where(eye, 1.0, 0.0) + x0)
        qkd.append(jnp.where(incl, qk[h] * decay[h], 0.0).astype(BF16))
    for _ in range(nstage - 1):
        xb = [x.astype(BF16) for x in xm]
        xm = [_dot(xb[h], xb[h]) for h in heads]
        tm = [tm[h] + _dot(tm[h].astype(BF16), xm[h].astype(BF16)) for h in heads]
    sol = [_dot(tm[h].astype(BF16), rhs) for h in heads]
    u = jnp.zeros((tb, width), F32)
    w = jnp.zeros((tb, width), F32)
    for h in heads:
        u = u + jnp.where(hl == h, sol[h][:, 0:width], 0.0)
        w = w + jnp.where(hl == h, sol[h][:, width:2 * width], 0.0)
    wb = w.astype(BF16)

    if has_init:
        tmp_ref[...] = jnp.zeros_like(tmp_ref)
        for sq in range(ns):
            for h in heads:
                tmp_ref[sq, h * HD:(h + 1) * HD, h * HD:(h + 1) * HD] = s0_ref[sq, h]
        sts = [tmp_ref[sq] for sq in range(ns)]
    else:
        sts = [jnp.where(i == 0, 0.0, st_ref[...])]
    rowsl = lambda sq, c: slice((sq * cps + c) * chunk, (sq * cps + c + 1) * chunk)
    for c in range(cps):
        stb = [st.astype(BF16) for st in sts]
        ws = [_dot(wb[rowsl(sq, c)], stb[sq]) for sq in range(ns)]
        for sq in range(ns):
            oacc_ref[rowsl(sq, c), :] = _dot(qg[rowsl(sq, c)], stb[sq])
        vns = [(u[rowsl(sq, c)] - ws[sq]).astype(BF16) for sq in range(ns)]
        for sq in range(ns):
            vn_ref[rowsl(sq, c), :] = vns[sq]
        incs = [jnp.where(bdm, _dot_tn(kd[rowsl(sq, c)], vns[sq]), 0.0) for sq in range(ns)]
        sts = [sts[sq] * eg[(sq * cps + c + 1) * chunk - 1:(sq * cps + c + 1) * chunk, :] + incs[sq]
               for sq in range(ns)]
    if not has_init:
        st_ref[...] = sts[0]

    @pl.when(last)
    def _():
        for sq in range(ns):
            tmp_ref[sq] = sts[sq]
        for sq in range(ns):
            for h in heads:
                sout_ref[sq, h] = tmp_ref[sq, h * HD:(h + 1) * HD, h * HD:(h + 1) * HD]

    vnb = vn_ref[...]
    o = oacc_ref[...]
    for h in heads:
        o = o + jnp.where(hl == h, _dot(qkd[h], vnb), 0.0)
    z = z_ref[...]
    o = (o * lax.rsqrt(_head_ssq(o, hs) * (1.0 / HD) + EPS)) * ng_ref[...]
    o_ref[...] = o * (z * _sigmoid(z))


def _gdn(cfull, g, cw, ng, cinit, s0, n_outer, n_inner, tb, ns, cps, chunk):
    rows = cfull.shape[0]
    width = C_HEADS * HD
    has_init = s0 is not None
    nseq = n_outer * ns
    eg = np.zeros((LANES, width), np.float32)
    eb = np.zeros((LANES, width), np.float32)
    eg2 = np.zeros((LANES, C_HEADS * LANES), np.float32)
    eb2 = np.zeros((LANES, C_HEADS * LANES), np.float32)
    for h in range(C_HEADS):
        eg[8 + h, h * HD:(h + 1) * HD] = 1.0
        eb[12 + h, h * HD:(h + 1) * HD] = 1.0
        eg2[8 + h, h * LANES:(h + 1) * LANES] = 1.0
        eb2[12 + h, h * LANES:(h + 1) * LANES] = 1.0
    rowm = lambda o, i: (o * n_inner + i, 0)
    fix2 = lambda o, i: (0, 0)
    in_specs = [pl.BlockSpec((tb, 3 * width), rowm),
                pl.BlockSpec((tb, width), lambda o, i: (o * n_inner + i, 3)),
                pl.BlockSpec((tb, LANES), rowm), pl.BlockSpec((CONV_W, 3 * width), fix2),
                pl.BlockSpec((1, width), fix2), pl.BlockSpec((tb, tb), fix2),
                pl.BlockSpec((width, width), fix2), pl.BlockSpec((LANES, width), fix2),
                pl.BlockSpec((LANES, width), fix2), pl.BlockSpec((LANES, C_HEADS * LANES), fix2),
                pl.BlockSpec((LANES, C_HEADS * LANES), fix2)]
    args = [cfull, cfull, g, cw, ng, _const(_chunk_tril(tb, chunk)), _const(_head_sum_matrix(width)),
            _const(eg), _const(eb), _const(eg2), _const(eb2)]
    if has_init:
        in_specs += [pl.BlockSpec((ns, 8, 3 * width), lambda o, i: (o, 0, 0)),
                     pl.BlockSpec((ns, C_HEADS, HD, HD), lambda o, i: (o, 0, 0, 0))]
        args += [cinit, s0]
    return pl.pallas_call(
        functools.partial(_gdn_kernel, ns=ns, cps=cps, chunk=chunk, has_init=has_init),
        grid=(n_outer, n_inner),
        in_specs=in_specs,
        out_specs=[pl.BlockSpec((tb, width), rowm),
                   pl.BlockSpec((ns, C_HEADS, HD, HD), lambda o, i: (o, 0, 0, 0))],
        out_shape=[jax.ShapeDtypeStruct((rows, width), F32),
                   jax.ShapeDtypeStruct((nseq, C_HEADS, HD, HD), F32)],
        scratch_shapes=[pltpu.VMEM((width, width), F32), pltpu.VMEM((8, 3 * width), F32),
                        pltpu.VMEM((tb, width), F32), pltpu.VMEM((tb, width), BF16),
                        pltpu.VMEM((ns, width, width), F32)],
        compiler_params=_cp("arbitrary", "arbitrary"),
        name="gdn",
    )(*args)


def _post_kernel(x_ref, oa_ref, ob_ref, oc_ref, wo_ref, ln_ref, wr_ref, br_ref,
                 xn_ref, hf_ref, gate_ref, sel_ref):
    x = x_ref[...]
    mix = _dot(oa_ref[...].astype(BF16), wo_ref[0:256, :])
    mix = mix + _dot(ob_ref[...].astype(BF16), wo_ref[256:768, :])
    mix = mix + _dot(oc_ref[...].astype(BF16), wo_ref[768:1024, :])
    xn = x + mix
    xn_ref[...] = xn
    ms = jnp.mean(xn * xn, axis=-1, keepdims=True)
    hf = (xn * lax.rsqrt(ms + EPS)) * ln_ref[...]
    hf_ref[...] = hf
    logits = _dot(hf.astype(BF16), wr_ref[...]) + br_ref[...]
    lane = lax.broadcasted_iota(jnp.int32, logits.shape, 1)
    big = jnp.int32(1 << 20)
    isg = (lane >= N_EXPERTS) & (lane < N_EXPERTS + N_GROUPS)
    gl = jnp.where(isg, logits, NEG)
    gm = jnp.max(gl, axis=1, keepdims=True)
    gidx = jnp.min(jnp.where(isg & (gl == gm), lane, big), axis=1, keepdims=True) - N_EXPERTS
    top_gp = 1.0 / jnp.sum(jnp.where(isg, jnp.exp(gl - gm), 0.0), axis=1, keepdims=True)
    ing = (lane < N_EXPERTS) & (_idiv(lane, EXPERTS_PER_GROUP) == gidx)
    el = jnp.where(ing, logits, NEG)
    em = jnp.max(el, axis=1, keepdims=True)
    ee = jnp.where(ing, jnp.exp(el - em), 0.0)
    prob = ee / jnp.sum(ee, axis=1, keepdims=True)
    p1 = jnp.max(prob, axis=1, keepdims=True)
    i1 = jnp.min(jnp.where(ing & (prob == p1), lane, big), axis=1, keepdims=True)
    rest = jnp.where(ing & (lane != i1), prob, -1.0)
    p2 = jnp.max(rest, axis=1, keepdims=True)
    i2 = jnp.min(jnp.where(ing & (lane != i1) & (rest == p2), lane, big), axis=1, keepdims=True)
    den = p1 + p2
    gate_ref[...] = jnp.where(lane == i1, (top_gp * p1) / den,
                              jnp.where(lane == i2, (top_gp * p2) / den, 0.0))
    sel_ref[...] = jnp.where((lane == i1) | (lane == i2), 1.0, 0.0)


def _post(x, oa, ob, oc, wo, ln, wr, br, tm):
    t = x.shape[0]
    row = lambda i: (i, 0)
    fix = lambda i: (0, 0)
    return pl.pallas_call(
        _post_kernel,
        grid=(t // tm,),
        in_specs=[pl.BlockSpec((tm, D_MODEL), row), pl.BlockSpec((tm, 256), row),
                  pl.BlockSpec((tm, 512), row), pl.BlockSpec((tm, 256), row),
                  pl.BlockSpec((D_MODEL, D_MODEL), fix), pl.BlockSpec((1, D_MODEL), fix),
                  pl.BlockSpec((D_MODEL, LANES), fix), pl.BlockSpec((1, LANES), fix)],
        out_specs=[pl.BlockSpec((tm, D_MODEL), row), pl.BlockSpec((tm, D_MODEL), row),
                   pl.BlockSpec((tm, LANES), row), pl.BlockSpec((tm, LANES), row)],
        out_shape=[jax.ShapeDtypeStruct((t, D_MODEL), F32), jax.ShapeDtypeStruct((t, D_MODEL), F32),
                   jax.ShapeDtypeStruct((t, LANES), F32), jax.ShapeDtypeStruct((t, LANES), F32)],
        compiler_params=_cp("arbitrary"),
        name="post",
    )(x, oa, ob, oc, wo, ln, wr, br)


MOE_TM = 256


def _moe_rank_kernel(sel_ref, tril_ref, rank_ref, cnt_ref, carry_ref):
    @pl.when(pl.program_id(0) == 0)
    def _():
        carry_ref[...] = jnp.zeros_like(carry_ref)

    sel = sel_ref[...]
    rank_ref[...] = _dot(tril_ref[...], sel.astype(BF16)) + carry_ref[...]
    carry_ref[...] = carry_ref[...] + jnp.sum(sel, axis=0, keepdims=True)
    cnt_ref[...] = carry_ref[...]


def _moe_rank(sel, tm):
    t = sel.shape[0]
    strict = np.tril(np.ones((tm, tm), np.float32), -1)
    return pl.pallas_call(
        _moe_rank_kernel,
        grid=(t // tm,),
        in_specs=[pl.BlockSpec((tm, LANES), lambda i: (i, 0)), pl.BlockSpec((tm, tm), lambda i: (0, 0))],
        out_specs=[pl.BlockSpec((tm, LANES), lambda i: (i, 0)), pl.BlockSpec((1, LANES), lambda i: (0, 0))],
        out_shape=[jax.ShapeDtypeStruct((t, LANES), F32), jax.ShapeDtypeStruct((1, LANES), F32)],
        scratch_shapes=[pltpu.VMEM((1, LANES), F32)],
        compiler_params=_cp("arbitrary"),
        name="moe_rank",
    )(sel, _const(strict))


def _moe_dest_kernel(sel_ref, gate_ref, rank_ref, off_ref, info_ref):
    on = sel_ref[...] > 0.0
    gates = gate_ref[...]
    dest = off_ref[...] + rank_ref[...]
    lane = lax.broadcasted_iota(jnp.int32, gates.shape, 1)
    la = jnp.min(jnp.where(on, lane, LANES), axis=1, keepdims=True)
    lb = jnp.max(jnp.where(on, lane, -1), axis=1, keepdims=True)
    pick = lambda v, l: jnp.sum(jnp.where(lane == l, v, 0.0), axis=1, keepdims=True)
    info_ref[...] = jnp.where(lane == 0, pick(dest, la),
                              jnp.where(lane == 1, pick(dest, lb),
                                        jnp.where(lane == 2, pick(gates, la),
                                                  jnp.where(lane == 3, pick(gates, lb), 0.0))))


def _moe_dest(sel, gates, rank, off, tm):
    t = sel.shape[0]
    row = lambda i: (i, 0)
    return pl.pallas_call(
        _moe_dest_kernel,
        grid=(t // tm,),
        in_specs=[pl.BlockSpec((tm, LANES), row)] * 3 + [pl.BlockSpec((1, LANES), lambda i: (0, 0))],
        out_specs=pl.BlockSpec((tm, LANES), row),
        out_shape=jax.ShapeDtypeStruct((t, LANES), F32),
        compiler_params=_cp("arbitrary"),
        name="moe_dest",
    )(sel, gates, rank, off)


def _row_copy(src_ref, src_row, dst_ref, dst_row, sem):
    return pltpu.make_async_copy(src_ref.at[pl.ds(src_row, 1), :], dst_ref.at[pl.ds(dst_row, 1), :], sem)


def _rows_wait(src_ref, dst_ref, nrows, sem):
    pltpu.make_async_copy(src_ref.at[pl.ds(0, nrows), :], dst_ref.at[pl.ds(0, nrows), :], sem).wait()


def _moe_dispatch_kernel(da_ref, db_ref, hf_ref, xs_in_ref, xs_ref, sem):
    del xs_in_ref
    tm = hf_ref.shape[0]

    def issue(r, carry):
        _row_copy(hf_ref, r, xs_ref, da_ref[0, 0, r], sem.at[0]).start()
        _row_copy(hf_ref, r, xs_ref, db_ref[0, 0, r], sem.at[0]).start()
        return carry

    lax.fori_loop(0, tm, issue, 0, unroll=4)
    _rows_wait(hf_ref, xs_ref, tm, sem.at[0])
    _rows_wait(hf_ref, xs_ref, tm, sem.at[0])


def _moe_dispatch(hf, da, db, nrows, tm):
    t = hf.shape[0]
    idx = pl.BlockSpec((1, 1, tm), lambda i: (i, 0, 0), memory_space=pltpu.SMEM)
    return pl.pallas_call(
        _moe_dispatch_kernel,
        grid=(t // tm,),
        in_specs=[idx, idx, pl.BlockSpec((tm, D_MODEL), lambda i: (i, 0)),
                  pl.BlockSpec(memory_space=pl.ANY)],
        out_specs=pl.BlockSpec(memory_space=pl.ANY),
        out_shape=jax.ShapeDtypeStruct((nrows, D_MODEL), F32),
        scratch_shapes=[pltpu.SemaphoreType.DMA((1,))],
        input_output_aliases={3: 0},
        compiler_params=_cp("arbitrary"),
        name="moe_dispatch",
    )(da, db, hf, jnp.zeros((nrows, D_MODEL), F32))


def _moe_ffn_kernel(texp_ref, nval_ref, x_ref, wg_ref, wu_ref, wd_ref, y_ref):
    del texp_ref
    i = pl.program_id(0)

    @pl.when(i < nval_ref[0])
    def _():
        x = x_ref[...].astype(BF16)
        gp = _dot(x, wg_ref[0])
        up = _dot(x, wu_ref[0])
        y_ref[...] = _dot(((gp * _sigmoid(gp)) * up).astype(BF16), wd_ref[0])

    @pl.when(i >= nval_ref[0])
    def _():
        y_ref[...] = jnp.zeros_like(y_ref)


def _moe_ffn(xs, texp, nval, wg, wu, wd):
    nrows = xs.shape[0]
    tile = lambda i, te, nv: (jnp.minimum(i, nv[0] - 1), 0)
    wmap = lambda i, te, nv: (te[i], 0, 0)
    grid_spec = pltpu.PrefetchScalarGridSpec(
        num_scalar_prefetch=2,
        grid=(nrows // MOE_TM,),
        in_specs=[pl.BlockSpec((MOE_TM, D_MODEL), tile),
                  pl.BlockSpec((1, D_MODEL, D_EXPERT), wmap), pl.BlockSpec((1, D_MODEL, D_EXPERT), wmap),
                  pl.BlockSpec((1, D_EXPERT, D_MODEL), wmap)],
        out_specs=pl.BlockSpec((MOE_TM, D_MODEL), lambda i, te, nv: (i, 0)),
    )
    return pl.pallas_call(
        _moe_ffn_kernel,
        grid_spec=grid_spec,
        out_shape=jax.ShapeDtypeStruct((nrows, D_MODEL), F32),
        compiler_params=_cp("arbitrary"),
        name="moe_ffn",
    )(texp, nval, xs, wg, wu, wd)


def _moe_combine_kernel(da_ref, db_ref, dan_ref, dbn_ref, info_ref, x_ref, ys_ref, o_ref, buf_ref, sem):
    tm = x_ref.shape[0]
    i = pl.program_id(0)
    n = pl.num_programs(0)
    slot = i % 2

    def gather(ia_ref, ib_ref, s):
        def issue(r, carry):
            _row_copy(ys_ref, ia_ref[0, 0, r], buf_ref.at[s, 0], r, sem.at[s]).start()
            _row_copy(ys_ref, ib_ref[0, 0, r], buf_ref.at[s, 1], r, sem.at[s]).start()
            return carry

        lax.fori_loop(0, tm, issue, 0, unroll=4)

    @pl.when(i == 0)
    def _():
        gather(da_ref, db_ref, 0)

    @pl.when(i + 1 < n)
    def _():
        gather(dan_ref, dbn_ref, 1 - slot)

    _rows_wait(ys_ref, buf_ref.at[slot, 0], tm, sem.at[slot])
    _rows_wait(ys_ref, buf_ref.at[slot, 1], tm, sem.at[slot])
    info = info_ref[...]
    o_ref[...] = (x_ref[...] + info[:, 2:3] * buf_ref[slot, 0]) + info[:, 3:4] * buf_ref[slot, 1]


def _moe_combine(xn, info, da, db, ys, tm):
    t = xn.shape[0]
    n = t // tm
    idx = pl.BlockSpec((1, 1, tm), lambda i: (i, 0, 0), memory_space=pltpu.SMEM)
    nxt = pl.BlockSpec((1, 1, tm), lambda i: (jnp.minimum(i + 1, n - 1), 0, 0), memory_space=pltpu.SMEM)
    return pl.pallas_call(
        _moe_combine_kernel,
        grid=(n,),
        in_specs=[idx, idx, nxt, nxt, pl.BlockSpec((tm, LANES), lambda i: (i, 0)),
                  pl.BlockSpec((tm, D_MODEL), lambda i: (i, 0)), pl.BlockSpec(memory_space=pl.ANY)],
        out_specs=pl.BlockSpec((tm, D_MODEL), lambda i: (i, 0)),
        out_shape=jax.ShapeDtypeStruct((t, D_MODEL), F32),
        scratch_shapes=[pltpu.VMEM((2, 2, tm, D_MODEL), F32), pltpu.SemaphoreType.DMA((2,))],
        compiler_params=_cp("arbitrary"),
        name="moe_combine",
    )(da, db, da, db, info, xn, ys)


def _moe(hf, gates, sel, xn, wg, wu, wd, layer, tm):
    t = hf.shape[0]
    nrows = -(-(2 * t + N_EXPERTS * (MOE_TM - 1)) // MOE_TM) * MOE_TM
    ntile = nrows // MOE_TM
    rank, cnt = _moe_rank(sel, tm)
    cnt = cnt[0, :N_EXPERTS].astype(jnp.int32)
    padded = (cnt + (MOE_TM - 1)) // MOE_TM * MOE_TM
    end = jnp.cumsum(padded)
    off = jnp.zeros((1, LANES), F32).at[0, :N_EXPERTS].set((end - padded).astype(F32))
    nval = (end[-1] // MOE_TM).astype(jnp.int32)
    tile_start = jnp.arange(ntile, dtype=jnp.int32) * MOE_TM
    texp = jnp.sum((end[None, :] <= tile_start[:, None]).astype(jnp.int32), axis=1)
    texp = jnp.minimum(texp, N_EXPERTS - 1)
    texp = jnp.where(jnp.arange(ntile) < nval, texp, texp[jnp.maximum(nval - 1, 0)]) + layer * N_EXPERTS
    info = _moe_dest(sel, gates, rank, off, min(1024, t))
    da = info[:, 0].astype(jnp.int32)
    db = info[:, 1].astype(jnp.int32)
    tmd = min(512, t)
    xs = _moe_dispatch(hf, da.reshape(t // tmd, 1, tmd), db.reshape(t // tmd, 1, tmd), nrows, tmd)
    ys = _moe_ffn(xs, texp, nval.reshape(1), wg, wu, wd)
    return _moe_combine(xn, info, da.reshape(t // tm, 1, tm), db.reshape(t // tm, 1, tm), ys, tm)


def _pad_rows(a, nseq, dec):
    c = a.shape[-1]
    return jnp.pad(a.reshape(nseq, dec, c), ((0, 0), (0, SUB - dec), (0, 0))).reshape(nseq * SUB, c)


def _layer_params(l, ln_mix, w_in, hgrn_norm, fox_bf, fox_qnorm, fox_knorm, gdn_conv, gdn_a_log,
                  gdn_dt_bias, gdn_norm, w_out, ln_ffn, w_group, b_group, w_router, b_router,
                  w_gate, w_up, w_down):
    w = w_in[l]
    gates = jnp.concatenate([w[:, 2560:2568], w[:, 3592:3600]], axis=1)
    wp = jnp.concatenate([w[:, 0:2560], w[:, 2568:3592], gates,
                          jnp.zeros((D_MODEL, LANES - 16), F32)], axis=1).astype(BF16)
    p1 = jnp.zeros((1, LANES), F32).at[0, 0:8].set(fox_bf[l]).at[0, 8:12].set(gdn_dt_bias[l])
    p2 = jnp.zeros((1, LANES), F32).at[0, 8:12].set(gdn_a_log[l])
    wr = jnp.concatenate([w_router[l], w_group[l],
                          jnp.zeros((D_MODEL, LANES - N_GROUPS - N_EXPERTS), F32)], axis=1).astype(BF16)
    br = jnp.zeros((1, LANES), F32).at[0, 0:N_EXPERTS].set(b_router[l])
    br = br.at[0, N_EXPERTS:N_EXPERTS + N_GROUPS].set(b_group[l])
    return dict(
        ln_mix=ln_mix[l][None, :], w=wp, p1=p1, p2=p2,
        qg=jnp.tile(fox_qnorm[l], B_HEADS)[None, :], kg=jnp.tile(fox_knorm[l], B_HEADS)[None, :],
        hgrn_ng=jnp.tile(hgrn_norm[l], A_HEADS)[None, :], gdn_ng=jnp.tile(gdn_norm[l], C_HEADS)[None, :],
        conv=gdn_conv[l], w_out=w_out[l].astype(BF16), ln_ffn=ln_ffn[l][None, :], wr=wr, br=br)


def kernel(x_prompt, x_sample, cache_k, cache_v, cache_logf, page_table, state_hgrn, state_gdn, state_conv,
           ln_mix, w_in, hgrn_lb, hgrn_norm, fox_bf, fox_qnorm, fox_knorm, gdn_conv, gdn_a_log, gdn_dt_bias,
           gdn_norm, w_out, ln_ffn, w_group, b_group, w_router, b_router, w_gate, w_up, w_down):
    nb, seq, _ = x_prompt.shape
    nsq, dec, _ = x_sample.shape
    depth = ln_mix.shape[0]
    n_phys, page = cache_k.shape[1], cache_k.shape[2]
    tp = nb * seq
    ts = nsq * dec
    hs512 = _const(_head_sum_matrix(512))
    ck = jnp.transpose(cache_k, (0, 1, 3, 4, 2)).reshape(depth, n_phys, 512, page)
    cv = jnp.transpose(cache_v, (0, 1, 3, 4, 2)).reshape(depth, n_phys, 512, page)
    clt = jnp.swapaxes(cache_logf, 2, 3)
    tbp = min(256, seq)
    gchunk = min(64, seq)
    seq_blk = 16

    wg = w_gate.astype(BF16).reshape(depth * N_EXPERTS, D_MODEL, D_EXPERT)
    wu = w_up.astype(BF16).reshape(depth * N_EXPERTS, D_MODEL, D_EXPERT)
    wd = w_down.astype(BF16).reshape(depth * N_EXPERTS, D_EXPERT, D_MODEL)

    yp = x_prompt.reshape(tp, D_MODEL)
    ys = x_sample.reshape(ts, D_MODEL)
    outs_p, outs_s = [], []
    for l in range(depth):
        P = _layer_params(l, ln_mix, w_in, hgrn_norm, fox_bf, fox_qnorm, fox_knorm, gdn_conv, gdn_a_log,
                          gdn_dt_bias, gdn_norm, w_out, ln_ffn, w_group, b_group, w_router, b_router,
                          w_gate, w_up, w_down)

        a, bq, bk, bv, c, g = _proj(yp, P["ln_mix"], P["w"], hs512, P["qg"], P["kg"], P["p1"], P["p2"],
                                    tm=min(256, tp))
        qa, ka = _foxprep(bq, bk, g, nb, tm=min(256, seq))
        ob = _fox(qa, ka, bv, nb, tq=min(256, seq))
        oa, hst = _hgrn(a, hgrn_lb, P["hgrn_ng"], None, l, nb, seq // tbp, tbp, 1, tbp // SUB, SUB)
        oc, gst = _gdn(c, g, P["conv"], P["gdn_ng"], None, None, nb, seq // tbp, tbp, 1,
                       tbp // gchunk, gchunk)
        xn, hf, gates, sel = _post(yp, oa, ob, oc, P["w_out"], P["ln_ffn"], P["wr"], P["br"],
                                   tm=min(256, tp))
        yp_new = _moe(hf, gates, sel, xn, wg, wu, wd, l, tm=min(256, tp))
        outs_p.append((bk.reshape(nb, seq, B_HEADS, HD), bv.reshape(nb, seq, B_HEADS, HD),
                       g[:, 0:8].reshape(nb, seq, B_HEADS), hst, gst,
                       c.reshape(nb, seq, 1024)[:, seq - (CONV_W - 1):, 0:768]))
        yp = yp_new

        a, bq, bk, bv, c, g = _proj(ys, P["ln_mix"], P["w"], hs512, P["qg"], P["kg"], P["p1"], P["p2"],
                                    tm=min(256, ts))
        knew = jnp.pad(jnp.swapaxes(bk.reshape(nsq, dec, 512), 1, 2), ((0, 0), (0, 0), (0, SUB - dec)))
        vnew = jnp.pad(jnp.swapaxes(bv.reshape(nsq, dec, 512), 1, 2), ((0, 0), (0, 0), (0, SUB - dec)))
        lfn = jnp.pad(jnp.swapaxes(g[:, 0:8].reshape(nsq, dec, B_HEADS), 1, 2),
                      ((0, 0), (0, 0), (0, SUB - dec)))
        ob = _foxdec(l, page_table, bq.reshape(nsq, dec, 512), knew, vnew, lfn, ck, cv, clt,
                     pp=min(8, page_table.shape[1])).reshape(ts, 512)
        n_outer = nsq // seq_blk
        oa, hst = _hgrn(_pad_rows(a, nsq, dec), hgrn_lb, P["hgrn_ng"], state_hgrn[l], l,
                        n_outer, 1, seq_blk * SUB, seq_blk, 1, dec)
        cinit = jnp.pad(state_conv[l], ((0, 0), (8 - (CONV_W - 1), 0), (0, 0)))
        oc, gst = _gdn(_pad_rows(c, nsq, dec), _pad_rows(g, nsq, dec), P["conv"], P["gdn_ng"], cinit,
                       state_gdn[l], n_outer, 1, seq_blk * SUB, seq_blk, 1, SUB)
        oa = oa.reshape(nsq, SUB, 256)[:, :dec].reshape(ts, 256)
        oc = oc.reshape(nsq, SUB, 256)[:, :dec].reshape(ts, 256)
        xn, hf, gates, sel = _post(ys, oa, ob, oc, P["w_out"], P["ln_ffn"], P["wr"], P["br"],
                                   tm=min(256, ts))
        ys_new = _moe(hf, gates, sel, xn, wg, wu, wd, l, tm=min(256, ts))
        xpad = jnp.concatenate([state_conv[l], c.reshape(nsq, dec, 1024)[:, :, 0:768]], axis=1)
        outs_s.append((bk.reshape(nsq, dec, B_HEADS, HD), bv.reshape(nsq, dec, B_HEADS, HD),
                       g[:, 0:8].reshape(nsq, dec, B_HEADS), hst, gst, xpad[:, -(CONV_W - 1):, :]))
        ys = ys_new

    stack = lambda outs, j: jnp.stack([o[j] for o in outs], axis=0)
    return (yp.reshape(nb, seq, D_MODEL), ys.reshape(nsq, dec, D_MODEL),
            *[stack(outs_p, j) for j in range(6)], *[stack(outs_s, j) for j in range(6)])
```

```python
import functools
import math

import numpy as np
import jax
import jax.numpy as jnp
from jax import lax
from jax.experimental import pallas as pl
from jax.experimental.pallas import tpu as pltpu

F32 = jnp.float32
BF16 = jnp.bfloat16
EPS = 1e-6
NEG = -1e30

D_MODEL = 1024
HD = 64
A_HEADS = 4
B_HEADS = 8
C_HEADS = 4
CONV_W = 4
N_GROUPS = 4
EXPERTS_PER_GROUP = 8
N_EXPERTS = N_GROUPS * EXPERTS_PER_GROUP
D_EXPERT = D_MODEL // 4
SUB = 16
LANES = 128
VMEM_LIMIT = 56 * 1024 * 1024


def _cp(*sem):
    return pltpu.CompilerParams(dimension_semantics=sem, vmem_limit_bytes=VMEM_LIMIT)


def _dot(a, b):
    return jnp.dot(a, b, preferred_element_type=F32)


def _dot_nt(a, b):
    return lax.dot_general(a, b, (((1,), (1,)), ((), ())), preferred_element_type=F32)


def _dot_tn(a, b):
    return lax.dot_general(a, b, (((0,), (0,)), ((), ())), preferred_element_type=F32)


def _split3(x):
    hi = x.astype(BF16)
    r = x - hi.astype(F32)
    mid = r.astype(BF16)
    lo = (r - mid.astype(F32)).astype(BF16)
    return hi, mid, lo


def _split2(x):
    hi = x.astype(BF16)
    return hi, (x - hi.astype(F32)).astype(BF16)


def _exact_left(m, x):
    hi, mid, lo = _split3(x)
    return (_dot(m, hi) + _dot(m, mid)) + _dot(m, lo)


def _exact_right(x, m):
    hi, mid, lo = _split3(x)
    return (_dot(hi, m) + _dot(mid, m)) + _dot(lo, m)


def _idiv(x, n):
    return jnp.right_shift(x, int(math.log2(n)))


def _imod(x, n):
    return jnp.bitwise_and(x, n - 1)


def _sigmoid(x):
    return 1.0 / (1.0 + jnp.exp(-x))


def _softplus(z):
    return jnp.maximum(z, 0.0) + jnp.log(1.0 + jnp.exp(-jnp.abs(z)))


def _head_ssq(z, hs):
    zz = z * z
    hi = zz.astype(BF16)
    lo = (zz - hi.astype(F32)).astype(BF16)
    return _dot(hi, hs) + _dot(lo, hs)


def _const(a, dtype=BF16):
    return jnp.asarray(a, dtype=dtype)


def _head_sum_matrix(width):
    i = np.arange(width)
    return (i[:, None] // HD == i[None, :] // HD).astype(np.float32)


def _chunk_tril(n, c):
    i = np.arange(n)
    return ((i[:, None] // c == i[None, :] // c) & (i[None, :] <= i[:, None])).astype(np.float32)


PROJ_COLS = 3712


def _proj_kernel(x_ref, ln_ref, w_ref, hs_ref, qg_ref, kg_ref, p1_ref, p2_ref,
                 a_ref, bq_ref, bk_ref, bv_ref, c_ref, g_ref):
    x = x_ref[...]
    ms = jnp.mean(x * x, axis=-1, keepdims=True)
    hn = ((x * lax.rsqrt(ms + EPS)) * ln_ref[...]).astype(BF16)
    a_ref[...] = _dot(hn, w_ref[:, 0:1024])
    hs = hs_ref[...]
    q = _dot(hn, w_ref[:, 1024:1536])
    bq_ref[...] = ((q * lax.rsqrt(_head_ssq(q, hs) * (1.0 / HD) + EPS)) * qg_ref[...]) * (HD ** -0.5)
    k = _dot(hn, w_ref[:, 1536:2048])
    bk_ref[...] = (k * lax.rsqrt(_head_ssq(k, hs) * (1.0 / HD) + EPS)) * kg_ref[...]
    bv_ref[...] = _dot(hn, w_ref[:, 2048:2560])
    c_ref[...] = _dot(hn, w_ref[:, 2560:3584])
    gr = _dot(hn, w_ref[:, 3584:3712])
    lane = lax.broadcasted_iota(jnp.int32, gr.shape, 1)
    z = gr + p1_ref[...]
    sp = _softplus(z)
    logsig = jnp.minimum(z, 0.0) - jnp.log(1.0 + jnp.exp(-jnp.abs(z)))
    glog = -jnp.exp(p2_ref[...]) * sp
    beta = _sigmoid(gr)
    g_ref[...] = jnp.where(lane < 8, logsig,
                           jnp.where(lane < 12, glog, jnp.where(lane < 16, beta, 0.0)))


def _proj(x, ln, w, hs512, qg, kg, p1, p2, tm):
    t = x.shape[0]
    row = lambda i: (i, 0)
    fix = lambda i: (0, 0)
    outs = [jax.ShapeDtypeStruct((t, n), F32) for n in (1024, 512, 512, 512, 1024, 128)]
    return pl.pallas_call(
        _proj_kernel,
        grid=(t // tm,),
        in_specs=[pl.BlockSpec((tm, D_MODEL), row), pl.BlockSpec((1, D_MODEL), fix),
                  pl.BlockSpec((D_MODEL, PROJ_COLS), fix), pl.BlockSpec((512, 512), fix),
                  pl.BlockSpec((1, 512), fix), pl.BlockSpec((1, 512), fix),
                  pl.BlockSpec((1, LANES), fix), pl.BlockSpec((1, LANES), fix)],
        out_specs=[pl.BlockSpec((tm, n), row) for n in (1024, 512, 512, 512, 1024, 128)],
        out_shape=outs,
        compiler_params=_cp("arbitrary"),
        name="proj",
    )(x, ln, w, hs512, qg, kg, p1, p2)


def _foxprep_kernel(bq_ref, bk_ref, g_ref, tril_ref, pq_ref, pcq_ref, pck_ref, oq_ref, ok_ref,
                    qa_ref, ka_ref, carry_ref):
    i = pl.program_id(1)

    @pl.when(i == 0)
    def _():
        carry_ref[...] = jnp.zeros_like(carry_ref)

    g = g_ref[...]
    c = _exact_left(tril_ref[...], g) + carry_ref[...]
    carry_ref[...] = c[-1:, :]
    hi, mid, lo = _split3(c)
    pq = pq_ref[...]
    qa = _dot(bq_ref[...].astype(BF16), pq) + oq_ref[...]
    ka = _dot(bk_ref[...].astype(BF16), pq) + ok_ref[...]
    for j, part in enumerate((hi, mid, lo)):
        qa = qa + _dot(part, pcq_ref[j])
        ka = ka - _dot(part, pck_ref[j])
    qa_ref[...] = qa.astype(BF16)
    ka_ref[...] = ka.astype(BF16)


def _foxprep(bq, bk, g, nb, tm):
    t = bq.shape[0]
    nblk = t // nb // tm
    pq = np.zeros((512, 1024), np.float32)
    for h in range(B_HEADS):
        pq[h * HD + np.arange(HD), h * LANES + np.arange(HD)] = 1.0
    pcq = np.zeros((3, LANES, 1024), np.float32)
    pck = np.zeros((3, LANES, 1024), np.float32)
    oq = np.zeros((1, 1024), np.float32)
    ok = np.zeros((1, 1024), np.float32)
    for h in range(B_HEADS):
        for j in range(3):
            pcq[j, h, h * LANES + HD + j] = 1.0
            pck[j, h, h * LANES + HD + 3 + j] = 1.0
            oq[0, h * LANES + HD + 3 + j] = 1.0
            ok[0, h * LANES + HD + j] = 1.0
    row = lambda b, i: (b * nblk + i, 0)
    fix2 = lambda b, i: (0, 0)
    fix3 = lambda b, i: (0, 0, 0)
    return pl.pallas_call(
        _foxprep_kernel,
        grid=(nb, nblk),
        in_specs=[pl.BlockSpec((tm, 512), row), pl.BlockSpec((tm, 512), row),
                  pl.BlockSpec((tm, LANES), row), pl.BlockSpec((tm, tm), fix2),
                  pl.BlockSpec((512, 1024), fix2), pl.BlockSpec((3, LANES, 1024), fix3),
                  pl.BlockSpec((3, LANES, 1024), fix3), pl.BlockSpec((1, 1024), fix2),
                  pl.BlockSpec((1, 1024), fix2)],
        out_specs=[pl.BlockSpec((tm, 1024), row), pl.BlockSpec((tm, 1024), row)],
        out_shape=[jax.ShapeDtypeStruct((t, 1024), BF16)] * 2,
        scratch_shapes=[pltpu.VMEM((1, LANES), F32)],
        compiler_params=_cp("arbitrary", "arbitrary"),
        name="foxprep",
    )(bq, bk, g, _const(_chunk_tril(tm, tm)), _const(pq), _const(pcq), _const(pck),
      _const(oq, F32), _const(ok, F32))


def _fox_kernel(qt_ref, kt_ref, q_ref, k_ref, v_ref, o_ref, m_ref, l_ref, acc_ref, *, tq):
    j = pl.program_id(1)
    qi = qt_ref[j]
    ki = kt_ref[j]

    @pl.when(ki == 0)
    def _():
        m_ref[...] = jnp.full_like(m_ref, NEG)
        l_ref[...] = jnp.zeros_like(l_ref)
        acc_ref[...] = jnp.zeros_like(acc_ref)

    heads = range(B_HEADS)
    lane = lax.broadcasted_iota(jnp.int32, (tq, LANES), 1)
    left = lane < HD

    def update(masked):
        if masked:
            rows = lax.broadcasted_iota(jnp.int32, (tq, tq), 0)
            cols = lax.broadcasted_iota(jnp.int32, (tq, tq), 1)
            keep = cols <= rows
        for pr in range(B_HEADS // 2):
            pair = (2 * pr, 2 * pr + 1)
            ss = [_dot_nt(q_ref[:, h * LANES:(h + 1) * LANES], k_ref[:, h * LANES:(h + 1) * LANES])
                  for h in pair]
            if masked:
                ss = [jnp.where(keep, s, NEG) for s in ss]
            ps, alphas = [], []
            for s, h in zip(ss, pair):
                m_old = m_ref[h]
                m_new = jnp.maximum(m_old, jnp.max(s, axis=1, keepdims=True))
                p = jnp.exp(s - m_new[:, 0:1])
                alpha = jnp.exp(m_old - m_new)
                l_ref[h] = alpha * l_ref[h] + jnp.sum(p, axis=1, keepdims=True)
                m_ref[h] = m_new
                ps.append(p.astype(BF16))
                alphas.append(alpha)
            vp = v_ref[:, pr * LANES:(pr + 1) * LANES]
            upd = _dot(ps[0], jnp.where(left, vp, 0.0).astype(BF16))
            upd = upd + _dot(ps[1], jnp.where(left, 0.0, vp).astype(BF16))
            a = jnp.where(left, alphas[0], alphas[1])
            acc_ref[:, pr * LANES:(pr + 1) * LANES] = a * acc_ref[:, pr * LANES:(pr + 1) * LANES] + upd

    @pl.when(ki < qi)
    def _():
        update(False)

    @pl.when(ki == qi)
    def _():
        update(True)
        for pr in range(B_HEADS // 2):
            l = jnp.where(left, l_ref[2 * pr], l_ref[2 * pr + 1])
            o_ref[:, pr * LANES:(pr + 1) * LANES] = acc_ref[:, pr * LANES:(pr + 1) * LANES] / l


def _fox(qa, ka, bv, nb, tq):
    t = qa.shape[0]
    nblk = t // nb // tq
    qi_tab = np.array([q for q in range(nblk) for _ in range(q + 1)], np.int32)
    ki_tab = np.array([k for q in range(nblk) for k in range(q + 1)], np.int32)
    grid_spec = pltpu.PrefetchScalarGridSpec(
        num_scalar_prefetch=2,
        grid=(nb, len(qi_tab)),
        in_specs=[pl.BlockSpec((tq, 1024), lambda b, j, qt, kt: (b * nblk + qt[j], 0)),
                  pl.BlockSpec((tq, 1024), lambda b, j, qt, kt: (b * nblk + kt[j], 0)),
                  pl.BlockSpec((tq, 512), lambda b, j, qt, kt: (b * nblk + kt[j], 0))],
        out_specs=pl.BlockSpec((tq, 512), lambda b, j, qt, kt: (b * nblk + qt[j], 0)),
        scratch_shapes=[pltpu.VMEM((B_HEADS, tq, LANES), F32), pltpu.VMEM((B_HEADS, tq, LANES), F32),
                        pltpu.VMEM((tq, 512), F32)],
    )
    return pl.pallas_call(
        functools.partial(_fox_kernel, tq=tq),
        grid_spec=grid_spec,
        out_shape=jax.ShapeDtypeStruct((t, 512), F32),
        compiler_params=_cp("arbitrary", "arbitrary"),
        name="fox",
    )(jnp.asarray(qi_tab), jnp.asarray(ki_tab), qa, ka, bv)


def _foxdec_kernel(pt_ref, q_ref, kn_ref, vn_ref, lfn_ref, cum_ref, wide_ref, *rest, pp, dec):
    kp = rest[0:pp]
    vp = rest[pp:2 * pp]
    lp = rest[2 * pp:3 * pp]
    o_ref = rest[3 * pp]
    qr_ref, m_ref, l_ref, r_ref, acc_ref = rest[3 * pp + 1:]
    step = pl.program_id(1)
    nrow = dec * B_HEADS
    rowi = lax.broadcasted_iota(jnp.int32, (nrow, 512), 0)
    coli = lax.broadcasted_iota(jnp.int32, (nrow, 512), 1)
    bd = _idiv(coli, HD) == _imod(rowi, B_HEADS)

    @pl.when(step == 0)
    def _():
        q = q_ref[0]
        qrows = jnp.concatenate(
            [jnp.broadcast_to(q[t:t + 1, :], (B_HEADS, 512)) for t in range(dec)], axis=0)
        qr_ref[...] = jnp.where(bd, qrows, 0.0).astype(BF16)
        m_ref[...] = jnp.full_like(m_ref, NEG)
        l_ref[...] = jnp.zeros_like(l_ref)
        r_ref[...] = jnp.zeros_like(r_ref)
        acc_ref[...] = jnp.zeros_like(acc_ref)

    def pages(krefs, vrefs, lrefs, causal):
        n = len(krefs)
        qr = qr_ref[...]
        ss = [_dot(qr, kr().astype(BF16)) for kr in krefs]
        cums = [_exact_right(lr(), cum_ref[...]) for lr in lrefs]
        r = r_ref[...]
        logits = []
        for u in range(n):
            w = jnp.concatenate([cums[u][:, 0:LANES]] * dec, axis=0)
            logit = ss[u] - (r + w)
            if causal:
                key = lax.broadcasted_iota(jnp.int32, (nrow, LANES), 1)
                trow = _idiv(lax.broadcasted_iota(jnp.int32, (nrow, LANES), 0), B_HEADS)
                logit = jnp.where(key <= trow, logit, NEG)
            logits.append(logit)
            r = r + jnp.concatenate([cums[u][:, LANES:2 * LANES]] * dec, axis=0)
        r_ref[...] = r
        m_old = m_ref[...]
        m_new = m_old
        for lg in logits:
            m_new = jnp.maximum(m_new, jnp.max(lg, axis=1, keepdims=True))
        ps = [jnp.exp(lg - m_new) for lg in logits]
        alpha = jnp.exp(m_old - m_new)
        lsum = jnp.sum(ps[0], axis=1, keepdims=True)
        for p in ps[1:]:
            lsum = lsum + jnp.sum(p, axis=1, keepdims=True)
        l_ref[...] = alpha * l_ref[...] + lsum
        m_ref[...] = m_new
        pv = _dot_nt(ps[0].astype(BF16), vrefs[0]().astype(BF16))
        for u in range(1, n):
            pv = pv + _dot_nt(ps[u].astype(BF16), vrefs[u]().astype(BF16))
        acc_ref[...] = alpha[:, 0:1] * acc_ref[...] + pv

    pages([lambda u=u: kp[u][...] for u in range(pp)], [lambda u=u: vp[u][...] for u in range(pp)],
          [lambda u=u: lp[u][...] for u in range(pp)], False)

    @pl.when(step == pl.num_programs(1) - 1)
    def _():
        wide = wide_ref[...]
        knew = _dot(kn_ref[0].astype(BF16), wide)
        vnew = _dot(vn_ref[0].astype(BF16), wide)
        lfnew = _exact_right(lfn_ref[0], wide)
        pages([lambda: knew], [lambda: vnew], [lambda: lfnew], True)
        res = jnp.where(bd, acc_ref[...] / l_ref[:, 0:1], 0.0)
        for t in range(dec):
            o_ref[0, t:t + 1, :] = jnp.sum(res[t * B_HEADS:(t + 1) * B_HEADS], axis=0, keepdims=True)


def _foxdec(layer, page_table, q, knew, vnew, lfn_t, cache_k, cache_v, cache_lft, pp):
    ns, dec, _ = q.shape
    n_pages = page_table.shape[1]
    page = cache_k.shape[3]
    nsteps = n_pages // pp
    nrow = dec * B_HEADS
    cum = np.concatenate([np.triu(np.ones((page, page), np.float32)), np.ones((page, page), np.float32)], axis=1)
    seq3 = lambda s, p, pt: (s, 0, 0)
    fix2 = lambda s, p, pt: (0, 0)

    def cache_map(u):
        return lambda s, p, pt: (layer, pt[s, p * pp + u], 0, 0)

    wide = np.eye(SUB, page, dtype=np.float32)
    in_specs = [pl.BlockSpec((1, dec, 512), seq3), pl.BlockSpec((1, 512, SUB), seq3),
                pl.BlockSpec((1, 512, SUB), seq3), pl.BlockSpec((1, B_HEADS, SUB), seq3),
                pl.BlockSpec((page, 2 * page), fix2), pl.BlockSpec((SUB, page), fix2)]
    in_specs += [pl.BlockSpec((None, None, 512, page), cache_map(u)) for u in range(pp)]
    in_specs += [pl.BlockSpec((None, None, 512, page), cache_map(u)) for u in range(pp)]
    in_specs += [pl.BlockSpec((None, None, B_HEADS, page), cache_map(u)) for u in range(pp)]
    grid_spec = pltpu.PrefetchScalarGridSpec(
        num_scalar_prefetch=1,
        grid=(ns, nsteps),
        in_specs=in_specs,
        out_specs=pl.BlockSpec((1, dec, 512), seq3),
        scratch_shapes=[pltpu.VMEM((nrow, 512), BF16), pltpu.VMEM((nrow, LANES), F32),
                        pltpu.VMEM((nrow, LANES), F32), pltpu.VMEM((nrow, LANES), F32),
                        pltpu.VMEM((nrow, 512), F32)],
    )
    return pl.pallas_call(
        functools.partial(_foxdec_kernel, pp=pp, dec=dec),
        grid_spec=grid_spec,
        out_shape=jax.ShapeDtypeStruct((ns, dec, 512), F32),
        compiler_params=_cp("arbitrary", "arbitrary"),
        name="foxdec",
    )(page_table, q, knew, vnew, lfn_t, _const(cum), _const(wide),
      *([cache_k] * pp), *([cache_v] * pp), *([cache_lft] * pp))


def _hgrn_kernel(*refs, layer, ns, cps, valid, has_init):
    (q_ref, f_ref, v_ref, gate_ref, lbraw_ref, ng_ref, tril_ref, rs_ref, tile_ref, hs_ref) = refs[:10]
    pos = 10
    s0_ref = None
    if has_init:
        s0_ref = refs[pos]
        pos += 1
    o_ref, sout_ref, st_ref, oin_ref, oint_ref, tmp_ref = refs[pos:pos + 6]
    i = pl.program_id(1)
    tb = q_ref.shape[0]
    nchunk = tb // SUB
    width = A_HEADS * HD

    lbraw = lbraw_ref[...]
    e = jnp.exp(lbraw - jnp.max(lbraw, axis=0, keepdims=True))
    prob = e / jnp.sum(e, axis=0, keepdims=True)
    lb = jnp.zeros((1, width), F32)
    for d in range(1, layer + 1):
        lb = lb + prob[d:d + 1, :]

    fr = f_ref[...]
    qr = q_ref[...]
    vr = v_ref[...]
    logf = jnp.log(lb + (1.0 - lb) * _sigmoid(fr))
    key = (1.0 - lb) * _sigmoid(-fr)
    if valid < SUB:
        rowv = _imod(lax.broadcasted_iota(jnp.int32, (tb, width), 0), SUB) < valid
        logf = jnp.where(rowv, logf, 0.0)
        key = jnp.where(rowv, key, 0.0)
    q = (qr * _sigmoid(qr)) * (HD ** -0.5)
    g = _exact_left(tril_ref[...], logf)
    g3 = g.reshape(nchunk, SUB, width)
    q3 = q.reshape(nchunk, SUB, width)
    k3 = key.reshape(nchunk, SUB, width)

    att = jnp.zeros((tb, LANES), F32)
    for s in range(SUB):
        d = jnp.minimum(g3 - g3[:, s:s + 1, :], 0.0)
        p = (q3 * jnp.exp(d)) * k3[:, s:s + 1, :]
        att = att + _dot(p.reshape(tb, width).astype(BF16), rs_ref[s])
    attb = att.astype(BF16)

    rows = lax.broadcasted_iota(jnp.int32, (tb, tb), 0)
    cols = lax.broadcasted_iota(jnp.int32, (tb, tb), 1)
    causal = (_idiv(rows, SUB) == _idiv(cols, SUB)) & (cols <= rows)
    lane = lax.broadcasted_iota(jnp.int32, (tb, LANES), 1)
    for pr in range(2):
        vpair = vr[:, pr * LANES:(pr + 1) * LANES]
        acc = None
        for hh in range(2):
            h = 2 * pr + hh
            full = _dot(attb, tile_ref[h])
            full = jnp.where(causal, full, 0.0).astype(BF16)
            vm = jnp.where((lane < HD) == (hh == 0), vpair, 0.0).astype(BF16)
            term = _dot(full, vm)
            acc = term if acc is None else acc + term
        oin_ref[:, pr * LANES:(pr + 1) * LANES] = acc

    gend3 = g3[:, SUB - 1:SUB, :]
    kd, kdl = _split2((k3 * jnp.exp(gend3 - g3)).reshape(tb, width))
    eg = jnp.exp(g)
    qg = (q * eg).astype(BF16)
    r128 = lax.broadcasted_iota(jnp.int32, (LANES, LANES), 0)
    c128 = lax.broadcasted_iota(jnp.int32, (LANES, LANES), 1)
    bdm = _idiv(r128, HD) == _idiv(c128, HD)
    vb, vbl = _split2(vr)
    last = i == pl.num_programs(1) - 1
    units = [(sq, pr) for sq in range(ns) for pr in range(2)]
    rowsl = lambda sq, c: slice((sq * cps + c) * SUB, (sq * cps + c + 1) * SUB)
    lanesl = lambda pr: slice(pr * LANES, (pr + 1) * LANES)
    incs = {}
    for sq, pr in units:
        for c in range(cps):
            rs_, ls_ = rowsl(sq, c), lanesl(pr)
            u = (_dot_tn(vb[rs_, ls_], kd[rs_, ls_]) + _dot_tn(vb[rs_, ls_], kdl[rs_, ls_])) \
                + _dot_tn(vbl[rs_, ls_], kd[rs_, ls_])
            incs[sq, pr, c] = jnp.where(bdm, u, 0.0)
    if has_init:
        tmp_ref[...] = jnp.zeros_like(tmp_ref)
        for sq, pr in units:
            tmp_ref[2 * sq + pr, 0:HD, 0:HD] = s0_ref[sq, 2 * pr]
            tmp_ref[2 * sq + pr, HD:LANES, HD:LANES] = s0_ref[sq, 2 * pr + 1]
    seen = {}
    final = {}
    for sq, pr in units:
        if has_init:
            st = tmp_ref[2 * sq + pr].T
        else:
            st = jnp.where(i == 0, 0.0, st_ref[pr])
        for c in range(cps):
            seen[sq, pr, c] = st.astype(BF16)
            r_end = (sq * cps + c + 1) * SUB
            st = st * eg[r_end - 1:r_end, lanesl(pr)] + incs[sq, pr, c]
        if not has_init:
            st_ref[pr] = st
        final[sq, pr] = st
    for sq, pr in units:
        for c in range(cps):
            oint_ref[rowsl(sq, c), lanesl(pr)] = _dot_nt(qg[rowsl(sq, c), lanesl(pr)], seen[sq, pr, c])

    @pl.when(last)
    def _():
        for sq, pr in units:
            tmp_ref[2 * sq + pr] = final[sq, pr].T
        for sq, pr in units:
            sout_ref[sq, 2 * pr] = tmp_ref[2 * sq + pr, 0:HD, 0:HD]
            sout_ref[sq, 2 * pr + 1] = tmp_ref[2 * sq + pr, HD:LANES, HD:LANES]

    o = oin_ref[...] + oint_ref[...]
    gr = gate_ref[...]
    o = (o * lax.rsqrt(_head_ssq(o, hs_ref[...]) * (1.0 / HD) + EPS)) * ng_ref[...]
    o_ref[...] = o * (gr * _sigmoid(gr))


def _hgrn(a, lbraw, ng, s0, layer, n_outer, n_inner, tb, ns, cps, valid):
    rows = a.shape[0]
    width = A_HEADS * HD
    has_init = s0 is not None
    nseq = n_outer * ns
    rs = np.zeros((SUB, width, LANES), np.float32)
    tile = np.zeros((A_HEADS, LANES, tb), np.float32)
    for s in range(SUB):
        for h in range(A_HEADS):
            rs[s, h * HD + np.arange(HD), h * SUB + s] = 1.0
            tile[h, h * SUB + s, np.arange(tb // SUB) * SUB + s] = 1.0

    def col(j):
        return lambda o, i: (o * n_inner + i, j)

    fix2 = lambda o, i: (0, 0)
    fix3 = lambda o, i: (0, 0, 0)
    in_specs = [pl.BlockSpec((tb, width), col(j)) for j in range(4)]
    in_specs += [pl.BlockSpec(lbraw.shape, fix2), pl.BlockSpec((1, width), fix2),
                 pl.BlockSpec((tb, tb), fix2), pl.BlockSpec((SUB, width, LANES), fix3),
                 pl.BlockSpec((A_HEADS, LANES, tb), fix3), pl.BlockSpec((width, width), fix2)]
    args = [a, a, a, a, lbraw, ng, _const(_chunk_tril(tb, SUB)), _const(rs), _const(tile),
            _const(_head_sum_matrix(width))]
    if has_init:
        in_specs.append(pl.BlockSpec((ns, A_HEADS, HD, HD), lambda o, i: (o, 0, 0, 0)))
        args.append(s0)
    return pl.pallas_call(
        functools.partial(_hgrn_kernel, layer=layer, ns=ns, cps=cps, valid=valid, has_init=has_init),
        grid=(n_outer, n_inner),
        in_specs=in_specs,
        out_specs=[pl.BlockSpec((tb, width), lambda o, i: (o * n_inner + i, 0)),
                   pl.BlockSpec((ns, A_HEADS, HD, HD), lambda o, i: (o, 0, 0, 0))],
        out_shape=[jax.ShapeDtypeStruct((rows, width), F32),
                   jax.ShapeDtypeStruct((nseq, A_HEADS, HD, HD), F32)],
        scratch_shapes=[pltpu.VMEM((2, LANES, LANES), F32), pltpu.VMEM((tb, width), F32),
                        pltpu.VMEM((tb, width), F32), pltpu.VMEM((2 * ns, LANES, LANES), F32)],
        compiler_params=_cp("arbitrary", "arbitrary"),
        name="hgrn",
    )(*args)


def _gdn_kernel(*refs, ns, cps, chunk, has_init):
    (x_ref, z_ref, g_ref, cw_ref, ng_ref, tril_ref, hs_ref, eg_ref, eb_ref, eg2_ref, eb2_ref) = refs[:11]
    pos = 11
    cinit_ref = s0_ref = None
    if has_init:
        cinit_ref, s0_ref = refs[pos:pos + 2]
        pos += 2
    o_ref, sout_ref, st_ref, cv_ref, oacc_ref, vn_ref, tmp_ref = refs[pos:pos + 7]
    i = pl.program_id(1)
    tb = x_ref.shape[0]
    width = C_HEADS * HD
    rows_seq = cps * chunk
    last = i == pl.num_programs(1) - 1

    x = x_ref[...]
    cw = cw_ref[...]
    acts = []
    for sq in range(ns):
        xs = x[sq * rows_seq:(sq + 1) * rows_seq]
        if has_init:
            prev = cinit_ref[sq]
        else:
            prev = jnp.where(i == 0, 0.0, cv_ref[...])
        xc = jnp.concatenate([prev, xs], axis=0)
        conv = xc * cw[CONV_W - 1:CONV_W, :]
        for j in range(1, CONV_W):
            conv = conv + pltpu.roll(xc, j, 0) * cw[CONV_W - 1 - j:CONV_W - j, :]
        acts.append(conv[8:])
        if not has_init:
            cv_ref[...] = xs[rows_seq - 8:]
    conv = acts[0] if ns == 1 else jnp.concatenate(acts, axis=0)
    act = conv * _sigmoid(conv)
    hs = hs_ref[...]
    q = act[:, 0:width]
    k = act[:, width:2 * width]
    v = act[:, 2 * width:3 * width]
    q = (q * lax.rsqrt(_head_ssq(q, hs) + EPS)) * (HD ** -0.5)
    k = k * lax.rsqrt(_head_ssq(k, hs) + EPS)

    gt = g_ref[...]
    gc = _exact_left(tril_ref[...], gt)
    gexp = _exact_right(gc, eg_ref[...])
    bexp = _exact_right(gt, eb_ref[...])
    gcol = _exact_right(gc, eg2_ref[...])
    bcol = _exact_right(gt, eb2_ref[...])
    nchunk = tb // chunk
    gexp3 = gexp.reshape(nchunk, chunk, width)
    gend3 = gexp3[:, chunk - 1:chunk, :]
    eg = jnp.exp(gexp)
    rhs = jnp.concatenate([v * bexp, (k * bexp) * eg], axis=1).astype(BF16)
    qg = (q * eg).astype(BF16)
    kdf = (k.reshape(nchunk, chunk, width) * jnp.exp(gend3 - gexp3)).reshape(tb, width)
    kb = k.astype(BF16)
    lane = lax.broadcasted_iota(jnp.int32, (tb, width), 1)
    kmask = [jnp.where(_idiv(lane, HD) == h, k, 0.0).astype(BF16) for h in range(C_HEADS)]
    qmask = [jnp.where(_idiv(lane, HD) == h, q, 0.0).astype(BF16) for h in range(C_HEADS)]

    rc = lax.broadcasted_iota(jnp.int32, (tb, tb), 0)
    cc = lax.broadcasted_iota(jnp.int32, (tb, tb), 1)
    same = _idiv(rc, chunk) == _idiv(cc, chunk)
    eye = rc == cc
    incl = same & (cc <= rc)
    strict = same & (cc < rc)
    hl = _idiv(lane, HD)
    r256 = lax.broadcasted_iota(jnp.int32, (width, width), 0)
    c256 = lax.broadcasted_iota(jnp.int32, (width, width), 1)
    bdm = _idiv(r256, HD) == _idiv(c256, HD)
    nstage = int(math.log2(chunk))
    heads = range(C_HEADS)
    reps = tb // LANES
    gct = gc.T
    decay, xm, tm, qkd = [], [], [], []
    for h in heads:
        g_t = jnp.concatenate([gcol[:, h * LANES:(h + 1) * LANES]] * reps, axis=1)
        decay.append(jnp.exp(jnp.minimum(g_t - gct[8 + h:9 + h, :], 0.0)))
    kk = [_dot_nt(kmask[h], kb) for h in heads]
    qk = [_dot_nt(qmask[h], kb) for h in heads]
    for h in heads:
        b_t = jnp.concatenate([bcol[:, h * LANES:(h + 1) * LANES]] * reps, axis=1)
        x0 = -jnp.where(strict, (b_t * kk[h]) * decay[h], 0.0)
        xm.append(x0)
        tm.append(jnp.where(eye, 1.0, 0.0) + x0)
        qkd.append(jnp.where(incl, qk[h] * decay[h], 0.0).astype(BF16))
    for _ in range(nstage - 1):
        xb = [x.astype(BF16) for x in xm]
        xm = [_dot(xb[h], xb[h]) for h in heads]
        tm = [tm[h] + _dot(tm[h].astype(BF16), xm[h].astype(BF16)) for h in heads]
    sol = [_dot(tm[h].astype(BF16), rhs) for h in heads]
    u = jnp.zeros((tb, width), F32)
    w = jnp.zeros((tb, width), F32)
    for h in heads:
        u = u + jnp.where(hl == h, sol[h][:, 0:width], 0.0)
        w = w + jnp.where(hl == h, sol[h][:, width:2 * width], 0.0)
    wh, wl = _split2(w)
    kdh, kdl = _split2(kdf)

    if has_init:
        tmp_ref[...] = jnp.zeros_like(tmp_ref)
        for sq in range(ns):
            for h in heads:
                tmp_ref[sq, h * HD:(h + 1) * HD, h * HD:(h + 1) * HD] = s0_ref[sq, h]
        sts = [tmp_ref[sq] for sq in range(ns)]
    else:
        sts = [jnp.where(i == 0, 0.0, st_ref[...])]
    rowsl = lambda sq, c: slice((sq * cps + c) * chunk, (sq * cps + c + 1) * chunk)
    for c in range(cps):
        sth, stl = zip(*[_split2(st) for st in sts])
        ws = [(_dot(wh[rowsl(sq, c)], sth[sq]) + _dot(wh[rowsl(sq, c)], stl[sq]))
              + _dot(wl[rowsl(sq, c)], sth[sq]) for sq in range(ns)]
        for sq in range(ns):
            oacc_ref[rowsl(sq, c), :] = _dot(qg[rowsl(sq, c)], sth[sq])
        vnf = [u[rowsl(sq, c)] - ws[sq] for sq in range(ns)]
        vnh, vnl = zip(*[_split2(v_) for v_ in vnf])
        for sq in range(ns):
            vn_ref[rowsl(sq, c), :] = vnh[sq]
        incs = [jnp.where(bdm, (_dot_tn(kdh[rowsl(sq, c)], vnh[sq]) + _dot_tn(kdh[rowsl(sq, c)], vnl[sq]))
                          + _dot_tn(kdl[rowsl(sq, c)], vnh[sq]), 0.0) for sq in range(ns)]
        sts = [sts[sq] * eg[(sq * cps + c + 1) * chunk - 1:(sq * cps + c + 1) * chunk, :] + incs[sq]
               for sq in range(ns)]
    if not has_init:
        st_ref[...] = sts[0]

    @pl.when(last)
    def _():
        for sq in range(ns):
            tmp_ref[sq] = sts[sq]
        for sq in range(ns):
            for h in heads:
                sout_ref[sq, h] = tmp_ref[sq, h * HD:(h + 1) * HD, h * HD:(h + 1) * HD]

    vnb = vn_ref[...]
    o = oacc_ref[...]
    for h in heads:
        o = o + jnp.where(hl == h, _dot(qkd[h], vnb), 0.0)
    z = z_ref[...]
    o = (o * lax.rsqrt(_head_ssq(o, hs) * (1.0 / HD) + EPS)) * ng_ref[...]
    o_ref[...] = o * (z * _sigmoid(z))


def _gdn(cfull, g, cw, ng, cinit, s0, n_outer, n_inner, tb, ns, cps, chunk):
    rows = cfull.shape[0]
    width = C_HEADS * HD
    has_init = s0 is not None
    nseq = n_outer * ns
    eg = np.zeros((LANES, width), np.float32)
    eb = np.zeros((LANES, width), np.float32)
    eg2 = np.zeros((LANES, C_HEADS * LANES), np.float32)
    eb2 = np.zeros((LANES, C_HEADS * LANES), np.float32)
    for h in range(C_HEADS):
        eg[8 + h, h * HD:(h + 1) * HD] = 1.0
        eb[12 + h, h * HD:(h + 1) * HD] = 1.0
        eg2[8 + h, h * LANES:(h + 1) * LANES] = 1.0
        eb2[12 + h, h * LANES:(h + 1) * LANES] = 1.0
    rowm = lambda o, i: (o * n_inner + i, 0)
    fix2 = lambda o, i: (0, 0)
    in_specs = [pl.BlockSpec((tb, 3 * width), rowm),
                pl.BlockSpec((tb, width), lambda o, i: (o * n_inner + i, 3)),
                pl.BlockSpec((tb, LANES), rowm), pl.BlockSpec((CONV_W, 3 * width), fix2),
                pl.BlockSpec((1, width), fix2), pl.BlockSpec((tb, tb), fix2),
                pl.BlockSpec((width, width), fix2), pl.BlockSpec((LANES, width), fix2),
                pl.BlockSpec((LANES, width), fix2), pl.BlockSpec((LANES, C_HEADS * LANES), fix2),
                pl.BlockSpec((LANES, C_HEADS * LANES), fix2)]
    args = [cfull, cfull, g, cw, ng, _const(_chunk_tril(tb, chunk)), _const(_head_sum_matrix(width)),
            _const(eg), _const(eb), _const(eg2), _const(eb2)]
    if has_init:
        in_specs += [pl.BlockSpec((ns, 8, 3 * width), lambda o, i: (o, 0, 0)),
                     pl.BlockSpec((ns, C_HEADS, HD, HD), lambda o, i: (o, 0, 0, 0))]
        args += [cinit, s0]
    return pl.pallas_call(
        functools.partial(_gdn_kernel, ns=ns, cps=cps, chunk=chunk, has_init=has_init),
        grid=(n_outer, n_inner),
        in_specs=in_specs,
        out_specs=[pl.BlockSpec((tb, width), rowm),
                   pl.BlockSpec((ns, C_HEADS, HD, HD), lambda o, i: (o, 0, 0, 0))],
        out_shape=[jax.ShapeDtypeStruct((rows, width), F32),
                   jax.ShapeDtypeStruct((nseq, C_HEADS, HD, HD), F32)],
        scratch_shapes=[pltpu.VMEM((width, width), F32), pltpu.VMEM((8, 3 * width), F32),
                        pltpu.VMEM((tb, width), F32), pltpu.VMEM((tb, width), BF16),
                        pltpu.VMEM((ns, width, width), F32)],
        compiler_params=_cp("arbitrary", "arbitrary"),
        name="gdn",
    )(*args)


def _post_kernel(x_ref, oa_ref, ob_ref, oc_ref, wo_ref, ln_ref, wr_ref, br_ref, tril_ref,
                 xn_ref, hf_ref, gate_ref, sel_ref, rank_ref, cnt_ref, carry_ref):
    x = x_ref[...]
    mix = _dot(oa_ref[...].astype(BF16), wo_ref[0:256, :])
    mix = mix + _dot(ob_ref[...].astype(BF16), wo_ref[256:768, :])
    mix = mix + _dot(oc_ref[...].astype(BF16), wo_ref[768:1024, :])
    xn = x + mix
    xn_ref[...] = xn
    ms = jnp.mean(xn * xn, axis=-1, keepdims=True)
    hf = (xn * lax.rsqrt(ms + EPS)) * ln_ref[...]
    hf_ref[...] = hf
    logits = _dot(hf.astype(BF16), wr_ref[...]) + br_ref[...]
    lane = lax.broadcasted_iota(jnp.int32, logits.shape, 1)
    big = jnp.int32(1 << 20)
    isg = (lane >= N_EXPERTS) & (lane < N_EXPERTS + N_GROUPS)
    gl = jnp.where(isg, logits, NEG)
    gm = jnp.max(gl, axis=1, keepdims=True)
    gidx = jnp.min(jnp.where(isg & (gl == gm), lane, big), axis=1, keepdims=True) - N_EXPERTS
    top_gp = 1.0 / jnp.sum(jnp.where(isg, jnp.exp(gl - gm), 0.0), axis=1, keepdims=True)
    ing = (lane < N_EXPERTS) & (_idiv(lane, EXPERTS_PER_GROUP) == gidx)
    el = jnp.where(ing, logits, NEG)
    em = jnp.max(el, axis=1, keepdims=True)
    ee = jnp.where(ing, jnp.exp(el - em), 0.0)
    prob = ee / jnp.sum(ee, axis=1, keepdims=True)
    p1 = jnp.max(prob, axis=1, keepdims=True)
    i1 = jnp.min(jnp.where(ing & (prob == p1), lane, big), axis=1, keepdims=True)
    rest = jnp.where(ing & (lane != i1), prob, -1.0)
    p2 = jnp.max(rest, axis=1, keepdims=True)
    i2 = jnp.min(jnp.where(ing & (lane != i1) & (rest == p2), lane, big), axis=1, keepdims=True)
    den = p1 + p2
    gate_ref[...] = jnp.where(lane == i1, (top_gp * p1) / den,
                              jnp.where(lane == i2, (top_gp * p2) / den, 0.0))
    sel = jnp.where((lane == i1) | (lane == i2), 1.0, 0.0)
    sel_ref[...] = sel

    @pl.when(pl.program_id(0) == 0)
    def _():
        carry_ref[...] = jnp.zeros_like(carry_ref)

    rank_ref[...] = _dot(tril_ref[...], sel.astype(BF16)) + carry_ref[...]
    carry_ref[...] = carry_ref[...] + jnp.sum(sel, axis=0, keepdims=True)
    cnt_ref[...] = carry_ref[...]


def _post(x, oa, ob, oc, wo, ln, wr, br, tm):
    t = x.shape[0]
    row = lambda i: (i, 0)
    fix = lambda i: (0, 0)
    return pl.pallas_call(
        _post_kernel,
        grid=(t // tm,),
        in_specs=[pl.BlockSpec((tm, D_MODEL), row), pl.BlockSpec((tm, 256), row),
                  pl.BlockSpec((tm, 512), row), pl.BlockSpec((tm, 256), row),
                  pl.BlockSpec((D_MODEL, D_MODEL), fix), pl.BlockSpec((1, D_MODEL), fix),
                  pl.BlockSpec((D_MODEL, LANES), fix), pl.BlockSpec((1, LANES), fix),
                  pl.BlockSpec((tm, tm), fix)],
        out_specs=[pl.BlockSpec((tm, D_MODEL), row), pl.BlockSpec((tm, D_MODEL), row),
                   pl.BlockSpec((tm, LANES), row), pl.BlockSpec((tm, LANES), row),
                   pl.BlockSpec((tm, LANES), row), pl.BlockSpec((1, LANES), fix)],
        out_shape=[jax.ShapeDtypeStruct((t, D_MODEL), F32), jax.ShapeDtypeStruct((t, D_MODEL), F32),
                   jax.ShapeDtypeStruct((t, LANES), F32), jax.ShapeDtypeStruct((t, LANES), F32),
                   jax.ShapeDtypeStruct((t, LANES), F32), jax.ShapeDtypeStruct((1, LANES), F32)],
        scratch_shapes=[pltpu.VMEM((1, LANES), F32)],
        compiler_params=_cp("arbitrary"),
        name="post",
    )(x, oa, ob, oc, wo, ln, wr, br, _const(np.tril(np.ones((tm, tm), np.float32), -1)))


MOE_TM = 256


def _moe_dest_kernel(sel_ref, gate_ref, rank_ref, off_ref, info_ref):
    on = sel_ref[...] > 0.0
    gates = gate_ref[...]
    dest = off_ref[...] + rank_ref[...]
    lane = lax.broadcasted_iota(jnp.int32, gates.shape, 1)
    la = jnp.min(jnp.where(on, lane, LANES), axis=1, keepdims=True)
    lb = jnp.max(jnp.where(on, lane, -1), axis=1, keepdims=True)
    pick = lambda v, l: jnp.sum(jnp.where(lane == l, v, 0.0), axis=1, keepdims=True)
    info_ref[...] = jnp.where(lane == 0, pick(dest, la),
                              jnp.where(lane == 1, pick(dest, lb),
                                        jnp.where(lane == 2, pick(gates, la),
                                                  jnp.where(lane == 3, pick(gates, lb), 0.0))))


def _moe_dest(sel, gates, rank, off, tm):
    t = sel.shape[0]
    row = lambda i: (i, 0)
    return pl.pallas_call(
        _moe_dest_kernel,
        grid=(t // tm,),
        in_specs=[pl.BlockSpec((tm, LANES), row)] * 3 + [pl.BlockSpec((1, LANES), lambda i: (0, 0))],
        out_specs=pl.BlockSpec((tm, LANES), row),
        out_shape=jax.ShapeDtypeStruct((t, LANES), F32),
        compiler_params=_cp("arbitrary"),
        name="moe_dest",
    )(sel, gates, rank, off)


def _row_copy(src_ref, src_row, dst_ref, dst_row, sem):
    return pltpu.make_async_copy(src_ref.at[pl.ds(src_row, 1), :], dst_ref.at[pl.ds(dst_row, 1), :], sem)


def _rows_wait(src_ref, dst_ref, nrows, sem):
    pltpu.make_async_copy(src_ref.at[pl.ds(0, nrows), :], dst_ref.at[pl.ds(0, nrows), :], sem).wait()


def _moe_dispatch_kernel(da_ref, db_ref, hf_ref, xs_in_ref, xs_ref, sem):
    del xs_in_ref
    tm = hf_ref.shape[0]

    def issue(r, carry):
        _row_copy(hf_ref, r, xs_ref, da_ref[0, 0, r], sem.at[0]).start()
        _row_copy(hf_ref, r, xs_ref, db_ref[0, 0, r], sem.at[0]).start()
        return carry

    lax.fori_loop(0, tm, issue, 0, unroll=4)
    _rows_wait(hf_ref, xs_ref, tm, sem.at[0])
    _rows_wait(hf_ref, xs_ref, tm, sem.at[0])


def _moe_dispatch(hf, da, db, nrows, tm):
    t = hf.shape[0]
    idx = pl.BlockSpec((1, 1, tm), lambda i: (i, 0, 0), memory_space=pltpu.SMEM)
    return pl.pallas_call(
        _moe_dispatch_kernel,
        grid=(t // tm,),
        in_specs=[idx, idx, pl.BlockSpec((tm, D_MODEL), lambda i: (i, 0)),
                  pl.BlockSpec(memory_space=pl.ANY)],
        out_specs=pl.BlockSpec(memory_space=pl.ANY),
        out_shape=jax.ShapeDtypeStruct((nrows, D_MODEL), F32),
        scratch_shapes=[pltpu.SemaphoreType.DMA((1,))],
        input_output_aliases={3: 0},
        compiler_params=_cp("arbitrary"),
        name="moe_dispatch",
    )(da, db, hf, jnp.zeros((nrows, D_MODEL), F32))


def _moe_ffn_kernel(texp_ref, nval_ref, x_ref, wg_ref, wu_ref, wd_ref, y_ref):
    del texp_ref
    i = pl.program_id(0)

    @pl.when(i < nval_ref[0])
    def _():
        x = x_ref[...].astype(BF16)
        gp = _dot(x, wg_ref[0])
        up = _dot(x, wu_ref[0])
        y_ref[...] = _dot(((gp * _sigmoid(gp)) * up).astype(BF16), wd_ref[0])

    @pl.when(i >= nval_ref[0])
    def _():
        y_ref[...] = jnp.zeros_like(y_ref)


def _moe_ffn(xs, texp, nval, wg, wu, wd):
    nrows = xs.shape[0]
    tile = lambda i, te, nv: (jnp.minimum(i, nv[0] - 1), 0)
    wmap = lambda i, te, nv: (te[i], 0, 0)
    grid_spec = pltpu.PrefetchScalarGridSpec(
        num_scalar_prefetch=2,
        grid=(nrows // MOE_TM,),
        in_specs=[pl.BlockSpec((MOE_TM, D_MODEL), tile),
                  pl.BlockSpec((1, D_MODEL, D_EXPERT), wmap), pl.BlockSpec((1, D_MODEL, D_EXPERT), wmap),
                  pl.BlockSpec((1, D_EXPERT, D_MODEL), wmap)],
        out_specs=pl.BlockSpec((MOE_TM, D_MODEL), lambda i, te, nv: (i, 0)),
    )
    return pl.pallas_call(
        _moe_ffn_kernel,
        grid_spec=grid_spec,
        out_shape=jax.ShapeDtypeStruct((nrows, D_MODEL), F32),
        compiler_params=_cp("arbitrary"),
        name="moe_ffn",
    )(texp, nval, xs, wg, wu, wd)


def _moe_combine_kernel(da_ref, db_ref, dan_ref, dbn_ref, info_ref, x_ref, ys_ref, o_ref, buf_ref, sem):
    tm = x_ref.shape[0]
    i = pl.program_id(0)
    n = pl.num_programs(0)
    slot = i % 2

    def gather(ia_ref, ib_ref, s):
        def issue(r, carry):
            _row_copy(ys_ref, ia_ref[0, 0, r], buf_ref.at[s, 0], r, sem.at[s]).start()
            _row_copy(ys_ref, ib_ref[0, 0, r], buf_ref.at[s, 1], r, sem.at[s]).start()
            return carry

        lax.fori_loop(0, tm, issue, 0, unroll=4)

    @pl.when(i == 0)
    def _():
        gather(da_ref, db_ref, 0)

    @pl.when(i + 1 < n)
    def _():
        gather(dan_ref, dbn_ref, 1 - slot)

    _rows_wait(ys_ref, buf_ref.at[slot, 0], tm, sem.at[slot])
    _rows_wait(ys_ref, buf_ref.at[slot, 1], tm, sem.at[slot])
    info = info_ref[...]
    o_ref[...] = (x_ref[...] + info[:, 2:3] * buf_ref[slot, 0]) + info[:, 3:4] * buf_ref[slot, 1]


def _moe_combine(xn, info, da, db, ys, tm):
    t = xn.shape[0]
    n = t // tm
    idx = pl.BlockSpec((1, 1, tm), lambda i: (i, 0, 0), memory_space=pltpu.SMEM)
    nxt = pl.BlockSpec((1, 1, tm), lambda i: (jnp.minimum(i + 1, n - 1), 0, 0), memory_space=pltpu.SMEM)
    return pl.pallas_call(
        _moe_combine_kernel,
        grid=(n,),
        in_specs=[idx, idx, nxt, nxt, pl.BlockSpec((tm, LANES), lambda i: (i, 0)),
                  pl.BlockSpec((tm, D_MODEL), lambda i: (i, 0)), pl.BlockSpec(memory_space=pl.ANY)],
        out_specs=pl.BlockSpec((tm, D_MODEL), lambda i: (i, 0)),
        out_shape=jax.ShapeDtypeStruct((t, D_MODEL), F32),
        scratch_shapes=[pltpu.VMEM((2, 2, tm, D_MODEL), F32), pltpu.SemaphoreType.DMA((2,))],
        compiler_params=_cp("arbitrary"),
        name="moe_combine",
    )(da, db, da, db, info, xn, ys)


def _moe(hf, gates, sel, rank, cnt, xn, wg, wu, wd, layer, tm):
    t = hf.shape[0]
    nrows = -(-(2 * t + N_EXPERTS * (MOE_TM - 1)) // MOE_TM) * MOE_TM
    ntile = nrows // MOE_TM
    cnt = cnt[0, :N_EXPERTS].astype(jnp.int32)
    padded = (cnt + (MOE_TM - 1)) // MOE_TM * MOE_TM
    end = jnp.cumsum(padded)
    off = jnp.zeros((1, LANES), F32).at[0, :N_EXPERTS].set((end - padded).astype(F32))
    nval = (end[-1] // MOE_TM).astype(jnp.int32)
    tile_start = jnp.arange(ntile, dtype=jnp.int32) * MOE_TM
    texp = jnp.sum((end[None, :] <= tile_start[:, None]).astype(jnp.int32), axis=1)
    texp = jnp.minimum(texp, N_EXPERTS - 1)
    texp = jnp.where(jnp.arange(ntile) < nval, texp, texp[jnp.maximum(nval - 1, 0)]) + layer * N_EXPERTS
    info = _moe_dest(sel, gates, rank, off, min(1024, t))
    da = info[:, 0].astype(jnp.int32)
    db = info[:, 1].astype(jnp.int32)
    tmd = min(512, t)
    xs = _moe_dispatch(hf, da.reshape(t // tmd, 1, tmd), db.reshape(t // tmd, 1, tmd), nrows, tmd)
    ys = _moe_ffn(xs, texp, nval.reshape(1), wg, wu, wd)
    return _moe_combine(xn, info, da.reshape(t // tm, 1, tm), db.reshape(t // tm, 1, tm), ys, tm)


def _pad_rows(a, nseq, dec):
    c = a.shape[-1]
    return jnp.pad(a.reshape(nseq, dec, c), ((0, 0), (0, SUB - dec), (0, 0))).reshape(nseq * SUB, c)


def _layer_params(l, ln_mix, w_in, hgrn_norm, fox_bf, fox_qnorm, fox_knorm, gdn_conv, gdn_a_log,
                  gdn_dt_bias, gdn_norm, w_out, ln_ffn, w_group, b_group, w_router, b_router,
                  w_gate, w_up, w_down):
    w = w_in[l]
    gates = jnp.concatenate([w[:, 2560:2568], w[:, 3592:3600]], axis=1)
    wp = jnp.concatenate([w[:, 0:2560], w[:, 2568:3592], gates,
                          jnp.zeros((D_MODEL, LANES - 16), F32)], axis=1).astype(BF16)
    p1 = jnp.zeros((1, LANES), F32).at[0, 0:8].set(fox_bf[l]).at[0, 8:12].set(gdn_dt_bias[l])
    p2 = jnp.zeros((1, LANES), F32).at[0, 8:12].set(gdn_a_log[l])
    wr = jnp.concatenate([w_router[l], w_group[l],
                          jnp.zeros((D_MODEL, LANES - N_GROUPS - N_EXPERTS), F32)], axis=1).astype(BF16)
    br = jnp.zeros((1, LANES), F32).at[0, 0:N_EXPERTS].set(b_router[l])
    br = br.at[0, N_EXPERTS:N_EXPERTS + N_GROUPS].set(b_group[l])
    return dict(
        ln_mix=ln_mix[l][None, :], w=wp, p1=p1, p2=p2,
        qg=jnp.tile(fox_qnorm[l], B_HEADS)[None, :], kg=jnp.tile(fox_knorm[l], B_HEADS)[None, :],
        hgrn_ng=jnp.tile(hgrn_norm[l], A_HEADS)[None, :], gdn_ng=jnp.tile(gdn_norm[l], C_HEADS)[None, :],
        conv=gdn_conv[l], w_out=w_out[l].astype(BF16), ln_ffn=ln_ffn[l][None, :], wr=wr, br=br)


def kernel(x_prompt, x_sample, cache_k, cache_v, cache_logf, page_table, state_hgrn, state_gdn, state_conv,
           ln_mix, w_in, hgrn_lb, hgrn_norm, fox_bf, fox_qnorm, fox_knorm, gdn_conv, gdn_a_log, gdn_dt_bias,
           gdn_norm, w_out, ln_ffn, w_group, b_group, w_router, b_router, w_gate, w_up, w_down):
    nb, seq, _ = x_prompt.shape
    nsq, dec, _ = x_sample.shape
    depth = ln_mix.shape[0]
    n_phys, page = cache_k.shape[1], cache_k.shape[2]
    tp = nb * seq
    ts = nsq * dec
    hs512 = _const(_head_sum_matrix(512))
    ck = jnp.transpose(cache_k, (0, 1, 3, 4, 2)).reshape(depth, n_phys, 512, page)
    cv = jnp.transpose(cache_v, (0, 1, 3, 4, 2)).reshape(depth, n_phys, 512, page)
    clt = jnp.swapaxes(cache_logf, 2, 3)
    tbp = min(256, seq)
    gchunk = min(64, seq)
    seq_blk = 16

    wg = w_gate.astype(BF16).reshape(depth * N_EXPERTS, D_MODEL, D_EXPERT)
    wu = w_up.astype(BF16).reshape(depth * N_EXPERTS, D_MODEL, D_EXPERT)
    wd = w_down.astype(BF16).reshape(depth * N_EXPERTS, D_EXPERT, D_MODEL)

    yp = x_prompt.reshape(tp, D_MODEL)
    ys = x_sample.reshape(ts, D_MODEL)
    outs_p, outs_s = [], []
    for l in range(depth):
        P = _layer_params(l, ln_mix, w_in, hgrn_norm, fox_bf, fox_qnorm, fox_knorm, gdn_conv, gdn_a_log,
                          gdn_dt_bias, gdn_norm, w_out, ln_ffn, w_group, b_group, w_router, b_router,
                          w_gate, w_up, w_down)

        a, bq, bk, bv, c, g = _proj(yp, P["ln_mix"], P["w"], hs512, P["qg"], P["kg"], P["p1"], P["p2"],
                                    tm=min(256, tp))
        qa, ka = _foxprep(bq, bk, g, nb, tm=min(256, seq))
        ob = _fox(qa, ka, bv, nb, tq=min(256, seq))
        oa, hst = _hgrn(a, hgrn_lb, P["hgrn_ng"], None, l, nb, seq // tbp, tbp, 1, tbp // SUB, SUB)
        oc, gst = _gdn(c, g, P["conv"], P["gdn_ng"], None, None, nb, seq // tbp, tbp, 1,
                       tbp // gchunk, gchunk)
        xn, hf, gates, sel, rank, cnt = _post(yp, oa, ob, oc, P["w_out"], P["ln_ffn"], P["wr"], P["br"],
                                              tm=min(256, tp))
        yp_new = _moe(hf, gates, sel, rank, cnt, xn, wg, wu, wd, l, tm=min(256, tp))
        outs_p.append((bk.reshape(nb, seq, B_HEADS, HD), bv.reshape(nb, seq, B_HEADS, HD),
                       g[:, 0:8].reshape(nb, seq, B_HEADS), hst, gst,
                       c.reshape(nb, seq, 1024)[:, seq - (CONV_W - 1):, 0:768]))
        yp = yp_new

        a, bq, bk, bv, c, g = _proj(ys, P["ln_mix"], P["w"], hs512, P["qg"], P["kg"], P["p1"], P["p2"],
                                    tm=min(256, ts))
        knew = jnp.pad(jnp.swapaxes(bk.reshape(nsq, dec, 512), 1, 2), ((0, 0), (0, 0), (0, SUB - dec)))
        vnew = jnp.pad(jnp.swapaxes(bv.reshape(nsq, dec, 512), 1, 2), ((0, 0), (0, 0), (0, SUB - dec)))
        lfn = jnp.pad(jnp.swapaxes(g[:, 0:8].reshape(nsq, dec, B_HEADS), 1, 2),
                      ((0, 0), (0, 0), (0, SUB - dec)))
        ob = _foxdec(l, page_table, bq.reshape(nsq, dec, 512), knew, vnew, lfn, ck, cv, clt,
                     pp=min(16, page_table.shape[1])).reshape(ts, 512)
        n_outer = nsq // seq_blk
        oa, hst = _hgrn(_pad_rows(a, nsq, dec), hgrn_lb, P["hgrn_ng"], state_hgrn[l], l,
                        n_outer, 1, seq_blk * SUB, seq_blk, 1, dec)
        cinit = jnp.pad(state_conv[l], ((0, 0), (8 - (CONV_W - 1), 0), (0, 0)))
        oc, gst = _gdn(_pad_rows(c, nsq, dec), _pad_rows(g, nsq, dec), P["conv"], P["gdn_ng"], cinit,
                       state_gdn[l], n_outer, 1, seq_blk * SUB, seq_blk, 1, SUB)
        oa = oa.reshape(nsq, SUB, 256)[:, :dec].reshape(ts, 256)
        oc = oc.reshape(nsq, SUB, 256)[:, :dec].reshape(ts, 256)
        xn, hf, gates, sel, rank, cnt = _post(ys, oa, ob, oc, P["w_out"], P["ln_ffn"], P["wr"], P["br"],
                                              tm=min(256, ts))
        ys_new = _moe(hf, gates, sel, rank, cnt, xn, wg, wu, wd, l, tm=min(256, ts))
        xpad = jnp.concatenate([state_conv[l], c.reshape(nsq, dec, 1024)[:, :, 0:768]], axis=1)
        outs_s.append((bk.reshape(nsq, dec, B_HEADS, HD), bv.reshape(nsq, dec, B_HEADS, HD),
                       g[:, 0:8].reshape(nsq, dec, B_HEADS), hst, gst, xpad[:, -(CONV_W - 1):, :]))
        ys = ys_new

    stack = lambda outs, j: jnp.stack([o[j] for o in outs], axis=0)
    return (yp.reshape(nb, seq, D_MODEL), ys.reshape(nsq, dec, D_MODEL),
            *[stack(outs_p, j) for j in range(6)], *[stack(outs_s, j) for j in range(6)])
```

```python
import functools
import math

import numpy as np
import jax
import jax.numpy as jnp
from jax import lax
from jax.experimental import pallas as pl
from jax.experimental.pallas import tpu as pltpu

F32 = jnp.float32
BF16 = jnp.bfloat16
EPS = 1e-6
NEG = -1e30

D_MODEL = 1024
HD = 64
A_HEADS = 4
B_HEADS = 8
C_HEADS = 4
CONV_W = 4
N_GROUPS = 4
EXPERTS_PER_GROUP = 8
N_EXPERTS = N_GROUPS * EXPERTS_PER_GROUP
D_EXPERT = D_MODEL // 4
SUB = 16
LANES = 128
VMEM_LIMIT = 56 * 1024 * 1024
ROW_TILE = 256
GDN_CHUNK = 64
SAMPLE_SEQS = 16
DECODE_PAGES = 16
MOE_DEST_TILE = 1024
MOE_DISPATCH_TILE = 512


def _cp(*sem):
    return pltpu.CompilerParams(dimension_semantics=sem, vmem_limit_bytes=VMEM_LIMIT)


def _dot(a, b):
    return jnp.dot(a, b, preferred_element_type=F32)


def _dot_nt(a, b):
    return lax.dot_general(a, b, (((1,), (1,)), ((), ())), preferred_element_type=F32)


def _dot_tn(a, b):
    return lax.dot_general(a, b, (((0,), (0,)), ((), ())), preferred_element_type=F32)


def _split3(x):
    hi = x.astype(BF16)
    r = x - hi.astype(F32)
    mid = r.astype(BF16)
    lo = (r - mid.astype(F32)).astype(BF16)
    return hi, mid, lo


def _split2(x):
    hi = x.astype(BF16)
    return hi, (x - hi.astype(F32)).astype(BF16)


def _exact_left(m, x):
    hi, mid, lo = _split3(x)
    return (_dot(m, hi) + _dot(m, mid)) + _dot(m, lo)


def _exact_right(x, m):
    hi, mid, lo = _split3(x)
    return (_dot(hi, m) + _dot(mid, m)) + _dot(lo, m)


def _idiv(x, n):
    return jnp.right_shift(x, int(math.log2(n)))


def _imod(x, n):
    return jnp.bitwise_and(x, n - 1)


def _sigmoid(x):
    return 1.0 / (1.0 + jnp.exp(-x))


def _softplus(z):
    return jnp.maximum(z, 0.0) + jnp.log(1.0 + jnp.exp(-jnp.abs(z)))


def _head_ssq(z, hs):
    zz = z * z
    hi = zz.astype(BF16)
    lo = (zz - hi.astype(F32)).astype(BF16)
    return _dot(hi, hs) + _dot(lo, hs)


def _const(a, dtype=BF16):
    return jnp.asarray(a, dtype=dtype)


def _head_sum_matrix(width):
    i = np.arange(width)
    return (i[:, None] // HD == i[None, :] // HD).astype(np.float32)


def _chunk_tril(n, c):
    i = np.arange(n)
    return ((i[:, None] // c == i[None, :] // c) & (i[None, :] <= i[:, None])).astype(np.float32)


PROJ_COLS = 3712


def _proj_kernel(x_ref, ln_ref, w_ref, hs_ref, qg_ref, kg_ref, p1_ref, p2_ref,
                 tril_ref, pq_ref, pcq_ref, pck_ref, oq_ref, ok_ref,
                 a_ref, bq_ref, bk_ref, bv_ref, c_ref, g_ref, qa_ref, ka_ref, carry_ref, *, bps):
    x = x_ref[...]
    ms = jnp.mean(x * x, axis=-1, keepdims=True)
    hn = ((x * lax.rsqrt(ms + EPS)) * ln_ref[...]).astype(BF16)
    a_ref[...] = _dot(hn, w_ref[:, 0:1024])
    hs = hs_ref[...]
    q = _dot(hn, w_ref[:, 1024:1536])
    bq = ((q * lax.rsqrt(_head_ssq(q, hs) * (1.0 / HD) + EPS)) * qg_ref[...]) * (HD ** -0.5)
    bq_ref[...] = bq
    k = _dot(hn, w_ref[:, 1536:2048])
    bk = (k * lax.rsqrt(_head_ssq(k, hs) * (1.0 / HD) + EPS)) * kg_ref[...]
    bk_ref[...] = bk
    bv_ref[...] = _dot(hn, w_ref[:, 2048:2560])
    c_ref[...] = _dot(hn, w_ref[:, 2560:3584])
    gr = _dot(hn, w_ref[:, 3584:3712])
    lane = lax.broadcasted_iota(jnp.int32, gr.shape, 1)
    z = gr + p1_ref[...]
    sp = _softplus(z)
    logsig = jnp.minimum(z, 0.0) - jnp.log(1.0 + jnp.exp(-jnp.abs(z)))
    glog = -jnp.exp(p2_ref[...]) * sp
    beta = _sigmoid(gr)
    g = jnp.where(lane < 8, logsig, jnp.where(lane < 12, glog, jnp.where(lane < 16, beta, 0.0)))
    g_ref[...] = g

    @pl.when(pl.program_id(0) % bps == 0)
    def _():
        carry_ref[...] = jnp.zeros_like(carry_ref)

    c = _exact_left(tril_ref[...], g) + carry_ref[...]
    carry_ref[...] = c[-1:, :]
    hi, mid, lo = _split3(c)
    pq = pq_ref[...]
    qa = _dot(bq.astype(BF16), pq) + oq_ref[...]
    ka = _dot(bk.astype(BF16), pq) + ok_ref[...]
    for j, part in enumerate((hi, mid, lo)):
        qa = qa + _dot(part, pcq_ref[j])
        ka = ka - _dot(part, pck_ref[j])
    qa_ref[...] = qa.astype(BF16)
    ka_ref[...] = ka.astype(BF16)


def _proj(x, ln, w, hs512, qg, kg, p1, p2, tm, bps):
    t = x.shape[0]
    pq = np.zeros((512, 1024), np.float32)
    pcq = np.zeros((3, LANES, 1024), np.float32)
    pck = np.zeros((3, LANES, 1024), np.float32)
    oq = np.zeros((1, 1024), np.float32)
    ok = np.zeros((1, 1024), np.float32)
    for h in range(B_HEADS):
        pq[h * HD + np.arange(HD), h * LANES + np.arange(HD)] = 1.0
        for j in range(3):
            pcq[j, h, h * LANES + HD + j] = 1.0
            pck[j, h, h * LANES + HD + 3 + j] = 1.0
            oq[0, h * LANES + HD + 3 + j] = 1.0
            ok[0, h * LANES + HD + j] = 1.0
    row = lambda i: (i, 0)
    fix = lambda i: (0, 0)
    fix3 = lambda i: (0, 0, 0)
    widths = (1024, 512, 512, 512, 1024, 128)
    outs = [jax.ShapeDtypeStruct((t, n), F32) for n in widths] + [jax.ShapeDtypeStruct((t, 1024), BF16)] * 2
    return pl.pallas_call(
        functools.partial(_proj_kernel, bps=bps),
        grid=(t // tm,),
        in_specs=[pl.BlockSpec((tm, D_MODEL), row), pl.BlockSpec((1, D_MODEL), fix),
                  pl.BlockSpec((D_MODEL, PROJ_COLS), fix), pl.BlockSpec((512, 512), fix),
                  pl.BlockSpec((1, 512), fix), pl.BlockSpec((1, 512), fix),
                  pl.BlockSpec((1, LANES), fix), pl.BlockSpec((1, LANES), fix),
                  pl.BlockSpec((tm, tm), fix), pl.BlockSpec((512, 1024), fix),
                  pl.BlockSpec((3, LANES, 1024), fix3), pl.BlockSpec((3, LANES, 1024), fix3),
                  pl.BlockSpec((1, 1024), fix), pl.BlockSpec((1, 1024), fix)],
        out_specs=[pl.BlockSpec((tm, n), row) for n in widths + (1024, 1024)],
        out_shape=outs,
        scratch_shapes=[pltpu.VMEM((1, LANES), F32)],
        compiler_params=_cp("arbitrary"),
        name="proj",
    )(x, ln, w, hs512, qg, kg, p1, p2, _const(_chunk_tril(tm, tm)), _const(pq), _const(pcq), _const(pck),
      _const(oq, F32), _const(ok, F32))


def _fox_kernel(qt_ref, kt_ref, q_ref, k_ref, v_ref, o_ref, m_ref, l_ref, acc_ref, *, tq):
    j = pl.program_id(1)
    qi = qt_ref[j]
    ki = kt_ref[j]

    @pl.when(ki == 0)
    def _():
        m_ref[...] = jnp.full_like(m_ref, NEG)
        l_ref[...] = jnp.zeros_like(l_ref)
        acc_ref[...] = jnp.zeros_like(acc_ref)

    heads = range(B_HEADS)
    lane = lax.broadcasted_iota(jnp.int32, (tq, LANES), 1)
    left = lane < HD

    def update(masked):
        if masked:
            rows = lax.broadcasted_iota(jnp.int32, (tq, tq), 0)
            cols = lax.broadcasted_iota(jnp.int32, (tq, tq), 1)
            keep = cols <= rows
        for pr in range(B_HEADS // 2):
            pair = (2 * pr, 2 * pr + 1)
            ss = [_dot_nt(q_ref[:, h * LANES:(h + 1) * LANES], k_ref[:, h * LANES:(h + 1) * LANES])
                  for h in pair]
            if masked:
                ss = [jnp.where(keep, s, NEG) for s in ss]
            ps, alphas = [], []
            for s, h in zip(ss, pair):
                m_old = m_ref[h]
                m_new = jnp.maximum(m_old, jnp.max(s, axis=1, keepdims=True))
                p = jnp.exp(s - m_new[:, 0:1])
                alpha = jnp.exp(m_old - m_new)
                l_ref[h] = alpha * l_ref[h] + jnp.sum(p, axis=1, keepdims=True)
                m_ref[h] = m_new
                ps.append(p.astype(BF16))
                alphas.append(alpha)
            vp = v_ref[:, pr * LANES:(pr + 1) * LANES]
            upd = _dot(ps[0], jnp.where(left, vp, 0.0).astype(BF16))
            upd = upd + _dot(ps[1], jnp.where(left, 0.0, vp).astype(BF16))
            a = jnp.where(left, alphas[0], alphas[1])
            acc_ref[:, pr * LANES:(pr + 1) * LANES] = a * acc_ref[:, pr * LANES:(pr + 1) * LANES] + upd

    @pl.when(ki < qi)
    def _():
        update(False)

    @pl.when(ki == qi)
    def _():
        update(True)
        for pr in range(B_HEADS // 2):
            l = jnp.where(left, l_ref[2 * pr], l_ref[2 * pr + 1])
            o_ref[:, pr * LANES:(pr + 1) * LANES] = acc_ref[:, pr * LANES:(pr + 1) * LANES] / l


def _fox(qa, ka, bv, nb, tq):
    t = qa.shape[0]
    nblk = t // nb // tq
    qi_tab = np.array([q for q in range(nblk) for _ in range(q + 1)], np.int32)
    ki_tab = np.array([k for q in range(nblk) for k in range(q + 1)], np.int32)
    grid_spec = pltpu.PrefetchScalarGridSpec(
        num_scalar_prefetch=2,
        grid=(nb, len(qi_tab)),
        in_specs=[pl.BlockSpec((tq, 1024), lambda b, j, qt, kt: (b * nblk + qt[j], 0)),
                  pl.BlockSpec((tq, 1024), lambda b, j, qt, kt: (b * nblk + kt[j], 0)),
                  pl.BlockSpec((tq, 512), lambda b, j, qt, kt: (b * nblk + kt[j], 0))],
        out_specs=pl.BlockSpec((tq, 512), lambda b, j, qt, kt: (b * nblk + qt[j], 0)),
        scratch_shapes=[pltpu.VMEM((B_HEADS, tq, LANES), F32), pltpu.VMEM((B_HEADS, tq, LANES), F32),
                        pltpu.VMEM((tq, 512), F32)],
    )
    return pl.pallas_call(
        functools.partial(_fox_kernel, tq=tq),
        grid_spec=grid_spec,
        out_shape=jax.ShapeDtypeStruct((t, 512), F32),
        compiler_params=_cp("arbitrary", "arbitrary"),
        name="fox",
    )(jnp.asarray(qi_tab), jnp.asarray(ki_tab), qa, ka, bv)


def _foxdec_kernel(pt_ref, q_ref, kn_ref, vn_ref, lfn_ref, cum_ref, wide_ref, *rest, pp, dec):
    kp = rest[0:pp]
    vp = rest[pp:2 * pp]
    lp = rest[2 * pp:3 * pp]
    o_ref = rest[3 * pp]
    qr_ref, m_ref, l_ref, r_ref, acc_ref = rest[3 * pp + 1:]
    step = pl.program_id(1)
    nrow = dec * B_HEADS
    rowi = lax.broadcasted_iota(jnp.int32, (nrow, 512), 0)
    coli = lax.broadcasted_iota(jnp.int32, (nrow, 512), 1)
    bd = _idiv(coli, HD) == _imod(rowi, B_HEADS)

    @pl.when(step == 0)
    def _():
        q = q_ref[0]
        qrows = jnp.concatenate(
            [jnp.broadcast_to(q[t:t + 1, :], (B_HEADS, 512)) for t in range(dec)], axis=0)
        qr_ref[...] = jnp.where(bd, qrows, 0.0).astype(BF16)
        m_ref[...] = jnp.full_like(m_ref, NEG)
        l_ref[...] = jnp.zeros_like(l_ref)
        r_ref[...] = jnp.zeros_like(r_ref)
        acc_ref[...] = jnp.zeros_like(acc_ref)

    def pages(krefs, vrefs, lrefs, causal):
        n = len(krefs)
        qr = qr_ref[...]
        ss = [_dot(qr, kr().astype(BF16)) for kr in krefs]
        cums = [_exact_right(lr(), cum_ref[...]) for lr in lrefs]
        r = r_ref[...]
        logits = []
        for u in range(n):
            w = jnp.concatenate([cums[u][:, 0:LANES]] * dec, axis=0)
            logit = ss[u] - (r + w)
            if causal:
                key = lax.broadcasted_iota(jnp.int32, (nrow, LANES), 1)
                trow = _idiv(lax.broadcasted_iota(jnp.int32, (nrow, LANES), 0), B_HEADS)
                logit = jnp.where(key <= trow, logit, NEG)
            logits.append(logit)
            r = r + jnp.concatenate([cums[u][:, LANES:2 * LANES]] * dec, axis=0)
        r_ref[...] = r
        m_old = m_ref[...]
        m_new = m_old
        for lg in logits:
            m_new = jnp.maximum(m_new, jnp.max(lg, axis=1, keepdims=True))
        ps = [jnp.exp(lg - m_new) for lg in logits]
        alpha = jnp.exp(m_old - m_new)
        lsum = jnp.sum(ps[0], axis=1, keepdims=True)
        for p in ps[1:]:
            lsum = lsum + jnp.sum(p, axis=1, keepdims=True)
        l_ref[...] = alpha * l_ref[...] + lsum
        m_ref[...] = m_new
        pv = _dot_nt(ps[0].astype(BF16), vrefs[0]().astype(BF16))
        for u in range(1, n):
            pv = pv + _dot_nt(ps[u].astype(BF16), vrefs[u]().astype(BF16))
        acc_ref[...] = alpha[:, 0:1] * acc_ref[...] + pv

    pages([lambda u=u: kp[u][...] for u in range(pp)], [lambda u=u: vp[u][...] for u in range(pp)],
          [lambda u=u: lp[u][...] for u in range(pp)], False)

    @pl.when(step == pl.num_programs(1) - 1)
    def _():
        wide = wide_ref[...]
        knew = _dot(kn_ref[0].astype(BF16), wide)
        vnew = _dot(vn_ref[0].astype(BF16), wide)
        lfnew = _exact_right(lfn_ref[0], wide)
        pages([lambda: knew], [lambda: vnew], [lambda: lfnew], True)
        res = jnp.where(bd, acc_ref[...] / l_ref[:, 0:1], 0.0)
        for t in range(dec):
            o_ref[0, t:t + 1, :] = jnp.sum(res[t * B_HEADS:(t + 1) * B_HEADS], axis=0, keepdims=True)


def _foxdec(layer, page_table, q, knew, vnew, lfn_t, cache_k, cache_v, cache_lft, pp):
    ns, dec, _ = q.shape
    n_pages = page_table.shape[1]
    page = cache_k.shape[3]
    nsteps = n_pages // pp
    nrow = dec * B_HEADS
    cum = np.concatenate([np.triu(np.ones((page, page), np.float32)), np.ones((page, page), np.float32)], axis=1)
    seq3 = lambda s, p, pt: (s, 0, 0)
    fix2 = lambda s, p, pt: (0, 0)

    def cache_map(u):
        return lambda s, p, pt: (layer, pt[s, p * pp + u], 0, 0)

    wide = np.eye(SUB, page, dtype=np.float32)
    in_specs = [pl.BlockSpec((1, dec, 512), seq3), pl.BlockSpec((1, 512, SUB), seq3),
                pl.BlockSpec((1, 512, SUB), seq3), pl.BlockSpec((1, B_HEADS, SUB), seq3),
                pl.BlockSpec((page, 2 * page), fix2), pl.BlockSpec((SUB, page), fix2)]
    in_specs += [pl.BlockSpec((None, None, 512, page), cache_map(u)) for u in range(pp)]
    in_specs += [pl.BlockSpec((None, None, 512, page), cache_map(u)) for u in range(pp)]
    in_specs += [pl.BlockSpec((None, None, B_HEADS, page), cache_map(u)) for u in range(pp)]
    grid_spec = pltpu.PrefetchScalarGridSpec(
        num_scalar_prefetch=1,
        grid=(ns, nsteps),
        in_specs=in_specs,
        out_specs=pl.BlockSpec((1, dec, 512), seq3),
        scratch_shapes=[pltpu.VMEM((nrow, 512), BF16), pltpu.VMEM((nrow, LANES), F32),
                        pltpu.VMEM((nrow, LANES), F32), pltpu.VMEM((nrow, LANES), F32),
                        pltpu.VMEM((nrow, 512), F32)],
    )
    return pl.pallas_call(
        functools.partial(_foxdec_kernel, pp=pp, dec=dec),
        grid_spec=grid_spec,
        out_shape=jax.ShapeDtypeStruct((ns, dec, 512), F32),
        compiler_params=_cp("arbitrary", "arbitrary"),
        name="foxdec",
    )(page_table, q, knew, vnew, lfn_t, _const(cum), _const(wide),
      *([cache_k] * pp), *([cache_v] * pp), *([cache_lft] * pp))


def _hgrn_kernel(*refs, layer, ns, cps, valid, has_init):
    (q_ref, f_ref, v_ref, gate_ref, lbraw_ref, ng_ref, tril_ref, rs_ref, tile_ref, hs_ref) = refs[:10]
    pos = 10
    s0_ref = None
    if has_init:
        s0_ref = refs[pos]
        pos += 1
    o_ref, sout_ref, st_ref, oin_ref, oint_ref, tmp_ref = refs[pos:pos + 6]
    i = pl.program_id(1)
    tb = q_ref.shape[0]
    nchunk = tb // SUB
    width = A_HEADS * HD

    lbraw = lbraw_ref[...]
    e = jnp.exp(lbraw - jnp.max(lbraw, axis=0, keepdims=True))
    prob = e / jnp.sum(e, axis=0, keepdims=True)
    lb = jnp.zeros((1, width), F32)
    for d in range(1, layer + 1):
        lb = lb + prob[d:d + 1, :]

    fr = f_ref[...]
    qr = q_ref[...]
    vr = v_ref[...]
    logf = jnp.log(lb + (1.0 - lb) * _sigmoid(fr))
    key = (1.0 - lb) * _sigmoid(-fr)
    if valid < SUB:
        rowv = _imod(lax.broadcasted_iota(jnp.int32, (tb, width), 0), SUB) < valid
        logf = jnp.where(rowv, logf, 0.0)
        key = jnp.where(rowv, key, 0.0)
    q = (qr * _sigmoid(qr)) * (HD ** -0.5)
    g = _exact_left(tril_ref[...], logf)
    g3 = g.reshape(nchunk, SUB, width)
    q3 = q.reshape(nchunk, SUB, width)
    k3 = key.reshape(nchunk, SUB, width)

    att = jnp.zeros((tb, LANES), F32)
    for s in range(SUB):
        d = jnp.minimum(g3 - g3[:, s:s + 1, :], 0.0)
        p = (q3 * jnp.exp(d)) * k3[:, s:s + 1, :]
        att = att + _dot(p.reshape(tb, width).astype(BF16), rs_ref[s])
    attb = att.astype(BF16)

    rows = lax.broadcasted_iota(jnp.int32, (tb, tb), 0)
    cols = lax.broadcasted_iota(jnp.int32, (tb, tb), 1)
    causal = (_idiv(rows, SUB) == _idiv(cols, SUB)) & (cols <= rows)
    lane = lax.broadcasted_iota(jnp.int32, (tb, LANES), 1)
    for pr in range(2):
        vpair = vr[:, pr * LANES:(pr + 1) * LANES]
        acc = None
        for hh in range(2):
            h = 2 * pr + hh
            full = _dot(attb, tile_ref[h])
            full = jnp.where(causal, full, 0.0).astype(BF16)
            vm = jnp.where((lane < HD) == (hh == 0), vpair, 0.0).astype(BF16)
            term = _dot(full, vm)
            acc = term if acc is None else acc + term
        oin_ref[:, pr * LANES:(pr + 1) * LANES] = acc

    gend3 = g3[:, SUB - 1:SUB, :]
    kd, kdl = _split2((k3 * jnp.exp(gend3 - g3)).reshape(tb, width))
    eg = jnp.exp(g)
    qg = (q * eg).astype(BF16)
    r128 = lax.broadcasted_iota(jnp.int32, (LANES, LANES), 0)
    c128 = lax.broadcasted_iota(jnp.int32, (LANES, LANES), 1)
    bdm = _idiv(r128, HD) == _idiv(c128, HD)
    vb, vbl = _split2(vr)
    last = i == pl.num_programs(1) - 1
    units = [(sq, pr) for sq in range(ns) for pr in range(2)]
    rowsl = lambda sq, c: slice((sq * cps + c) * SUB, (sq * cps + c + 1) * SUB)
    lanesl = lambda pr: slice(pr * LANES, (pr + 1) * LANES)
    incs = {}
    for sq, pr in units:
        for c in range(cps):
            rs_, ls_ = rowsl(sq, c), lanesl(pr)
            u = (_dot_tn(vb[rs_, ls_], kd[rs_, ls_]) + _dot_tn(vb[rs_, ls_], kdl[rs_, ls_])) \
                + _dot_tn(vbl[rs_, ls_], kd[rs_, ls_])
            incs[sq, pr, c] = jnp.where(bdm, u, 0.0)
    if has_init:
        tmp_ref[...] = jnp.zeros_like(tmp_ref)
        for sq, pr in units:
            tmp_ref[2 * sq + pr, 0:HD, 0:HD] = s0_ref[sq, 2 * pr]
            tmp_ref[2 * sq + pr, HD:LANES, HD:LANES] = s0_ref[sq, 2 * pr + 1]
    seen = {}
    final = {}
    for sq, pr in units:
        if has_init:
            st = tmp_ref[2 * sq + pr].T
        else:
            st = jnp.where(i == 0, 0.0, st_ref[pr])
        for c in range(cps):
            seen[sq, pr, c] = st.astype(BF16)
            r_end = (sq * cps + c + 1) * SUB
            st = st * eg[r_end - 1:r_end, lanesl(pr)] + incs[sq, pr, c]
        if not has_init:
            st_ref[pr] = st
        final[sq, pr] = st
    for sq, pr in units:
        for c in range(cps):
            oint_ref[rowsl(sq, c), lanesl(pr)] = _dot_nt(qg[rowsl(sq, c), lanesl(pr)], seen[sq, pr, c])

    @pl.when(last)
    def _():
        for sq, pr in units:
            tmp_ref[2 * sq + pr] = final[sq, pr].T
        for sq, pr in units:
            sout_ref[sq, 2 * pr] = tmp_ref[2 * sq + pr, 0:HD, 0:HD]
            sout_ref[sq, 2 * pr + 1] = tmp_ref[2 * sq + pr, HD:LANES, HD:LANES]

    o = oin_ref[...] + oint_ref[...]
    gr = gate_ref[...]
    o = (o * lax.rsqrt(_head_ssq(o, hs_ref[...]) * (1.0 / HD) + EPS)) * ng_ref[...]
    o_ref[...] = o * (gr * _sigmoid(gr))


def _hgrn(a, lbraw, ng, s0, layer, n_outer, n_inner, tb, ns, cps, valid):
    rows = a.shape[0]
    width = A_HEADS * HD
    has_init = s0 is not None
    nseq = n_outer * ns
    rs = np.zeros((SUB, width, LANES), np.float32)
    tile = np.zeros((A_HEADS, LANES, tb), np.float32)
    for s in range(SUB):
        for h in range(A_HEADS):
            rs[s, h * HD + np.arange(HD), h * SUB + s] = 1.0
            tile[h, h * SUB + s, np.arange(tb // SUB) * SUB + s] = 1.0

    def col(j):
        return lambda o, i: (o * n_inner + i, j)

    fix2 = lambda o, i: (0, 0)
    fix3 = lambda o, i: (0, 0, 0)
    in_specs = [pl.BlockSpec((tb, width), col(j)) for j in range(4)]
    in_specs += [pl.BlockSpec(lbraw.shape, fix2), pl.BlockSpec((1, width), fix2),
                 pl.BlockSpec((tb, tb), fix2), pl.BlockSpec((SUB, width, LANES), fix3),
                 pl.BlockSpec((A_HEADS, LANES, tb), fix3), pl.BlockSpec((width, width), fix2)]
    args = [a, a, a, a, lbraw, ng, _const(_chunk_tril(tb, SUB)), _const(rs), _const(tile),
            _const(_head_sum_matrix(width))]
    if has_init:
        in_specs.append(pl.BlockSpec((ns, A_HEADS, HD, HD), lambda o, i: (o, 0, 0, 0)))
        args.append(s0)
    return pl.pallas_call(
        functools.partial(_hgrn_kernel, layer=layer, ns=ns, cps=cps, valid=valid, has_init=has_init),
        grid=(n_outer, n_inner),
        in_specs=in_specs,
        out_specs=[pl.BlockSpec((tb, width), lambda o, i: (o * n_inner + i, 0)),
                   pl.BlockSpec((ns, A_HEADS, HD, HD), lambda o, i: (o, 0, 0, 0))],
        out_shape=[jax.ShapeDtypeStruct((rows, width), F32),
                   jax.ShapeDtypeStruct((nseq, A_HEADS, HD, HD), F32)],
        scratch_shapes=[pltpu.VMEM((2, LANES, LANES), F32), pltpu.VMEM((tb, width), F32),
                        pltpu.VMEM((tb, width), F32), pltpu.VMEM((2 * ns, LANES, LANES), F32)],
        compiler_params=_cp("arbitrary", "arbitrary"),
        name="hgrn",
    )(*args)


def _gdn_kernel(*refs, ns, cps, chunk, has_init):
    (x_ref, z_ref, g_ref, cw_ref, ng_ref, tril_ref, hs_ref, eg_ref, eb_ref, eg2_ref, eb2_ref) = refs[:11]
    pos = 11
    cinit_ref = s0_ref = None
    if has_init:
        cinit_ref, s0_ref = refs[pos:pos + 2]
        pos += 2
    o_ref, sout_ref, st_ref, cv_ref, oacc_ref, vn_ref, tmp_ref = refs[pos:pos + 7]
    i = pl.program_id(1)
    tb = x_ref.shape[0]
    width = C_HEADS * HD
    rows_seq = cps * chunk
    last = i == pl.num_programs(1) - 1

    x = x_ref[...]
    cw = cw_ref[...]
    acts = []
    for sq in range(ns):
        xs = x[sq * rows_seq:(sq + 1) * rows_seq]
        if has_init:
            prev = cinit_ref[sq]
        else:
            prev = jnp.where(i == 0, 0.0, cv_ref[...])
        xc = jnp.concatenate([prev, xs], axis=0)
        conv = xc * cw[CONV_W - 1:CONV_W, :]
        for j in range(1, CONV_W):
            conv = conv + pltpu.roll(xc, j, 0) * cw[CONV_W - 1 - j:CONV_W - j, :]
        acts.append(conv[8:])
        if not has_init:
            cv_ref[...] = xs[rows_seq - 8:]
    conv = acts[0] if ns == 1 else jnp.concatenate(acts, axis=0)
    act = conv * _sigmoid(conv)
    hs = hs_ref[...]
    q = act[:, 0:width]
    k = act[:, width:2 * width]
    v = act[:, 2 * width:3 * width]
    q = (q * lax.rsqrt(_head_ssq(q, hs) + EPS)) * (HD ** -0.5)
    k = k * lax.rsqrt(_head_ssq(k, hs) + EPS)

    gt = g_ref[...]
    gc = _exact_left(tril_ref[...], gt)
    gexp = _exact_right(gc, eg_ref[...])
    bexp = _exact_right(gt, eb_ref[...])
    gcol = _exact_right(gc, eg2_ref[...])
    bcol = _exact_right(gt, eb2_ref[...])
    nchunk = tb // chunk
    gexp3 = gexp.reshape(nchunk, chunk, width)
    gend3 = gexp3[:, chunk - 1:chunk, :]
    eg = jnp.exp(gexp)
    rhs = jnp.concatenate([v * bexp, (k * bexp) * eg], axis=1).astype(BF16)
    qg = (q * eg).astype(BF16)
    kd = (k.reshape(nchunk, chunk, width) * jnp.exp(gend3 - gexp3)).reshape(tb, width).astype(BF16)
    kb = k.astype(BF16)
    lane = lax.broadcasted_iota(jnp.int32, (tb, width), 1)
    kmask = [jnp.where(_idiv(lane, HD) == h, k, 0.0).astype(BF16) for h in range(C_HEADS)]
    qmask = [jnp.where(_idiv(lane, HD) == h, q, 0.0).astype(BF16) for h in range(C_HEADS)]

    rc = lax.broadcasted_iota(jnp.int32, (tb, tb), 0)
    cc = lax.broadcasted_iota(jnp.int32, (tb, tb), 1)
    same = _idiv(rc, chunk) == _idiv(cc, chunk)
    eye = rc == cc
    incl = same & (cc <= rc)
    strict = same & (cc < rc)
    hl = _idiv(lane, HD)
    r256 = lax.broadcasted_iota(jnp.int32, (width, width), 0)
    c256 = lax.broadcasted_iota(jnp.int32, (width, width), 1)
    bdm = _idiv(r256, HD) == _idiv(c256, HD)
    nstage = int(math.log2(chunk))
    heads = range(C_HEADS)
    reps = tb // LANES
    gct = gc.T
    decay, xm, tm, qkd = [], [], [], []
    for h in heads:
        g_t = jnp.concatenate([gcol[:, h * LANES:(h + 1) * LANES]] * reps, axis=1)
        decay.append(jnp.exp(jnp.minimum(g_t - gct[8 + h:9 + h, :], 0.0)))
    kk = [_dot_nt(kmask[h], kb) for h in heads]
    qk = [_dot_nt(qmask[h], kb) for h in heads]
    for h in heads:
        b_t = jnp.concatenate([bcol[:, h * LANES:(h + 1) * LANES]] * reps, axis=1)
        x0 = -jnp.where(strict, (b_t * kk[h]) * decay[h], 0.0)
        xm.append(x0)
        tm.append(jnp.where(eye, 1.0, 0.0) + x0)
        qkd.append(jnp.where(incl, qk[h] * decay[h], 0.0).astype(BF16))
    for _ in range(nstage - 1):
        xb = [x.astype(BF16) for x in xm]
        xm = [_dot(xb[h], xb[h]) for h in heads]
        tm = [tm[h] + _dot(tm[h].astype(BF16), xm[h].astype(BF16)) for h in heads]
    sol = [_dot(tm[h].astype(BF16), rhs) for h in heads]
    u = jnp.zeros((tb, width), F32)
    w = jnp.zeros((tb, width), F32)
    for h in heads:
        u = u + jnp.where(hl == h, sol[h][:, 0:width], 0.0)
        w = w + jnp.where(hl == h, sol[h][:, width:2 * width], 0.0)
    wb = w.astype(BF16)

    if has_init:
        tmp_ref[...] = jnp.zeros_like(tmp_ref)
        for sq in range(ns):
            for h in heads:
                tmp_ref[sq, h * HD:(h + 1) * HD, h * HD:(h + 1) * HD] = s0_ref[sq, h]
        sts = [tmp_ref[sq] for sq in range(ns)]
    else:
        sts = [jnp.where(i == 0, 0.0, st_ref[...])]
    rowsl = lambda sq, c: slice((sq * cps + c) * chunk, (sq * cps + c + 1) * chunk)
    for c in range(cps):
        stb = [st.astype(BF16) for st in sts]
        ws = [_dot(wb[rowsl(sq, c)], stb[sq]) for sq in range(ns)]
        for sq in range(ns):
            oacc_ref[rowsl(sq, c), :] = _dot(qg[rowsl(sq, c)], stb[sq])
        vns = [(u[rowsl(sq, c)] - ws[sq]).astype(BF16) for sq in range(ns)]
        for sq in range(ns):
            vn_ref[rowsl(sq, c), :] = vns[sq]
        incs = [jnp.where(bdm, _dot_tn(kd[rowsl(sq, c)], vns[sq]), 0.0) for sq in range(ns)]
        sts = [sts[sq] * eg[(sq * cps + c + 1) * chunk - 1:(sq * cps + c + 1) * chunk, :] + incs[sq]
               for sq in range(ns)]
    if not has_init:
        st_ref[...] = sts[0]

    @pl.when(last)
    def _():
        for sq in range(ns):
            tmp_ref[sq] = sts[sq]
        for sq in range(ns):
            for h in heads:
                sout_ref[sq, h] = tmp_ref[sq, h * HD:(h + 1) * HD, h * HD:(h + 1) * HD]

    vnb = vn_ref[...]
    o = oacc_ref[...]
    for h in heads:
        o = o + jnp.where(hl == h, _dot(qkd[h], vnb), 0.0)
    z = z_ref[...]
    o = (o * lax.rsqrt(_head_ssq(o, hs) * (1.0 / HD) + EPS)) * ng_ref[...]
    o_ref[...] = o * (z * _sigmoid(z))


def _gdn(cfull, g, cw, ng, cinit, s0, n_outer, n_inner, tb, ns, cps, chunk):
    rows = cfull.shape[0]
    width = C_HEADS * HD
    has_init = s0 is not None
    nseq = n_outer * ns
    eg = np.zeros((LANES, width), np.float32)
    eb = np.zeros((LANES, width), np.float32)
    eg2 = np.zeros((LANES, C_HEADS * LANES), np.float32)
    eb2 = np.zeros((LANES, C_HEADS * LANES), np.float32)
    for h in range(C_HEADS):
        eg[8 + h, h * HD:(h + 1) * HD] = 1.0
        eb[12 + h, h * HD:(h + 1) * HD] = 1.0
        eg2[8 + h, h * LANES:(h + 1) * LANES] = 1.0
        eb2[12 + h, h * LANES:(h + 1) * LANES] = 1.0
    rowm = lambda o, i: (o * n_inner + i, 0)
    fix2 = lambda o, i: (0, 0)
    in_specs = [pl.BlockSpec((tb, 3 * width), rowm),
                pl.BlockSpec((tb, width), lambda o, i: (o * n_inner + i, 3)),
                pl.BlockSpec((tb, LANES), rowm), pl.BlockSpec((CONV_W, 3 * width), fix2),
                pl.BlockSpec((1, width), fix2), pl.BlockSpec((tb, tb), fix2),
                pl.BlockSpec((width, width), fix2), pl.BlockSpec((LANES, width), fix2),
                pl.BlockSpec((LANES, width), fix2), pl.BlockSpec((LANES, C_HEADS * LANES), fix2),
                pl.BlockSpec((LANES, C_HEADS * LANES), fix2)]
    args = [cfull, cfull, g, cw, ng, _const(_chunk_tril(tb, chunk)), _const(_head_sum_matrix(width)),
            _const(eg), _const(eb), _const(eg2), _const(eb2)]
    if has_init:
        in_specs += [pl.BlockSpec((ns, 8, 3 * width), lambda o, i: (o, 0, 0)),
                     pl.BlockSpec((ns, C_HEADS, HD, HD), lambda o, i: (o, 0, 0, 0))]
        args += [cinit, s0]
    return pl.pallas_call(
        functools.partial(_gdn_kernel, ns=ns, cps=cps, chunk=chunk, has_init=has_init),
        grid=(n_outer, n_inner),
        in_specs=in_specs,
        out_specs=[pl.BlockSpec((tb, width), rowm),
                   pl.BlockSpec((ns, C_HEADS, HD, HD), lambda o, i: (o, 0, 0, 0))],
        out_shape=[jax.ShapeDtypeStruct((rows, width), F32),
                   jax.ShapeDtypeStruct((nseq, C_HEADS, HD, HD), F32)],
        scratch_shapes=[pltpu.VMEM((width, width), F32), pltpu.VMEM((8, 3 * width), F32),
                        pltpu.VMEM((tb, width), F32), pltpu.VMEM((tb, width), BF16),
                        pltpu.VMEM((ns, width, width), F32)],
        compiler_params=_cp("arbitrary", "arbitrary"),
        name="gdn",
    )(*args)


def _post_kernel(x_ref, oa_ref, ob_ref, oc_ref, wo_ref, ln_ref, wr_ref, br_ref, tril_ref,
                 xn_ref, hf_ref, gate_ref, sel_ref, rank_ref, cnt_ref, carry_ref):
    x = x_ref[...]
    mix = _dot(oa_ref[...].astype(BF16), wo_ref[0:256, :])
    mix = mix + _dot(ob_ref[...].astype(BF16), wo_ref[256:768, :])
    mix = mix + _dot(oc_ref[...].astype(BF16), wo_ref[768:1024, :])
    xn = x + mix
    xn_ref[...] = xn
    ms = jnp.mean(xn * xn, axis=-1, keepdims=True)
    hf = (xn * lax.rsqrt(ms + EPS)) * ln_ref[...]
    hf_ref[...] = hf
    logits = _dot(hf.astype(BF16), wr_ref[...]) + br_ref[...]
    lane = lax.broadcasted_iota(jnp.int32, logits.shape, 1)
    big = jnp.int32(1 << 20)
    isg = (lane >= N_EXPERTS) & (lane < N_EXPERTS + N_GROUPS)
    gl = jnp.where(isg, logits, NEG)
    gm = jnp.max(gl, axis=1, keepdims=True)
    gidx = jnp.min(jnp.where(isg & (gl == gm), lane, big), axis=1, keepdims=True) - N_EXPERTS
    top_gp = 1.0 / jnp.sum(jnp.where(isg, jnp.exp(gl - gm), 0.0), axis=1, keepdims=True)
    ing = (lane < N_EXPERTS) & (_idiv(lane, EXPERTS_PER_GROUP) == gidx)
    el = jnp.where(ing, logits, NEG)
    em = jnp.max(el, axis=1, keepdims=True)
    ee = jnp.where(ing, jnp.exp(el - em), 0.0)
    prob = ee / jnp.sum(ee, axis=1, keepdims=True)
    p1 = jnp.max(prob, axis=1, keepdims=True)
    i1 = jnp.min(jnp.where(ing & (prob == p1), lane, big), axis=1, keepdims=True)
    rest = jnp.where(ing & (lane != i1), prob, -1.0)
    p2 = jnp.max(rest, axis=1, keepdims=True)
    i2 = jnp.min(jnp.where(ing & (lane != i1) & (rest == p2), lane, big), axis=1, keepdims=True)
    den = p1 + p2
    gate_ref[...] = jnp.where(lane == i1, (top_gp * p1) / den,
                              jnp.where(lane == i2, (top_gp * p2) / den, 0.0))
    sel = jnp.where((lane == i1) | (lane == i2), 1.0, 0.0)
    sel_ref[...] = sel

    @pl.when(pl.program_id(0) == 0)
    def _():
        carry_ref[...] = jnp.zeros_like(carry_ref)

    rank_ref[...] = _dot(tril_ref[...], sel.astype(BF16)) + carry_ref[...]
    carry_ref[...] = carry_ref[...] + jnp.sum(sel, axis=0, keepdims=True)
    cnt_ref[...] = carry_ref[...]


def _post(x, oa, ob, oc, wo, ln, wr, br, tm):
    t = x.shape[0]
    row = lambda i: (i, 0)
    fix = lambda i: (0, 0)
    return pl.pallas_call(
        _post_kernel,
        grid=(t // tm,),
        in_specs=[pl.BlockSpec((tm, D_MODEL), row), pl.BlockSpec((tm, 256), row),
                  pl.BlockSpec((tm, 512), row), pl.BlockSpec((tm, 256), row),
                  pl.BlockSpec((D_MODEL, D_MODEL), fix), pl.BlockSpec((1, D_MODEL), fix),
                  pl.BlockSpec((D_MODEL, LANES), fix), pl.BlockSpec((1, LANES), fix),
                  pl.BlockSpec((tm, tm), fix)],
        out_specs=[pl.BlockSpec((tm, D_MODEL), row), pl.BlockSpec((tm, D_MODEL), row),
                   pl.BlockSpec((tm, LANES), row), pl.BlockSpec((tm, LANES), row),
                   pl.BlockSpec((tm, LANES), row), pl.BlockSpec((1, LANES), fix)],
        out_shape=[jax.ShapeDtypeStruct((t, D_MODEL), F32), jax.ShapeDtypeStruct((t, D_MODEL), F32),
                   jax.ShapeDtypeStruct((t, LANES), F32), jax.ShapeDtypeStruct((t, LANES), F32),
                   jax.ShapeDtypeStruct((t, LANES), F32), jax.ShapeDtypeStruct((1, LANES), F32)],
        scratch_shapes=[pltpu.VMEM((1, LANES), F32)],
        compiler_params=_cp("arbitrary"),
        name="post",
    )(x, oa, ob, oc, wo, ln, wr, br, _const(np.tril(np.ones((tm, tm), np.float32), -1)))


MOE_TM = 256


def _moe_dest_kernel(sel_ref, gate_ref, rank_ref, off_ref, info_ref):
    on = sel_ref[...] > 0.0
    gates = gate_ref[...]
    dest = off_ref[...] + rank_ref[...]
    lane = lax.broadcasted_iota(jnp.int32, gates.shape, 1)
    la = jnp.min(jnp.where(on, lane, LANES), axis=1, keepdims=True)
    lb = jnp.max(jnp.where(on, lane, -1), axis=1, keepdims=True)
    pick = lambda v, l: jnp.sum(jnp.where(lane == l, v, 0.0), axis=1, keepdims=True)
    info_ref[...] = jnp.where(lane == 0, pick(dest, la),
                              jnp.where(lane == 1, pick(dest, lb),
                                        jnp.where(lane == 2, pick(gates, la),
                                                  jnp.where(lane == 3, pick(gates, lb), 0.0))))


def _moe_dest(sel, gates, rank, off, tm):
    t = sel.shape[0]
    row = lambda i: (i, 0)
    return pl.pallas_call(
        _moe_dest_kernel,
        grid=(t // tm,),
        in_specs=[pl.BlockSpec((tm, LANES), row)] * 3 + [pl.BlockSpec((1, LANES), lambda i: (0, 0))],
        out_specs=pl.BlockSpec((tm, LANES), row),
        out_shape=jax.ShapeDtypeStruct((t, LANES), F32),
        compiler_params=_cp("arbitrary"),
        name="moe_dest",
    )(sel, gates, rank, off)


def _row_copy(src_ref, src_row, dst_ref, dst_row, sem):
    return pltpu.make_async_copy(src_ref.at[pl.ds(src_row, 1), :], dst_ref.at[pl.ds(dst_row, 1), :], sem)


def _rows_wait(src_ref, dst_ref, nrows, sem):
    pltpu.make_async_copy(src_ref.at[pl.ds(0, nrows), :], dst_ref.at[pl.ds(0, nrows), :], sem).wait()


def _moe_dispatch_kernel(da_ref, db_ref, hf_ref, xs_in_ref, xs_ref, sem):
    del xs_in_ref
    tm = hf_ref.shape[0]

    def issue(r, carry):
        _row_copy(hf_ref, r, xs_ref, da_ref[0, 0, r], sem.at[0]).start()
        _row_copy(hf_ref, r, xs_ref, db_ref[0, 0, r], sem.at[0]).start()
        return carry

    lax.fori_loop(0, tm, issue, 0, unroll=4)
    _rows_wait(hf_ref, xs_ref, tm, sem.at[0])
    _rows_wait(hf_ref, xs_ref, tm, sem.at[0])


def _moe_dispatch(hf, da, db, nrows, tm):
    t = hf.shape[0]
    idx = pl.BlockSpec((1, 1, tm), lambda i: (i, 0, 0), memory_space=pltpu.SMEM)
    return pl.pallas_call(
        _moe_dispatch_kernel,
        grid=(t // tm,),
        in_specs=[idx, idx, pl.BlockSpec((tm, D_MODEL), lambda i: (i, 0)),
                  pl.BlockSpec(memory_space=pl.ANY)],
        out_specs=pl.BlockSpec(memory_space=pl.ANY),
        out_shape=jax.ShapeDtypeStruct((nrows, D_MODEL), F32),
        scratch_shapes=[pltpu.SemaphoreType.DMA((1,))],
        input_output_aliases={3: 0},
        compiler_params=_cp("arbitrary"),
        name="moe_dispatch",
    )(da, db, hf, jnp.zeros((nrows, D_MODEL), F32))


def _moe_ffn_kernel(texp_ref, nval_ref, x_ref, wg_ref, wu_ref, wd_ref, y_ref):
    del texp_ref
    i = pl.program_id(0)

    @pl.when(i < nval_ref[0])
    def _():
        x = x_ref[...].astype(BF16)
        gp = _dot(x, wg_ref[0])
        up = _dot(x, wu_ref[0])
        y_ref[...] = _dot(((gp * _sigmoid(gp)) * up).astype(BF16), wd_ref[0])

    @pl.when(i >= nval_ref[0])
    def _():
        y_ref[...] = jnp.zeros_like(y_ref)


def _moe_ffn(xs, texp, nval, wg, wu, wd):
    nrows = xs.shape[0]
    tile = lambda i, te, nv: (jnp.minimum(i, nv[0] - 1), 0)
    wmap = lambda i, te, nv: (te[i], 0, 0)
    grid_spec = pltpu.PrefetchScalarGridSpec(
        num_scalar_prefetch=2,
        grid=(nrows // MOE_TM,),
        in_specs=[pl.BlockSpec((MOE_TM, D_MODEL), tile),
                  pl.BlockSpec((1, D_MODEL, D_EXPERT), wmap), pl.BlockSpec((1, D_MODEL, D_EXPERT), wmap),
                  pl.BlockSpec((1, D_EXPERT, D_MODEL), wmap)],
        out_specs=pl.BlockSpec((MOE_TM, D_MODEL), lambda i, te, nv: (i, 0)),
    )
    return pl.pallas_call(
        _moe_ffn_kernel,
        grid_spec=grid_spec,
        out_shape=jax.ShapeDtypeStruct((nrows, D_MODEL), F32),
        compiler_params=_cp("arbitrary"),
        name="moe_ffn",
    )(texp, nval, xs, wg, wu, wd)


def _moe_combine_kernel(da_ref, db_ref, dan_ref, dbn_ref, info_ref, x_ref, ys_ref, o_ref, buf_ref, sem):
    tm = x_ref.shape[0]
    i = pl.program_id(0)
    n = pl.num_programs(0)
    slot = i % 2

    def gather(ia_ref, ib_ref, s):
        def issue(r, carry):
            _row_copy(ys_ref, ia_ref[0, 0, r], buf_ref.at[s, 0], r, sem.at[s]).start()
            _row_copy(ys_ref, ib_ref[0, 0, r], buf_ref.at[s, 1], r, sem.at[s]).start()
            return carry

        lax.fori_loop(0, tm, issue, 0, unroll=4)

    @pl.when(i == 0)
    def _():
        gather(da_ref, db_ref, 0)

    @pl.when(i + 1 < n)
    def _():
        gather(dan_ref, dbn_ref, 1 - slot)

    _rows_wait(ys_ref, buf_ref.at[slot, 0], tm, sem.at[slot])
    _rows_wait(ys_ref, buf_ref.at[slot, 1], tm, sem.at[slot])
    info = info_ref[...]
    o_ref[...] = (x_ref[...] + info[:, 2:3] * buf_ref[slot, 0]) + info[:, 3:4] * buf_ref[slot, 1]


def _moe_combine(xn, info, da, db, ys, tm):
    t = xn.shape[0]
    n = t // tm
    idx = pl.BlockSpec((1, 1, tm), lambda i: (i, 0, 0), memory_space=pltpu.SMEM)
    nxt = pl.BlockSpec((1, 1, tm), lambda i: (jnp.minimum(i + 1, n - 1), 0, 0), memory_space=pltpu.SMEM)
    return pl.pallas_call(
        _moe_combine_kernel,
        grid=(n,),
        in_specs=[idx, idx, nxt, nxt, pl.BlockSpec((tm, LANES), lambda i: (i, 0)),
                  pl.BlockSpec((tm, D_MODEL), lambda i: (i, 0)), pl.BlockSpec(memory_space=pl.ANY)],
        out_specs=pl.BlockSpec((tm, D_MODEL), lambda i: (i, 0)),
        out_shape=jax.ShapeDtypeStruct((t, D_MODEL), F32),
        scratch_shapes=[pltpu.VMEM((2, 2, tm, D_MODEL), F32), pltpu.SemaphoreType.DMA((2,))],
        compiler_params=_cp("arbitrary"),
        name="moe_combine",
    )(da, db, da, db, info, xn, ys)


def _moe(hf, gates, sel, rank, cnt, xn, wg, wu, wd, layer, tm):
    t = hf.shape[0]
    nrows = -(-(2 * t + N_EXPERTS * (MOE_TM - 1)) // MOE_TM) * MOE_TM
    ntile = nrows // MOE_TM
    cnt = cnt[0, :N_EXPERTS].astype(jnp.int32)
    padded = (cnt + (MOE_TM - 1)) // MOE_TM * MOE_TM
    end = jnp.cumsum(padded)
    off = jnp.zeros((1, LANES), F32).at[0, :N_EXPERTS].set((end - padded).astype(F32))
    nval = (end[-1] // MOE_TM).astype(jnp.int32)
    tile_start = jnp.arange(ntile, dtype=jnp.int32) * MOE_TM
    texp = jnp.sum((end[None, :] <= tile_start[:, None]).astype(jnp.int32), axis=1)
    texp = jnp.minimum(texp, N_EXPERTS - 1)
    texp = jnp.where(jnp.arange(ntile) < nval, texp, texp[jnp.maximum(nval - 1, 0)]) + layer * N_EXPERTS
    info = _moe_dest(sel, gates, rank, off, min(MOE_DEST_TILE, t))
    da = info[:, 0].astype(jnp.int32)
    db = info[:, 1].astype(jnp.int32)
    tmd = min(MOE_DISPATCH_TILE, t)
    xs = _moe_dispatch(hf, da.reshape(t // tmd, 1, tmd), db.reshape(t // tmd, 1, tmd), nrows, tmd)
    ys = _moe_ffn(xs, texp, nval.reshape(1), wg, wu, wd)
    return _moe_combine(xn, info, da.reshape(t // tm, 1, tm), db.reshape(t // tm, 1, tm), ys, tm)


def _pad_rows(a, nseq, dec):
    c = a.shape[-1]
    return jnp.pad(a.reshape(nseq, dec, c), ((0, 0), (0, SUB - dec), (0, 0))).reshape(nseq * SUB, c)


def _layer_params(l, ln_mix, w_in, hgrn_norm, fox_bf, fox_qnorm, fox_knorm, gdn_conv, gdn_a_log,
                  gdn_dt_bias, gdn_norm, w_out, ln_ffn, w_group, b_group, w_router, b_router,
                  w_gate, w_up, w_down):
    w = w_in[l]
    gates = jnp.concatenate([w[:, 2560:2568], w[:, 3592:3600]], axis=1)
    wp = jnp.concatenate([w[:, 0:2560], w[:, 2568:3592], gates,
                          jnp.zeros((D_MODEL, LANES - 16), F32)], axis=1).astype(BF16)
    p1 = jnp.zeros((1, LANES), F32).at[0, 0:8].set(fox_bf[l]).at[0, 8:12].set(gdn_dt_bias[l])
    p2 = jnp.zeros((1, LANES), F32).at[0, 8:12].set(gdn_a_log[l])
    wr = jnp.concatenate([w_router[l], w_group[l],
                          jnp.zeros((D_MODEL, LANES - N_GROUPS - N_EXPERTS), F32)], axis=1).astype(BF16)
    br = jnp.zeros((1, LANES), F32).at[0, 0:N_EXPERTS].set(b_router[l])
    br = br.at[0, N_EXPERTS:N_EXPERTS + N_GROUPS].set(b_group[l])
    return dict(
        ln_mix=ln_mix[l][None, :], w=wp, p1=p1, p2=p2,
        qg=jnp.tile(fox_qnorm[l], B_HEADS)[None, :], kg=jnp.tile(fox_knorm[l], B_HEADS)[None, :],
        hgrn_ng=jnp.tile(hgrn_norm[l], A_HEADS)[None, :], gdn_ng=jnp.tile(gdn_norm[l], C_HEADS)[None, :],
        conv=gdn_conv[l], w_out=w_out[l].astype(BF16), ln_ffn=ln_ffn[l][None, :], wr=wr, br=br)


def kernel(x_prompt, x_sample, cache_k, cache_v, cache_logf, page_table, state_hgrn, state_gdn, state_conv,
           ln_mix, w_in, hgrn_lb, hgrn_norm, fox_bf, fox_qnorm, fox_knorm, gdn_conv, gdn_a_log, gdn_dt_bias,
           gdn_norm, w_out, ln_ffn, w_group, b_group, w_router, b_router, w_gate, w_up, w_down):
    nb, seq, _ = x_prompt.shape
    nsq, dec, _ = x_sample.shape
    depth = ln_mix.shape[0]
    n_phys, page = cache_k.shape[1], cache_k.shape[2]
    tp = nb * seq
    ts = nsq * dec
    hs512 = _const(_head_sum_matrix(512))
    ck = jnp.transpose(cache_k, (0, 1, 3, 4, 2)).reshape(depth, n_phys, 512, page)
    cv = jnp.transpose(cache_v, (0, 1, 3, 4, 2)).reshape(depth, n_phys, 512, page)
    clt = jnp.swapaxes(cache_logf, 2, 3)
    tmp_ = min(ROW_TILE, tp)
    tms_ = min(ROW_TILE, ts)
    tbp = min(ROW_TILE, seq)
    gchunk = min(GDN_CHUNK, seq)
    seq_blk = SAMPLE_SEQS

    wg = w_gate.astype(BF16).reshape(depth * N_EXPERTS, D_MODEL, D_EXPERT)
    wu = w_up.astype(BF16).reshape(depth * N_EXPERTS, D_MODEL, D_EXPERT)
    wd = w_down.astype(BF16).reshape(depth * N_EXPERTS, D_EXPERT, D_MODEL)

    yp = x_prompt.reshape(tp, D_MODEL)
    ys = x_sample.reshape(ts, D_MODEL)
    outs_p, outs_s = [], []
    for l in range(depth):
        P = _layer_params(l, ln_mix, w_in, hgrn_norm, fox_bf, fox_qnorm, fox_knorm, gdn_conv, gdn_a_log,
                          gdn_dt_bias, gdn_norm, w_out, ln_ffn, w_group, b_group, w_router, b_router,
                          w_gate, w_up, w_down)

        a, _, bk, bv, c, g, qa, ka = _proj(yp, P["ln_mix"], P["w"], hs512, P["qg"], P["kg"], P["p1"],
                                           P["p2"], tm=tbp, bps=seq // tbp)
        ob = _fox(qa, ka, bv, nb, tq=tbp)
        oa, hst = _hgrn(a, hgrn_lb, P["hgrn_ng"], None, l, nb, seq // tbp, tbp, 1, tbp // SUB, SUB)
        oc, gst = _gdn(c, g, P["conv"], P["gdn_ng"], None, None, nb, seq // tbp, tbp, 1,
                       tbp // gchunk, gchunk)
        xn, hf, gates, sel, rank, cnt = _post(yp, oa, ob, oc, P["w_out"], P["ln_ffn"], P["wr"], P["br"],
                                              tm=tmp_)
        yp_new = _moe(hf, gates, sel, rank, cnt, xn, wg, wu, wd, l, tm=tmp_)
        outs_p.append((bk.reshape(nb, seq, B_HEADS, HD), bv.reshape(nb, seq, B_HEADS, HD),
                       g[:, 0:8].reshape(nb, seq, B_HEADS), hst, gst,
                       c.reshape(nb, seq, 1024)[:, seq - (CONV_W - 1):, 0:768]))
        yp = yp_new

        a, bq, bk, bv, c, g, _, _ = _proj(ys, P["ln_mix"], P["w"], hs512, P["qg"], P["kg"], P["p1"],
                                          P["p2"], tm=tms_, bps=1)
        knew = jnp.pad(jnp.swapaxes(bk.reshape(nsq, dec, 512), 1, 2), ((0, 0), (0, 0), (0, SUB - dec)))
        vnew = jnp.pad(jnp.swapaxes(bv.reshape(nsq, dec, 512), 1, 2), ((0, 0), (0, 0), (0, SUB - dec)))
        lfn = jnp.pad(jnp.swapaxes(g[:, 0:8].reshape(nsq, dec, B_HEADS), 1, 2),
                      ((0, 0), (0, 0), (0, SUB - dec)))
        ob = _foxdec(l, page_table, bq.reshape(nsq, dec, 512), knew, vnew, lfn, ck, cv, clt,
                     pp=min(DECODE_PAGES, page_table.shape[1])).reshape(ts, 512)
        n_outer = nsq // seq_blk
        oa, hst = _hgrn(_pad_rows(a, nsq, dec), hgrn_lb, P["hgrn_ng"], state_hgrn[l], l,
                        n_outer, 1, seq_blk * SUB, seq_blk, 1, dec)
        cinit = jnp.pad(state_conv[l], ((0, 0), (8 - (CONV_W - 1), 0), (0, 0)))
        oc, gst = _gdn(_pad_rows(c, nsq, dec), _pad_rows(g, nsq, dec), P["conv"], P["gdn_ng"], cinit,
                       state_gdn[l], n_outer, 1, seq_blk * SUB, seq_blk, 1, SUB)
        oa = oa.reshape(nsq, SUB, 256)[:, :dec].reshape(ts, 256)
        oc = oc.reshape(nsq, SUB, 256)[:, :dec].reshape(ts, 256)
        xn, hf, gates, sel, rank, cnt = _post(ys, oa, ob, oc, P["w_out"], P["ln_ffn"], P["wr"], P["br"],
                                              tm=tms_)
        ys_new = _moe(hf, gates, sel, rank, cnt, xn, wg, wu, wd, l, tm=tms_)
        xpad = jnp.concatenate([state_conv[l], c.reshape(nsq, dec, 1024)[:, :, 0:768]], axis=1)
        outs_s.append((bk.reshape(nsq, dec, B_HEADS, HD), bv.reshape(nsq, dec, B_HEADS, HD),
                       g[:, 0:8].reshape(nsq, dec, B_HEADS), hst, gst, xpad[:, -(CONV_W - 1):, :]))
        ys = ys_new

    stack = lambda outs, j: jnp.stack([o[j] for o in outs], axis=0)
    return (yp.reshape(nb, seq, D_MODEL), ys.reshape(nsq, dec, D_MODEL),
            *[stack(outs_p, j) for j in range(6)], *[stack(outs_s, j) for j in range(6)])
```

```python
import functools
import math

import numpy as np
import jax
import jax.numpy as jnp
from jax import lax
from jax.experimental import pallas as pl
from jax.experimental.pallas import tpu as pltpu

F32 = jnp.float32
BF16 = jnp.bfloat16
EPS = 1e-6
NEG = -1e30

D_MODEL = 1024
HD = 64
A_HEADS = 4
B_HEADS = 8
C_HEADS = 4
CONV_W = 4
N_GROUPS = 4
EXPERTS_PER_GROUP = 8
N_EXPERTS = N_GROUPS * EXPERTS_PER_GROUP
D_EXPERT = D_MODEL // 4
SUB = 16
LANES = 128
VMEM_LIMIT = 56 * 1024 * 1024
ROW_TILE = 256
GDN_CHUNK = 64
SAMPLE_SEQS = 16
DECODE_PAGES = 16
MOE_DEST_TILE = 1024
MOE_DISPATCH_TILE = 512


def _cp(*sem):
    return pltpu.CompilerParams(dimension_semantics=sem, vmem_limit_bytes=VMEM_LIMIT)


def _dot(a, b):
    return jnp.dot(a, b, preferred_element_type=F32)


def _dot_nt(a, b):
    return lax.dot_general(a, b, (((1,), (1,)), ((), ())), preferred_element_type=F32)


def _dot_tn(a, b):
    return lax.dot_general(a, b, (((0,), (0,)), ((), ())), preferred_element_type=F32)


def _split3(x):
    hi = x.astype(BF16)
    r = x - hi.astype(F32)
    mid = r.astype(BF16)
    lo = (r - mid.astype(F32)).astype(BF16)
    return hi, mid, lo


def _split2(x):
    hi = x.astype(BF16)
    return hi, (x - hi.astype(F32)).astype(BF16)


def _exact_left(m, x):
    hi, mid, lo = _split3(x)
    return (_dot(m, hi) + _dot(m, mid)) + _dot(m, lo)


def _exact_right(x, m):
    hi, mid, lo = _split3(x)
    return (_dot(hi, m) + _dot(mid, m)) + _dot(lo, m)


def _idiv(x, n):
    return jnp.right_shift(x, int(math.log2(n)))


def _imod(x, n):
    return jnp.bitwise_and(x, n - 1)


def _sigmoid(x):
    return 1.0 / (1.0 + jnp.exp(-x))


def _softplus(z):
    return jnp.maximum(z, 0.0) + jnp.log(1.0 + jnp.exp(-jnp.abs(z)))


def _head_ssq(z, hs):
    zz = z * z
    hi = zz.astype(BF16)
    lo = (zz - hi.astype(F32)).astype(BF16)
    return _dot(hi, hs) + _dot(lo, hs)


def _const(a, dtype=BF16):
    return jnp.asarray(a, dtype=dtype)


def _head_sum_matrix(width):
    i = np.arange(width)
    return (i[:, None] // HD == i[None, :] // HD).astype(np.float32)


def _chunk_tril(n, c):
    i = np.arange(n)
    return ((i[:, None] // c == i[None, :] // c) & (i[None, :] <= i[:, None])).astype(np.float32)


PROJ_COLS = 3712


def _proj_kernel(x_ref, ln_ref, w_ref, hs_ref, qg_ref, kg_ref, p1_ref, p2_ref,
                 tril_ref, pq_ref, pcq_ref, pck_ref, oq_ref, ok_ref,
                 a_ref, bq_ref, bk_ref, bv_ref, c_ref, g_ref, qa_ref, ka_ref, carry_ref, *, bps):
    x = x_ref[...]
    ms = jnp.mean(x * x, axis=-1, keepdims=True)
    hn = ((x * lax.rsqrt(ms + EPS)) * ln_ref[...]).astype(BF16)
    a_ref[...] = _dot(hn, w_ref[:, 0:1024])
    hs = hs_ref[...]
    q = _dot(hn, w_ref[:, 1024:1536])
    bq = ((q * lax.rsqrt(_head_ssq(q, hs) * (1.0 / HD) + EPS)) * qg_ref[...]) * (HD ** -0.5)
    bq_ref[...] = bq
    k = _dot(hn, w_ref[:, 1536:2048])
    bk = (k * lax.rsqrt(_head_ssq(k, hs) * (1.0 / HD) + EPS)) * kg_ref[...]
    bk_ref[...] = bk
    bv_ref[...] = _dot(hn, w_ref[:, 2048:2560])
    c_ref[...] = _dot(hn, w_ref[:, 2560:3584])
    gr = _dot(hn, w_ref[:, 3584:3712])
    lane = lax.broadcasted_iota(jnp.int32, gr.shape, 1)
    z = gr + p1_ref[...]
    sp = _softplus(z)
    logsig = jnp.minimum(z, 0.0) - jnp.log(1.0 + jnp.exp(-jnp.abs(z)))
    glog = -jnp.exp(p2_ref[...]) * sp
    beta = _sigmoid(gr)
    g = jnp.where(lane < 8, logsig, jnp.where(lane < 12, glog, jnp.where(lane < 16, beta, 0.0)))
    g_ref[...] = g

    @pl.when(pl.program_id(0) % bps == 0)
    def _():
        carry_ref[...] = jnp.zeros_like(carry_ref)

    c = _exact_left(tril_ref[...], g) + carry_ref[...]
    carry_ref[...] = c[-1:, :]
    hi, mid, lo = _split3(c)
    pq = pq_ref[...]
    qa = _dot(bq.astype(BF16), pq) + oq_ref[...]
    ka = _dot(bk.astype(BF16), pq) + ok_ref[...]
    for j, part in enumerate((hi, mid, lo)):
        qa = qa + _dot(part, pcq_ref[j])
        ka = ka - _dot(part, pck_ref[j])
    qa_ref[...] = qa.astype(BF16)
    ka_ref[...] = ka.astype(BF16)


def _proj(x, ln, w, hs512, qg, kg, p1, p2, tm, bps):
    t = x.shape[0]
    pq = np.zeros((512, 1024), np.float32)
    pcq = np.zeros((3, LANES, 1024), np.float32)
    pck = np.zeros((3, LANES, 1024), np.float32)
    oq = np.zeros((1, 1024), np.float32)
    ok = np.zeros((1, 1024), np.float32)
    for h in range(B_HEADS):
        pq[h * HD + np.arange(HD), h * LANES + np.arange(HD)] = 1.0
        for j in range(3):
            pcq[j, h, h * LANES + HD + j] = 1.0
            pck[j, h, h * LANES + HD + 3 + j] = 1.0
            oq[0, h * LANES + HD + 3 + j] = 1.0
            ok[0, h * LANES + HD + j] = 1.0
    row = lambda i: (i, 0)
    fix = lambda i: (0, 0)
    fix3 = lambda i: (0, 0, 0)
    widths = (1024, 512, 512, 512, 1024, 128)
    outs = [jax.ShapeDtypeStruct((t, n), F32) for n in widths] + [jax.ShapeDtypeStruct((t, 1024), BF16)] * 2
    return pl.pallas_call(
        functools.partial(_proj_kernel, bps=bps),
        grid=(t // tm,),
        in_specs=[pl.BlockSpec((tm, D_MODEL), row), pl.BlockSpec((1, D_MODEL), fix),
                  pl.BlockSpec((D_MODEL, PROJ_COLS), fix), pl.BlockSpec((512, 512), fix),
                  pl.BlockSpec((1, 512), fix), pl.BlockSpec((1, 512), fix),
                  pl.BlockSpec((1, LANES), fix), pl.BlockSpec((1, LANES), fix),
                  pl.BlockSpec((tm, tm), fix), pl.BlockSpec((512, 1024), fix),
                  pl.BlockSpec((3, LANES, 1024), fix3), pl.BlockSpec((3, LANES, 1024), fix3),
                  pl.BlockSpec((1, 1024), fix), pl.BlockSpec((1, 1024), fix)],
        out_specs=[pl.BlockSpec((tm, n), row) for n in widths + (1024, 1024)],
        out_shape=outs,
        scratch_shapes=[pltpu.VMEM((1, LANES), F32)],
        compiler_params=_cp("arbitrary"),
        name="proj",
    )(x, ln, w, hs512, qg, kg, p1, p2, _const(_chunk_tril(tm, tm)), _const(pq), _const(pcq), _const(pck),
      _const(oq, F32), _const(ok, F32))


def _fox_kernel(qt_ref, kt_ref, q_ref, k_ref, v_ref, o_ref, m_ref, l_ref, acc_ref, *, tq):
    j = pl.program_id(1)
    qi = qt_ref[j]
    ki = kt_ref[j]

    @pl.when(ki == 0)
    def _():
        m_ref[...] = jnp.full_like(m_ref, NEG)
        l_ref[...] = jnp.zeros_like(l_ref)
        acc_ref[...] = jnp.zeros_like(acc_ref)

    heads = range(B_HEADS)
    lane = lax.broadcasted_iota(jnp.int32, (tq, LANES), 1)
    left = lane < HD

    def update(masked):
        if masked:
            rows = lax.broadcasted_iota(jnp.int32, (tq, tq), 0)
            cols = lax.broadcasted_iota(jnp.int32, (tq, tq), 1)
            keep = cols <= rows
        for pr in range(B_HEADS // 2):
            pair = (2 * pr, 2 * pr + 1)
            ss = [_dot_nt(q_ref[:, h * LANES:(h + 1) * LANES], k_ref[:, h * LANES:(h + 1) * LANES])
                  for h in pair]
            if masked:
                ss = [jnp.where(keep, s, NEG) for s in ss]
            ps, alphas = [], []
            for s, h in zip(ss, pair):
                m_old = m_ref[h]
                m_new = jnp.maximum(m_old, jnp.max(s, axis=1, keepdims=True))
                p = jnp.exp(s - m_new[:, 0:1])
                alpha = jnp.exp(m_old - m_new)
                l_ref[h] = alpha * l_ref[h] + jnp.sum(p, axis=1, keepdims=True)
                m_ref[h] = m_new
                ps.append(p.astype(BF16))
                alphas.append(alpha)
            vp = v_ref[:, pr * LANES:(pr + 1) * LANES]
            upd = _dot(ps[0], jnp.where(left, vp, 0.0).astype(BF16))
            upd = upd + _dot(ps[1], jnp.where(left, 0.0, vp).astype(BF16))
            a = jnp.where(left, alphas[0], alphas[1])
            acc_ref[:, pr * LANES:(pr + 1) * LANES] = a * acc_ref[:, pr * LANES:(pr + 1) * LANES] + upd

    @pl.when(ki < qi)
    def _():
        update(False)

    @pl.when(ki == qi)
    def _():
        update(True)
        for pr in range(B_HEADS // 2):
            l = jnp.where(left, l_ref[2 * pr], l_ref[2 * pr + 1])
            o_ref[:, pr * LANES:(pr + 1) * LANES] = acc_ref[:, pr * LANES:(pr + 1) * LANES] / l


def _fox(qa, ka, bv, nb, tq):
    t = qa.shape[0]
    nblk = t // nb // tq
    qi_tab = np.array([q for q in range(nblk) for _ in range(q + 1)], np.int32)
    ki_tab = np.array([k for q in range(nblk) for k in range(q + 1)], np.int32)
    grid_spec = pltpu.PrefetchScalarGridSpec(
        num_scalar_prefetch=2,
        grid=(nb, len(qi_tab)),
        in_specs=[pl.BlockSpec((tq, 1024), lambda b, j, qt, kt: (b * nblk + qt[j], 0)),
                  pl.BlockSpec((tq, 1024), lambda b, j, qt, kt: (b * nblk + kt[j], 0)),
                  pl.BlockSpec((tq, 512), lambda b, j, qt, kt: (b * nblk + kt[j], 0))],
        out_specs=pl.BlockSpec((tq, 512), lambda b, j, qt, kt: (b * nblk + qt[j], 0)),
        scratch_shapes=[pltpu.VMEM((B_HEADS, tq, LANES), F32), pltpu.VMEM((B_HEADS, tq, LANES), F32),
                        pltpu.VMEM((tq, 512), F32)],
    )
    return pl.pallas_call(
        functools.partial(_fox_kernel, tq=tq),
        grid_spec=grid_spec,
        out_shape=jax.ShapeDtypeStruct((t, 512), F32),
        compiler_params=_cp("arbitrary", "arbitrary"),
        name="fox",
    )(jnp.asarray(qi_tab), jnp.asarray(ki_tab), qa, ka, bv)


def _foxdec_kernel(pt_ref, q_ref, kn_ref, vn_ref, lfn_ref, cum_ref, wide_ref, *rest, pp, dec):
    kp = rest[0:pp]
    vp = rest[pp:2 * pp]
    lp = rest[2 * pp:3 * pp]
    o_ref = rest[3 * pp]
    qr_ref, m_ref, l_ref, r_ref, acc_ref = rest[3 * pp + 1:]
    step = pl.program_id(1)
    nrow = dec * B_HEADS
    rowi = lax.broadcasted_iota(jnp.int32, (nrow, 512), 0)
    coli = lax.broadcasted_iota(jnp.int32, (nrow, 512), 1)
    bd = _idiv(coli, HD) == _imod(rowi, B_HEADS)

    @pl.when(step == 0)
    def _():
        q = q_ref[0]
        qrows = jnp.concatenate(
            [jnp.broadcast_to(q[t:t + 1, :], (B_HEADS, 512)) for t in range(dec)], axis=0)
        qr_ref[...] = jnp.where(bd, qrows, 0.0).astype(BF16)
        m_ref[...] = jnp.full_like(m_ref, NEG)
        l_ref[...] = jnp.zeros_like(l_ref)
        r_ref[...] = jnp.zeros_like(r_ref)
        acc_ref[...] = jnp.zeros_like(acc_ref)

    def pages(krefs, vrefs, lrefs, causal):
        n = len(krefs)
        qr = qr_ref[...]
        ss = [_dot(qr, kr().astype(BF16)) for kr in krefs]
        cums = [_exact_right(lr(), cum_ref[...]) for lr in lrefs]
        r = r_ref[...]
        logits = []
        for u in range(n):
            w = jnp.concatenate([cums[u][:, 0:LANES]] * dec, axis=0)
            logit = ss[u] - (r + w)
            if causal:
                key = lax.broadcasted_iota(jnp.int32, (nrow, LANES), 1)
                trow = _idiv(lax.broadcasted_iota(jnp.int32, (nrow, LANES), 0), B_HEADS)
                logit = jnp.where(key <= trow, logit, NEG)
            logits.append(logit)
            r = r + jnp.concatenate([cums[u][:, LANES:2 * LANES]] * dec, axis=0)
        r_ref[...] = r
        m_old = m_ref[...]
        m_new = m_old
        for lg in logits:
            m_new = jnp.maximum(m_new, jnp.max(lg, axis=1, keepdims=True))
        ps = [jnp.exp(lg - m_new) for lg in logits]
        alpha = jnp.exp(m_old - m_new)
        lsum = jnp.sum(ps[0], axis=1, keepdims=True)
        for p in ps[1:]:
            lsum = lsum + jnp.sum(p, axis=1, keepdims=True)
        l_ref[...] = alpha * l_ref[...] + lsum
        m_ref[...] = m_new
        pv = _dot_nt(ps[0].astype(BF16), vrefs[0]().astype(BF16))
        for u in range(1, n):
            pv = pv + _dot_nt(ps[u].astype(BF16), vrefs[u]().astype(BF16))
        acc_ref[...] = alpha[:, 0:1] * acc_ref[...] + pv

    pages([lambda u=u: kp[u][...] for u in range(pp)], [lambda u=u: vp[u][...] for u in range(pp)],
          [lambda u=u: lp[u][...] for u in range(pp)], False)

    @pl.when(step == pl.num_programs(1) - 1)
    def _():
        wide = wide_ref[...]
        knew = _dot(kn_ref[0].astype(BF16), wide)
        vnew = _dot(vn_ref[0].astype(BF16), wide)
        lfnew = _exact_right(lfn_ref[0], wide)
        pages([lambda: knew], [lambda: vnew], [lambda: lfnew], True)
        res = jnp.where(bd, acc_ref[...] / l_ref[:, 0:1], 0.0)
        for t in range(dec):
            o_ref[0, t:t + 1, :] = jnp.sum(res[t * B_HEADS:(t + 1) * B_HEADS], axis=0, keepdims=True)


def _foxdec(layer, page_table, q, knew, vnew, lfn_t, cache_k, cache_v, cache_lft, pp):
    ns, dec, _ = q.shape
    n_pages = page_table.shape[1]
    page = cache_k.shape[3]
    nsteps = n_pages // pp
    nrow = dec * B_HEADS
    cum = np.concatenate([np.triu(np.ones((page, page), np.float32)), np.ones((page, page), np.float32)], axis=1)
    seq3 = lambda s, p, pt: (s, 0, 0)
    fix2 = lambda s, p, pt: (0, 0)

    def cache_map(u):
        return lambda s, p, pt: (layer, pt[s, p * pp + u], 0, 0)

    wide = np.eye(SUB, page, dtype=np.float32)
    in_specs = [pl.BlockSpec((1, dec, 512), seq3), pl.BlockSpec((1, 512, SUB), seq3),
                pl.BlockSpec((1, 512, SUB), seq3), pl.BlockSpec((1, B_HEADS, SUB), seq3),
                pl.BlockSpec((page, 2 * page), fix2), pl.BlockSpec((SUB, page), fix2)]
    in_specs += [pl.BlockSpec((None, None, 512, page), cache_map(u)) for u in range(pp)]
    in_specs += [pl.BlockSpec((None, None, 512, page), cache_map(u)) for u in range(pp)]
    in_specs += [pl.BlockSpec((None, None, B_HEADS, page), cache_map(u)) for u in range(pp)]
    grid_spec = pltpu.PrefetchScalarGridSpec(
        num_scalar_prefetch=1,
        grid=(ns, nsteps),
        in_specs=in_specs,
        out_specs=pl.BlockSpec((1, dec, 512), seq3),
        scratch_shapes=[pltpu.VMEM((nrow, 512), BF16), pltpu.VMEM((nrow, LANES), F32),
                        pltpu.VMEM((nrow, LANES), F32), pltpu.VMEM((nrow, LANES), F32),
                        pltpu.VMEM((nrow, 512), F32)],
    )
    return pl.pallas_call(
        functools.partial(_foxdec_kernel, pp=pp, dec=dec),
        grid_spec=grid_spec,
        out_shape=jax.ShapeDtypeStruct((ns, dec, 512), F32),
        compiler_params=_cp("arbitrary", "arbitrary"),
        name="foxdec",
    )(page_table, q, knew, vnew, lfn_t, _const(cum), _const(wide),
      *([cache_k] * pp), *([cache_v] * pp), *([cache_lft] * pp))


def _hgrn_kernel(*refs, layer, ns, cps, valid, has_init):
    (q_ref, f_ref, v_ref, gate_ref, lbraw_ref, ng_ref, tril_ref, rs_ref, tile_ref, hs_ref) = refs[:10]
    pos = 10
    s0_ref = None
    if has_init:
        s0_ref = refs[pos]
        pos += 1
    o_ref, sout_ref, st_ref, oin_ref, oint_ref, tmp_ref = refs[pos:pos + 6]
    i = pl.program_id(1)
    tb = q_ref.shape[0]
    nchunk = tb // SUB
    width = A_HEADS * HD

    lbraw = lbraw_ref[...]
    e = jnp.exp(lbraw - jnp.max(lbraw, axis=0, keepdims=True))
    prob = e / jnp.sum(e, axis=0, keepdims=True)
    lb = jnp.zeros((1, width), F32)
    for d in range(1, layer + 1):
        lb = lb + prob[d:d + 1, :]

    fr = f_ref[...]
    qr = q_ref[...]
    vr = v_ref[...]
    logf = jnp.log(lb + (1.0 - lb) * _sigmoid(fr))
    key = (1.0 - lb) * _sigmoid(-fr)
    if valid < SUB:
        rowv = _imod(lax.broadcasted_iota(jnp.int32, (tb, width), 0), SUB) < valid
        logf = jnp.where(rowv, logf, 0.0)
        key = jnp.where(rowv, key, 0.0)
    q = (qr * _sigmoid(qr)) * (HD ** -0.5)
    g = _exact_left(tril_ref[...], logf)
    g3 = g.reshape(nchunk, SUB, width)
    q3 = q.reshape(nchunk, SUB, width)
    k3 = key.reshape(nchunk, SUB, width)

    att = jnp.zeros((tb, LANES), F32)
    for s in range(SUB):
        d = jnp.minimum(g3 - g3[:, s:s + 1, :], 0.0)
        p = (q3 * jnp.exp(d)) * k3[:, s:s + 1, :]
        att = att + _dot(p.reshape(tb, width).astype(BF16), rs_ref[s])
    attb = att.astype(BF16)

    rows = lax.broadcasted_iota(jnp.int32, (tb, tb), 0)
    cols = lax.broadcasted_iota(jnp.int32, (tb, tb), 1)
    causal = (_idiv(rows, SUB) == _idiv(cols, SUB)) & (cols <= rows)
    lane = lax.broadcasted_iota(jnp.int32, (tb, LANES), 1)
    for pr in range(2):
        vpair = vr[:, pr * LANES:(pr + 1) * LANES]
        acc = None
        for hh in range(2):
            h = 2 * pr + hh
            full = _dot(attb, tile_ref[h])
            full = jnp.where(causal, full, 0.0).astype(BF16)
            vm = jnp.where((lane < HD) == (hh == 0), vpair, 0.0).astype(BF16)
            term = _dot(full, vm)
            acc = term if acc is None else acc + term
        oin_ref[:, pr * LANES:(pr + 1) * LANES] = acc

    gend3 = g3[:, SUB - 1:SUB, :]
    kd, kdl = _split2((k3 * jnp.exp(gend3 - g3)).reshape(tb, width))
    eg = jnp.exp(g)
    qg = (q * eg).astype(BF16)
    r128 = lax.broadcasted_iota(jnp.int32, (LANES, LANES), 0)
    c128 = lax.broadcasted_iota(jnp.int32, (LANES, LANES), 1)
    bdm = _idiv(r128, HD) == _idiv(c128, HD)
    vb, vbl = _split2(vr)
    last = i == pl.num_programs(1) - 1
    units = [(sq, pr) for sq in range(ns) for pr in range(2)]
    rowsl = lambda sq, c: slice((sq * cps + c) * SUB, (sq * cps + c + 1) * SUB)
    lanesl = lambda pr: slice(pr * LANES, (pr + 1) * LANES)
    incs = {}
    for sq, pr in units:
        for c in range(cps):
            rs_, ls_ = rowsl(sq, c), lanesl(pr)
            u = (_dot_tn(vb[rs_, ls_], kd[rs_, ls_]) + _dot_tn(vb[rs_, ls_], kdl[rs_, ls_])) \
                + _dot_tn(vbl[rs_, ls_], kd[rs_, ls_])
            incs[sq, pr, c] = jnp.where(bdm, u, 0.0)
    if has_init:
        tmp_ref[...] = jnp.zeros_like(tmp_ref)
        for sq, pr in units:
            tmp_ref[2 * sq + pr, 0:HD, 0:HD] = s0_ref[sq, 2 * pr]
            tmp_ref[2 * sq + pr, HD:LANES, HD:LANES] = s0_ref[sq, 2 * pr + 1]
    seen = {}
    final = {}
    for sq, pr in units:
        if has_init:
            st = tmp_ref[2 * sq + pr].T
        else:
            st = jnp.where(i == 0, 0.0, st_ref[pr])
        for c in range(cps):
            seen[sq, pr, c] = st.astype(BF16)
            r_end = (sq * cps + c + 1) * SUB
            st = st * eg[r_end - 1:r_end, lanesl(pr)] + incs[sq, pr, c]
        if not has_init:
            st_ref[pr] = st
        final[sq, pr] = st
    for sq, pr in units:
        for c in range(cps):
            oint_ref[rowsl(sq, c), lanesl(pr)] = _dot_nt(qg[rowsl(sq, c), lanesl(pr)], seen[sq, pr, c])

    @pl.when(last)
    def _():
        for sq, pr in units:
            tmp_ref[2 * sq + pr] = final[sq, pr].T
        for sq, pr in units:
            sout_ref[sq, 2 * pr] = tmp_ref[2 * sq + pr, 0:HD, 0:HD]
            sout_ref[sq, 2 * pr + 1] = tmp_ref[2 * sq + pr, HD:LANES, HD:LANES]

    o = oin_ref[...] + oint_ref[...]
    gr = gate_ref[...]
    o = (o * lax.rsqrt(_head_ssq(o, hs_ref[...]) * (1.0 / HD) + EPS)) * ng_ref[...]
    o_ref[...] = o * (gr * _sigmoid(gr))


def _hgrn(a, lbraw, ng, s0, layer, n_outer, n_inner, tb, ns, cps, valid):
    rows = a.shape[0]
    width = A_HEADS * HD
    has_init = s0 is not None
    nseq = n_outer * ns
    rs = np.zeros((SUB, width, LANES), np.float32)
    tile = np.zeros((A_HEADS, LANES, tb), np.float32)
    for s in range(SUB):
        for h in range(A_HEADS):
            rs[s, h * HD + np.arange(HD), h * SUB + s] = 1.0
            tile[h, h * SUB + s, np.arange(tb // SUB) * SUB + s] = 1.0

    def col(j):
        return lambda o, i: (o * n_inner + i, j)

    fix2 = lambda o, i: (0, 0)
    fix3 = lambda o, i: (0, 0, 0)
    in_specs = [pl.BlockSpec((tb, width), col(j)) for j in range(4)]
    in_specs += [pl.BlockSpec(lbraw.shape, fix2), pl.BlockSpec((1, width), fix2),
                 pl.BlockSpec((tb, tb), fix2), pl.BlockSpec((SUB, width, LANES), fix3),
                 pl.BlockSpec((A_HEADS, LANES, tb), fix3), pl.BlockSpec((width, width), fix2)]
    args = [a, a, a, a, lbraw, ng, _const(_chunk_tril(tb, SUB)), _const(rs), _const(tile),
            _const(_head_sum_matrix(width))]
    if has_init:
        in_specs.append(pl.BlockSpec((ns, A_HEADS, HD, HD), lambda o, i: (o, 0, 0, 0)))
        args.append(s0)
    return pl.pallas_call(
        functools.partial(_hgrn_kernel, layer=layer, ns=ns, cps=cps, valid=valid, has_init=has_init),
        grid=(n_outer, n_inner),
        in_specs=in_specs,
        out_specs=[pl.BlockSpec((tb, width), lambda o, i: (o * n_inner + i, 0)),
                   pl.BlockSpec((ns, A_HEADS, HD, HD), lambda o, i: (o, 0, 0, 0))],
        out_shape=[jax.ShapeDtypeStruct((rows, width), F32),
                   jax.ShapeDtypeStruct((nseq, A_HEADS, HD, HD), F32)],
        scratch_shapes=[pltpu.VMEM((2, LANES, LANES), F32), pltpu.VMEM((tb, width), F32),
                        pltpu.VMEM((tb, width), F32), pltpu.VMEM((2 * ns, LANES, LANES), F32)],
        compiler_params=_cp("arbitrary", "arbitrary"),
        name="hgrn",
    )(*args)


def _gdn_kernel(*refs, ns, cps, chunk, has_init):
    (x_ref, z_ref, g_ref, cw_ref, ng_ref, tril_ref, hs_ref, eg_ref, eb_ref, eg2_ref, eb2_ref) = refs[:11]
    pos = 11
    cinit_ref = s0_ref = None
    if has_init:
        cinit_ref, s0_ref = refs[pos:pos + 2]
        pos += 2
    o_ref, sout_ref, st_ref, cv_ref, oacc_ref, vn_ref, tmp_ref = refs[pos:pos + 7]
    i = pl.program_id(1)
    tb = x_ref.shape[0]
    width = C_HEADS * HD
    rows_seq = cps * chunk
    last = i == pl.num_programs(1) - 1

    x = x_ref[...]
    cw = cw_ref[...]
    acts = []
    for sq in range(ns):
        xs = x[sq * rows_seq:(sq + 1) * rows_seq]
        if has_init:
            prev = cinit_ref[sq]
        else:
            prev = jnp.where(i == 0, 0.0, cv_ref[...])
        xc = jnp.concatenate([prev, xs], axis=0)
        conv = xc * cw[CONV_W - 1:CONV_W, :]
        for j in range(1, CONV_W):
            conv = conv + pltpu.roll(xc, j, 0) * cw[CONV_W - 1 - j:CONV_W - j, :]
        acts.append(conv[8:])
        if not has_init:
            cv_ref[...] = xs[rows_seq - 8:]
    conv = acts[0] if ns == 1 else jnp.concatenate(acts, axis=0)
    act = conv * _sigmoid(conv)
    hs = hs_ref[...]
    q = act[:, 0:width]
    k = act[:, width:2 * width]
    v = act[:, 2 * width:3 * width]
    q = (q * lax.rsqrt(_head_ssq(q, hs) + EPS)) * (HD ** -0.5)
    k = k * lax.rsqrt(_head_ssq(k, hs) + EPS)

    gt = g_ref[...]
    gc = _exact_left(tril_ref[...], gt)
    gexp = _exact_right(gc, eg_ref[...])
    bexp = _exact_right(gt, eb_ref[...])
    gcol = _exact_right(gc, eg2_ref[...])
    bcol = _exact_right(gt, eb2_ref[...])
    nchunk = tb // chunk
    gexp3 = gexp.reshape(nchunk, chunk, width)
    gend3 = gexp3[:, chunk - 1:chunk, :]
    eg = jnp.exp(gexp)
    rhs = jnp.concatenate([v * bexp, (k * bexp) * eg], axis=1).astype(BF16)
    qg = (q * eg).astype(BF16)
    kd = (k.reshape(nchunk, chunk, width) * jnp.exp(gend3 - gexp3)).reshape(tb, width).astype(BF16)
    kb = k.astype(BF16)
    lane = lax.broadcasted_iota(jnp.int32, (tb, width), 1)
    kmask = [jnp.where(_idiv(lane, HD) == h, k, 0.0).astype(BF16) for h in range(C_HEADS)]
    qmask = [jnp.where(_idiv(lane, HD) == h, q, 0.0).astype(BF16) for h in range(C_HEADS)]

    rc = lax.broadcasted_iota(jnp.int32, (tb, tb), 0)
    cc = lax.broadcasted_iota(jnp.int32, (tb, tb), 1)
    same = _idiv(rc, chunk) == _idiv(cc, chunk)
    eye = rc == cc
    incl = same & (cc <= rc)
    strict = same & (cc < rc)
    hl = _idiv(lane, HD)
    r256 = lax.broadcasted_iota(jnp.int32, (width, width), 0)
    c256 = lax.broadcasted_iota(jnp.int32, (width, width), 1)
    bdm = _idiv(r256, HD) == _idiv(c256, HD)
    nstage = int(math.log2(chunk))
    heads = range(C_HEADS)
    reps = tb // LANES
    gct = gc.T
    decay, xm, tm, qkd = [], [], [], []
    for h in heads:
        g_t = jnp.concatenate([gcol[:, h * LANES:(h + 1) * LANES]] * reps, axis=1)
        decay.append(jnp.exp(jnp.minimum(g_t - gct[8 + h:9 + h, :], 0.0)))
    kk = [_dot_nt(kmask[h], kb) for h in heads]
    qk = [_dot_nt(qmask[h], kb) for h in heads]
    for h in heads:
        b_t = jnp.concatenate([bcol[:, h * LANES:(h + 1) * LANES]] * reps, axis=1)
        x0 = -jnp.where(strict, (b_t * kk[h]) * decay[h], 0.0)
        xm.append(x0)
        tm.append(jnp.where(eye, 1.0, 0.0) + x0)
        qkd.append(jnp.where(incl, qk[h] * decay[h], 0.0).astype(BF16))
    for _ in range(nstage - 1):
        xb = [x.astype(BF16) for x in xm]
        xm = [_dot(xb[h], xb[h]) for h in heads]
        tm = [tm[h] + _dot(tm[h].astype(BF16), xm[h].astype(BF16)) for h in heads]
    sol = [_dot(tm[h].astype(BF16), rhs) for h in heads]
    u = jnp.zeros((tb, width), F32)
    w = jnp.zeros((tb, width), F32)
    for h in heads:
        u = u + jnp.where(hl == h, sol[h][:, 0:width], 0.0)
        w = w + jnp.where(hl == h, sol[h][:, width:2 * width], 0.0)
    wb = w.astype(BF16)

    if has_init:
        tmp_ref[...] = jnp.zeros_like(tmp_ref)
        for sq in range(ns):
            for h in heads:
                tmp_ref[sq, h * HD:(h + 1) * HD, h * HD:(h + 1) * HD] = s0_ref[sq, h]
        sts = [tmp_ref[sq] for sq in range(ns)]
    else:
        sts = [jnp.where(i == 0, 0.0, st_ref[...])]
    rowsl = lambda sq, c: slice((sq * cps + c) * chunk, (sq * cps + c + 1) * chunk)
    for c in range(cps):
        stb = [st.astype(BF16) for st in sts]
        ws = [_dot(wb[rowsl(sq, c)], stb[sq]) for sq in range(ns)]
        for sq in range(ns):
            oacc_ref[rowsl(sq, c), :] = _dot(qg[rowsl(sq, c)], stb[sq])
        vns = [(u[rowsl(sq, c)] - ws[sq]).astype(BF16) for sq in range(ns)]
        for sq in range(ns):
            vn_ref[rowsl(sq, c), :] = vns[sq]
        incs = [jnp.where(bdm, _dot_tn(kd[rowsl(sq, c)], vns[sq]), 0.0) for sq in range(ns)]
        sts = [sts[sq] * eg[(sq * cps + c + 1) * chunk - 1:(sq * cps + c + 1) * chunk, :] + incs[sq]
               for sq in range(ns)]
    if not has_init:
        st_ref[...] = sts[0]

    @pl.when(last)
    def _():
        for sq in range(ns):
            tmp_ref[sq] = sts[sq]
        for sq in range(ns):
            for h in heads:
                sout_ref[sq, h] = tmp_ref[sq, h * HD:(h + 1) * HD, h * HD:(h + 1) * HD]

    vnb = vn_ref[...]
    o = oacc_ref[...]
    for h in heads:
        o = o + jnp.where(hl == h, _dot(qkd[h], vnb), 0.0)
    z = z_ref[...]
    o = (o * lax.rsqrt(_head_ssq(o, hs) * (1.0 / HD) + EPS)) * ng_ref[...]
    o_ref[...] = o * (z * _sigmoid(z))


def _gdn(cfull, g, cw, ng, cinit, s0, n_outer, n_inner, tb, ns, cps, chunk):
    rows = cfull.shape[0]
    width = C_HEADS * HD
    has_init = s0 is not None
    nseq = n_outer * ns
    eg = np.zeros((LANES, width), np.float32)
    eb = np.zeros((LANES, width), np.float32)
    eg2 = np.zeros((LANES, C_HEADS * LANES), np.float32)
    eb2 = np.zeros((LANES, C_HEADS * LANES), np.float32)
    for h in range(C_HEADS):
        eg[8 + h, h * HD:(h + 1) * HD] = 1.0
        eb[12 + h, h * HD:(h + 1) * HD] = 1.0
        eg2[8 + h, h * LANES:(h + 1) * LANES] = 1.0
        eb2[12 + h, h * LANES:(h + 1) * LANES] = 1.0
    rowm = lambda o, i: (o * n_inner + i, 0)
    fix2 = lambda o, i: (0, 0)
    in_specs = [pl.BlockSpec((tb, 3 * width), rowm),
                pl.BlockSpec((tb, width), lambda o, i: (o * n_inner + i, 3)),
                pl.BlockSpec((tb, LANES), rowm), pl.BlockSpec((CONV_W, 3 * width), fix2),
                pl.BlockSpec((1, width), fix2), pl.BlockSpec((tb, tb), fix2),
                pl.BlockSpec((width, width), fix2), pl.BlockSpec((LANES, width), fix2),
                pl.BlockSpec((LANES, width), fix2), pl.BlockSpec((LANES, C_HEADS * LANES), fix2),
                pl.BlockSpec((LANES, C_HEADS * LANES), fix2)]
    args = [cfull, cfull, g, cw, ng, _const(_chunk_tril(tb, chunk)), _const(_head_sum_matrix(width)),
            _const(eg), _const(eb), _const(eg2), _const(eb2)]
    if has_init:
        in_specs += [pl.BlockSpec((ns, 8, 3 * width), lambda o, i: (o, 0, 0)),
                     pl.BlockSpec((ns, C_HEADS, HD, HD), lambda o, i: (o, 0, 0, 0))]
        args += [cinit, s0]
    return pl.pallas_call(
        functools.partial(_gdn_kernel, ns=ns, cps=cps, chunk=chunk, has_init=has_init),
        grid=(n_outer, n_inner),
        in_specs=in_specs,
        out_specs=[pl.BlockSpec((tb, width), rowm),
                   pl.BlockSpec((ns, C_HEADS, HD, HD), lambda o, i: (o, 0, 0, 0))],
        out_shape=[jax.ShapeDtypeStruct((rows, width), F32),
                   jax.ShapeDtypeStruct((nseq, C_HEADS, HD, HD), F32)],
        scratch_shapes=[pltpu.VMEM((width, width), F32), pltpu.VMEM((8, 3 * width), F32),
                        pltpu.VMEM((tb, width), F32), pltpu.VMEM((tb, width), BF16),
                        pltpu.VMEM((ns, width, width), F32)],
        compiler_params=_cp("arbitrary", "arbitrary"),
        name="gdn",
    )(*args)


def _post_kernel(x_ref, oa_ref, ob_ref, oc_ref, wo_ref, ln_ref, wr_ref, br_ref, tril_ref,
                 xn_ref, hf_ref, gate_ref, sel_ref, rank_ref, cnt_ref, carry_ref):
    x = x_ref[...]
    mix = _dot(oa_ref[...].astype(BF16), wo_ref[0:256, :])
    mix = mix + _dot(ob_ref[...].astype(BF16), wo_ref[256:768, :])
    mix = mix + _dot(oc_ref[...].astype(BF16), wo_ref[768:1024, :])
    xn = x + mix
    xn_ref[...] = xn
    ms = jnp.mean(xn * xn, axis=-1, keepdims=True)
    hf = (xn * lax.rsqrt(ms + EPS)) * ln_ref[...]
    hf_ref[...] = hf
    logits = _dot(hf.astype(BF16), wr_ref[...]) + br_ref[...]
    lane = lax.broadcasted_iota(jnp.int32, logits.shape, 1)
    big = jnp.int32(1 << 20)
    isg = (lane >= N_EXPERTS) & (lane < N_EXPERTS + N_GROUPS)
    gl = jnp.where(isg, logits, NEG)
    gm = jnp.max(gl, axis=1, keepdims=True)
    gidx = jnp.min(jnp.where(isg & (gl == gm), lane, big), axis=1, keepdims=True) - N_EXPERTS
    top_gp = 1.0 / jnp.sum(jnp.where(isg, jnp.exp(gl - gm), 0.0), axis=1, keepdims=True)
    ing = (lane < N_EXPERTS) & (_idiv(lane, EXPERTS_PER_GROUP) == gidx)
    el = jnp.where(ing, logits, NEG)
    em = jnp.max(el, axis=1, keepdims=True)
    ee = jnp.where(ing, jnp.exp(el - em), 0.0)
    prob = ee / jnp.sum(ee, axis=1, keepdims=True)
    p1 = jnp.max(prob, axis=1, keepdims=True)
    i1 = jnp.min(jnp.where(ing & (prob == p1), lane, big), axis=1, keepdims=True)
    rest = jnp.where(ing & (lane != i1), prob, -1.0)
    p2 = jnp.max(rest, axis=1, keepdims=True)
    i2 = jnp.min(jnp.where(ing & (lane != i1) & (rest == p2), lane, big), axis=1, keepdims=True)
    den = p1 + p2
    gate_ref[...] = jnp.where(lane == i1, (top_gp * p1) / den,
                              jnp.where(lane == i2, (top_gp * p2) / den, 0.0))
    sel = jnp.where((lane == i1) | (lane == i2), 1.0, 0.0)
    sel_ref[...] = sel

    @pl.when(pl.program_id(0) == 0)
    def _():
        carry_ref[...] = jnp.zeros_like(carry_ref)

    rank_ref[...] = _dot(tril_ref[...], sel.astype(BF16)) + carry_ref[...]
    carry_ref[...] = carry_ref[...] + jnp.sum(sel, axis=0, keepdims=True)
    cnt_ref[...] = carry_ref[...]


def _post(x, oa, ob, oc, wo, ln, wr, br, tm):
    t = x.shape[0]
    row = lambda i: (i, 0)
    fix = lambda i: (0, 0)
    return pl.pallas_call(
        _post_kernel,
        grid=(t // tm,),
        in_specs=[pl.BlockSpec((tm, D_MODEL), row), pl.BlockSpec((tm, 256), row),
                  pl.BlockSpec((tm, 512), row), pl.BlockSpec((tm, 256), row),
                  pl.BlockSpec((D_MODEL, D_MODEL), fix), pl.BlockSpec((1, D_MODEL), fix),
                  pl.BlockSpec((D_MODEL, LANES), fix), pl.BlockSpec((1, LANES), fix),
                  pl.BlockSpec((tm, tm), fix)],
        out_specs=[pl.BlockSpec((tm, D_MODEL), row), pl.BlockSpec((tm, D_MODEL), row),
                   pl.BlockSpec((tm, LANES), row), pl.BlockSpec((tm, LANES), row),
                   pl.BlockSpec((tm, LANES), row), pl.BlockSpec((1, LANES), fix)],
        out_shape=[jax.ShapeDtypeStruct((t, D_MODEL), F32), jax.ShapeDtypeStruct((t, D_MODEL), F32),
                   jax.ShapeDtypeStruct((t, LANES), F32), jax.ShapeDtypeStruct((t, LANES), F32),
                   jax.ShapeDtypeStruct((t, LANES), F32), jax.ShapeDtypeStruct((1, LANES), F32)],
        scratch_shapes=[pltpu.VMEM((1, LANES), F32)],
        compiler_params=_cp("arbitrary"),
        name="post",
    )(x, oa, ob, oc, wo, ln, wr, br, _const(np.tril(np.ones((tm, tm), np.float32), -1)))


MOE_TM = 256


def _moe_dest_kernel(sel_ref, gate_ref, rank_ref, off_ref, info_ref):
    on = sel_ref[...] > 0.0
    gates = gate_ref[...]
    dest = off_ref[...] + rank_ref[...]
    lane = lax.broadcasted_iota(jnp.int32, gates.shape, 1)
    la = jnp.min(jnp.where(on, lane, LANES), axis=1, keepdims=True)
    lb = jnp.max(jnp.where(on, lane, -1), axis=1, keepdims=True)
    pick = lambda v, l: jnp.sum(jnp.where(lane == l, v, 0.0), axis=1, keepdims=True)
    info_ref[...] = jnp.where(lane == 0, pick(dest, la),
                              jnp.where(lane == 1, pick(dest, lb),
                                        jnp.where(lane == 2, pick(gates, la),
                                                  jnp.where(lane == 3, pick(gates, lb), 0.0))))


def _moe_dest(sel, gates, rank, off, tm):
    t = sel.shape[0]
    row = lambda i: (i, 0)
    return pl.pallas_call(
        _moe_dest_kernel,
        grid=(t // tm,),
        in_specs=[pl.BlockSpec((tm, LANES), row)] * 3 + [pl.BlockSpec((1, LANES), lambda i: (0, 0))],
        out_specs=pl.BlockSpec((tm, LANES), row),
        out_shape=jax.ShapeDtypeStruct((t, LANES), F32),
        compiler_params=_cp("arbitrary"),
        name="moe_dest",
    )(sel, gates, rank, off)


def _row_copy(src_ref, src_row, dst_ref, dst_row, sem):
    return pltpu.make_async_copy(src_ref.at[pl.ds(src_row, 1), :], dst_ref.at[pl.ds(dst_row, 1), :], sem)


def _rows_wait(src_ref, dst_ref, nrows, sem):
    pltpu.make_async_copy(src_ref.at[pl.ds(0, nrows), :], dst_ref.at[pl.ds(0, nrows), :], sem).wait()


def _moe_dispatch_kernel(da_ref, db_ref, hf_ref, xs_in_ref, xs_ref, sem):
    del xs_in_ref
    tm = hf_ref.shape[0]

    def issue(r, carry):
        _row_copy(hf_ref, r, xs_ref, da_ref[0, 0, r], sem.at[0]).start(priority=0)
        _row_copy(hf_ref, r, xs_ref, db_ref[0, 0, r], sem.at[0]).start(priority=1)
        return carry

    lax.fori_loop(0, tm, issue, 0, unroll=4)
    _rows_wait(hf_ref, xs_ref, tm, sem.at[0])
    _rows_wait(hf_ref, xs_ref, tm, sem.at[0])


def _moe_dispatch(hf, da, db, rows_buf, tm):
    t = hf.shape[0]
    nrows = rows_buf.shape[0]
    idx = pl.BlockSpec((1, 1, tm), lambda i: (i, 0, 0), memory_space=pltpu.SMEM)
    return pl.pallas_call(
        _moe_dispatch_kernel,
        grid=(t // tm,),
        in_specs=[idx, idx, pl.BlockSpec((tm, D_MODEL), lambda i: (i, 0)),
                  pl.BlockSpec(memory_space=pl.ANY)],
        out_specs=pl.BlockSpec(memory_space=pl.ANY),
        out_shape=jax.ShapeDtypeStruct((nrows, D_MODEL), F32),
        scratch_shapes=[pltpu.SemaphoreType.DMA((1,))],
        input_output_aliases={3: 0},
        compiler_params=_cp("arbitrary"),
        name="moe_dispatch",
    )(da, db, hf, rows_buf)


def _moe_ffn_kernel(texp_ref, nval_ref, x_ref, wg_ref, wu_ref, wd_ref, y_ref):
    del texp_ref
    i = pl.program_id(0)

    @pl.when(i < nval_ref[0])
    def _():
        x = x_ref[...].astype(BF16)
        gp = _dot(x, wg_ref[0])
        up = _dot(x, wu_ref[0])
        y_ref[...] = _dot(((gp * _sigmoid(gp)) * up).astype(BF16), wd_ref[0])

    @pl.when(i >= nval_ref[0])
    def _():
        y_ref[...] = jnp.zeros_like(y_ref)


def _moe_ffn(xs, texp, nval, wg, wu, wd):
    nrows = xs.shape[0]
    tile = lambda i, te, nv: (jnp.minimum(i, nv[0] - 1), 0)
    wmap = lambda i, te, nv: (te[i], 0, 0)
    grid_spec = pltpu.PrefetchScalarGridSpec(
        num_scalar_prefetch=2,
        grid=(nrows // MOE_TM,),
        in_specs=[pl.BlockSpec((MOE_TM, D_MODEL), tile),
                  pl.BlockSpec((1, D_MODEL, D_EXPERT), wmap), pl.BlockSpec((1, D_MODEL, D_EXPERT), wmap),
                  pl.BlockSpec((1, D_EXPERT, D_MODEL), wmap)],
        out_specs=pl.BlockSpec((MOE_TM, D_MODEL), lambda i, te, nv: (i, 0)),
    )
    return pl.pallas_call(
        _moe_ffn_kernel,
        grid_spec=grid_spec,
        out_shape=jax.ShapeDtypeStruct((nrows, D_MODEL), F32),
        compiler_params=_cp("arbitrary"),
        name="moe_ffn",
    )(texp, nval, xs, wg, wu, wd)


def _moe_combine_kernel(da_ref, db_ref, dan_ref, dbn_ref, info_ref, x_ref, ys_ref, o_ref, buf_ref, sem):
    tm = x_ref.shape[0]
    i = pl.program_id(0)
    n = pl.num_programs(0)
    slot = i % 2

    def gather(ia_ref, ib_ref, s):
        def issue(r, carry):
            _row_copy(ys_ref, ia_ref[0, 0, r], buf_ref.at[s, 0], r, sem.at[s]).start(priority=0)
            _row_copy(ys_ref, ib_ref[0, 0, r], buf_ref.at[s, 1], r, sem.at[s]).start(priority=1)
            return carry

        lax.fori_loop(0, tm, issue, 0, unroll=4)

    @pl.when(i == 0)
    def _():
        gather(da_ref, db_ref, 0)

    @pl.when(i + 1 < n)
    def _():
        gather(dan_ref, dbn_ref, 1 - slot)

    _rows_wait(ys_ref, buf_ref.at[slot, 0], tm, sem.at[slot])
    _rows_wait(ys_ref, buf_ref.at[slot, 1], tm, sem.at[slot])
    info = info_ref[...]
    o_ref[...] = (x_ref[...] + info[:, 2:3] * buf_ref[slot, 0]) + info[:, 3:4] * buf_ref[slot, 1]


def _moe_combine(xn, info, da, db, ys, tm):
    t = xn.shape[0]
    n = t // tm
    idx = pl.BlockSpec((1, 1, tm), lambda i: (i, 0, 0), memory_space=pltpu.SMEM)
    nxt = pl.BlockSpec((1, 1, tm), lambda i: (jnp.minimum(i + 1, n - 1), 0, 0), memory_space=pltpu.SMEM)
    return pl.pallas_call(
        _moe_combine_kernel,
        grid=(n,),
        in_specs=[idx, idx, nxt, nxt, pl.BlockSpec((tm, LANES), lambda i: (i, 0)),
                  pl.BlockSpec((tm, D_MODEL), lambda i: (i, 0)), pl.BlockSpec(memory_space=pl.ANY)],
        out_specs=pl.BlockSpec((tm, D_MODEL), lambda i: (i, 0)),
        out_shape=jax.ShapeDtypeStruct((t, D_MODEL), F32),
        scratch_shapes=[pltpu.VMEM((2, 2, tm, D_MODEL), F32), pltpu.SemaphoreType.DMA((2,))],
        compiler_params=_cp("arbitrary"),
        name="moe_combine",
    )(da, db, da, db, info, xn, ys)


def _moe_rows(t):
    return -(-(2 * t + N_EXPERTS * (MOE_TM - 1)) // MOE_TM) * MOE_TM


def _moe(hf, gates, sel, rank, cnt, xn, wg, wu, wd, layer, rows_buf, tm):
    t = hf.shape[0]
    nrows = rows_buf.shape[0]
    ntile = nrows // MOE_TM
    cnt = cnt[0, :N_EXPERTS].astype(jnp.int32)
    padded = (cnt + (MOE_TM - 1)) // MOE_TM * MOE_TM
    end = jnp.cumsum(padded)
    off = jnp.zeros((1, LANES), F32).at[0, :N_EXPERTS].set((end - padded).astype(F32))
    nval = (end[-1] // MOE_TM).astype(jnp.int32)
    tile_start = jnp.arange(ntile, dtype=jnp.int32) * MOE_TM
    texp = jnp.sum((end[None, :] <= tile_start[:, None]).astype(jnp.int32), axis=1)
    texp = jnp.minimum(texp, N_EXPERTS - 1)
    texp = jnp.where(jnp.arange(ntile) < nval, texp, texp[jnp.maximum(nval - 1, 0)]) + layer * N_EXPERTS
    info = _moe_dest(sel, gates, rank, off, min(MOE_DEST_TILE, t))
    da = info[:, 0].astype(jnp.int32)
    db = info[:, 1].astype(jnp.int32)
    tmd = min(MOE_DISPATCH_TILE, t)
    xs = _moe_dispatch(hf, da.reshape(t // tmd, 1, tmd), db.reshape(t // tmd, 1, tmd), rows_buf, tmd)
    ys = _moe_ffn(xs, texp, nval.reshape(1), wg, wu, wd)
    return _moe_combine(xn, info, da.reshape(t // tm, 1, tm), db.reshape(t // tm, 1, tm), ys, tm), xs


def _pad_rows(a, nseq, dec):
    c = a.shape[-1]
    return jnp.pad(a.reshape(nseq, dec, c), ((0, 0), (0, SUB - dec), (0, 0))).reshape(nseq * SUB, c)


def _layer_params(l, ln_mix, w_in, hgrn_norm, fox_bf, fox_qnorm, fox_knorm, gdn_conv, gdn_a_log,
                  gdn_dt_bias, gdn_norm, w_out, ln_ffn, w_group, b_group, w_router, b_router,
                  w_gate, w_up, w_down):
    w = w_in[l]
    gates = jnp.concatenate([w[:, 2560:2568], w[:, 3592:3600]], axis=1)
    wp = jnp.concatenate([w[:, 0:2560], w[:, 2568:3592], gates,
                          jnp.zeros((D_MODEL, LANES - 16), F32)], axis=1).astype(BF16)
    p1 = jnp.zeros((1, LANES), F32).at[0, 0:8].set(fox_bf[l]).at[0, 8:12].set(gdn_dt_bias[l])
    p2 = jnp.zeros((1, LANES), F32).at[0, 8:12].set(gdn_a_log[l])
    wr = jnp.concatenate([w_router[l], w_group[l],
                          jnp.zeros((D_MODEL, LANES - N_GROUPS - N_EXPERTS), F32)], axis=1).astype(BF16)
    br = jnp.zeros((1, LANES), F32).at[0, 0:N_EXPERTS].set(b_router[l])
    br = br.at[0, N_EXPERTS:N_EXPERTS + N_GROUPS].set(b_group[l])
    return dict(
        ln_mix=ln_mix[l][None, :], w=wp, p1=p1, p2=p2,
        qg=jnp.tile(fox_qnorm[l], B_HEADS)[None, :], kg=jnp.tile(fox_knorm[l], B_HEADS)[None, :],
        hgrn_ng=jnp.tile(hgrn_norm[l], A_HEADS)[None, :], gdn_ng=jnp.tile(gdn_norm[l], C_HEADS)[None, :],
        conv=gdn_conv[l], w_out=w_out[l].astype(BF16), ln_ffn=ln_ffn[l][None, :], wr=wr, br=br)


def kernel(x_prompt, x_sample, cache_k, cache_v, cache_logf, page_table, state_hgrn, state_gdn, state_conv,
           ln_mix, w_in, hgrn_lb, hgrn_norm, fox_bf, fox_qnorm, fox_knorm, gdn_conv, gdn_a_log, gdn_dt_bias,
           gdn_norm, w_out, ln_ffn, w_group, b_group, w_router, b_router, w_gate, w_up, w_down):
    nb, seq, _ = x_prompt.shape
    nsq, dec, _ = x_sample.shape
    depth = ln_mix.shape[0]
    n_phys, page = cache_k.shape[1], cache_k.shape[2]
    tp = nb * seq
    ts = nsq * dec
    hs512 = _const(_head_sum_matrix(512))
    ck = jnp.transpose(cache_k, (0, 1, 3, 4, 2)).reshape(depth, n_phys, 512, page)
    cv = jnp.transpose(cache_v, (0, 1, 3, 4, 2)).reshape(depth, n_phys, 512, page)
    clt = jnp.swapaxes(cache_logf, 2, 3)
    tmp_ = min(ROW_TILE, tp)
    tms_ = min(ROW_TILE, ts)
    tbp = min(ROW_TILE, seq)
    gchunk = min(GDN_CHUNK, seq)
    seq_blk = SAMPLE_SEQS

    wg = w_gate.astype(BF16).reshape(depth * N_EXPERTS, D_MODEL, D_EXPERT)
    wu = w_up.astype(BF16).reshape(depth * N_EXPERTS, D_MODEL, D_EXPERT)
    wd = w_down.astype(BF16).reshape(depth * N_EXPERTS, D_EXPERT, D_MODEL)

    rows_p = jnp.zeros((_moe_rows(tp), D_MODEL), F32)
    rows_s = jnp.zeros((_moe_rows(ts), D_MODEL), F32)

    yp = x_prompt.reshape(tp, D_MODEL)
    ys = x_sample.reshape(ts, D_MODEL)
    outs_p, outs_s = [], []
    for l in range(depth):
        P = _layer_params(l, ln_mix, w_in, hgrn_norm, fox_bf, fox_qnorm, fox_knorm, gdn_conv, gdn_a_log,
                          gdn_dt_bias, gdn_norm, w_out, ln_ffn, w_group, b_group, w_router, b_router,
                          w_gate, w_up, w_down)

        a, _, bk, bv, c, g, qa, ka = _proj(yp, P["ln_mix"], P["w"], hs512, P["qg"], P["kg"], P["p1"],
                                           P["p2"], tm=tbp, bps=seq // tbp)
        ob = _fox(qa, ka, bv, nb, tq=tbp)
        oa, hst = _hgrn(a, hgrn_lb, P["hgrn_ng"], None, l, nb, seq // tbp, tbp, 1, tbp // SUB, SUB)
        oc, gst = _gdn(c, g, P["conv"], P["gdn_ng"], None, None, nb, seq // tbp, tbp, 1,
                       tbp // gchunk, gchunk)
        xn, hf, gates, sel, rank, cnt = _post(yp, oa, ob, oc, P["w_out"], P["ln_ffn"], P["wr"], P["br"],
                                              tm=tmp_)
        yp_new, rows_p = _moe(hf, gates, sel, rank, cnt, xn, wg, wu, wd, l, rows_p, tm=tmp_)
        outs_p.append((bk.reshape(nb, seq, B_HEADS, HD), bv.reshape(nb, seq, B_HEADS, HD),
                       g[:, 0:8].reshape(nb, seq, B_HEADS), hst, gst,
                       c.reshape(nb, seq, 1024)[:, seq - (CONV_W - 1):, 0:768]))
        yp = yp_new

        a, bq, bk, bv, c, g, _, _ = _proj(ys, P["ln_mix"], P["w"], hs512, P["qg"], P["kg"], P["p1"],
                                          P["p2"], tm=tms_, bps=1)
        knew = jnp.pad(jnp.swapaxes(bk.reshape(nsq, dec, 512), 1, 2), ((0, 0), (0, 0), (0, SUB - dec)))
        vnew = jnp.pad(jnp.swapaxes(bv.reshape(nsq, dec, 512), 1, 2), ((0, 0), (0, 0), (0, SUB - dec)))
        lfn = jnp.pad(jnp.swapaxes(g[:, 0:8].reshape(nsq, dec, B_HEADS), 1, 2),
                      ((0, 0), (0, 0), (0, SUB - dec)))
        ob = _foxdec(l, page_table, bq.reshape(nsq, dec, 512), knew, vnew, lfn, ck, cv, clt,
                     pp=min(DECODE_PAGES, page_table.shape[1])).reshape(ts, 512)
        n_outer = nsq // seq_blk
        oa, hst = _hgrn(_pad_rows(a, nsq, dec), hgrn_lb, P["hgrn_ng"], state_hgrn[l], l,
                        n_outer, 1, seq_blk * SUB, seq_blk, 1, dec)
        cinit = jnp.pad(state_conv[l], ((0, 0), (8 - (CONV_W - 1), 0), (0, 0)))
        oc, gst = _gdn(_pad_rows(c, nsq, dec), _pad_rows(g, nsq, dec), P["conv"], P["gdn_ng"], cinit,
                       state_gdn[l], n_outer, 1, seq_blk * SUB, seq_blk, 1, SUB)
        oa = oa.reshape(nsq, SUB, 256)[:, :dec].reshape(ts, 256)
        oc = oc.reshape(nsq, SUB, 256)[:, :dec].reshape(ts, 256)
        xn, hf, gates, sel, rank, cnt = _post(ys, oa, ob, oc, P["w_out"], P["ln_ffn"], P["wr"], P["br"],
                                              tm=tms_)
        ys_new, rows_s = _moe(hf, gates, sel, rank, cnt, xn, wg, wu, wd, l, rows_s, tm=tms_)
        xpad = jnp.concatenate([state_conv[l], c.reshape(nsq, dec, 1024)[:, :, 0:768]], axis=1)
        outs_s.append((bk.reshape(nsq, dec, B_HEADS, HD), bv.reshape(nsq, dec, B_HEADS, HD),
                       g[:, 0:8].reshape(nsq, dec, B_HEADS), hst, gst, xpad[:, -(CONV_W - 1):, :]))
        ys = ys_new

    stack = lambda outs, j: jnp.stack([o[j] for o in outs], axis=0)
    return (yp.reshape(nb, seq, D_MODEL), ys.reshape(nsq, dec, D_MODEL),
            *[stack(outs_p, j) for j in range(6)], *[stack(outs_s, j) for j in range(6)])
```

```python
import functools
import math

import numpy as np
import jax
import jax.numpy as jnp
from jax import lax
from jax.experimental import pallas as pl
from jax.experimental.pallas import tpu as pltpu

F32 = jnp.float32
BF16 = jnp.bfloat16
EPS = 1e-6
NEG = -1e30

D_MODEL = 1024
HD = 64
A_HEADS = 4
B_HEADS = 8
C_HEADS = 4
CONV_W = 4
N_GROUPS = 4
EXPERTS_PER_GROUP = 8
N_EXPERTS = N_GROUPS * EXPERTS_PER_GROUP
D_EXPERT = D_MODEL // 4
SUB = 16
LANES = 128
VMEM_LIMIT = 56 * 1024 * 1024
ROW_TILE = 256
GDN_CHUNK = 64
SAMPLE_SEQS = 16
DECODE_PAGES = 16
MOE_DEST_TILE = 1024
MOE_DISPATCH_TILE = 512


def _cp(*sem):
    return pltpu.CompilerParams(dimension_semantics=sem, vmem_limit_bytes=VMEM_LIMIT)


def _dot(a, b):
    return jnp.dot(a, b, preferred_element_type=F32)


def _dot_nt(a, b):
    return lax.dot_general(a, b, (((1,), (1,)), ((), ())), preferred_element_type=F32)


def _dot_tn(a, b):
    return lax.dot_general(a, b, (((0,), (0,)), ((), ())), preferred_element_type=F32)


def _split3(x):
    hi = x.astype(BF16)
    r = x - hi.astype(F32)
    mid = r.astype(BF16)
    lo = (r - mid.astype(F32)).astype(BF16)
    return hi, mid, lo


def _split2(x):
    hi = x.astype(BF16)
    return hi, (x - hi.astype(F32)).astype(BF16)


def _exact_left(m, x):
    hi, mid, lo = _split3(x)
    return (_dot(m, hi) + _dot(m, mid)) + _dot(m, lo)


def _exact_right(x, m):
    hi, mid, lo = _split3(x)
    return (_dot(hi, m) + _dot(mid, m)) + _dot(lo, m)


def _idiv(x, n):
    return jnp.right_shift(x, int(math.log2(n)))


def _imod(x, n):
    return jnp.bitwise_and(x, n - 1)


def _sigmoid(x):
    return 1.0 / (1.0 + jnp.exp(-x))


def _softplus(z):
    return jnp.maximum(z, 0.0) + jnp.log(1.0 + jnp.exp(-jnp.abs(z)))


def _head_ssq(z, hs):
    zz = z * z
    hi = zz.astype(BF16)
    lo = (zz - hi.astype(F32)).astype(BF16)
    return _dot(hi, hs) + _dot(lo, hs)


def _const(a, dtype=BF16):
    return jnp.asarray(a, dtype=dtype)


def _head_sum_matrix(width):
    i = np.arange(width)
    return (i[:, None] // HD == i[None, :] // HD).astype(np.float32)


def _chunk_tril(n, c):
    i = np.arange(n)
    return ((i[:, None] // c == i[None, :] // c) & (i[None, :] <= i[:, None])).astype(np.float32)


PROJ_COLS = 3712


def _proj_kernel(x_ref, ln_ref, w_ref, hs_ref, qg_ref, kg_ref, p1_ref, p2_ref,
                 tril_ref, pq_ref, pcq_ref, pck_ref, oq_ref, ok_ref,
                 a_ref, bq_ref, bk_ref, bv_ref, c_ref, g_ref, qa_ref, ka_ref, carry_ref, *, bps):
    x = x_ref[...]
    ms = jnp.mean(x * x, axis=-1, keepdims=True)
    hn = ((x * lax.rsqrt(ms + EPS)) * ln_ref[...]).astype(BF16)
    a_ref[...] = _dot(hn, w_ref[:, 0:1024])
    hs = hs_ref[...]
    q = _dot(hn, w_ref[:, 1024:1536])
    bq = ((q * lax.rsqrt(_head_ssq(q, hs) * (1.0 / HD) + EPS)) * qg_ref[...]) * (HD ** -0.5)
    bq_ref[...] = bq
    k = _dot(hn, w_ref[:, 1536:2048])
    bk = (k * lax.rsqrt(_head_ssq(k, hs) * (1.0 / HD) + EPS)) * kg_ref[...]
    bk_ref[...] = bk
    bv_ref[...] = _dot(hn, w_ref[:, 2048:2560])
    c_ref[...] = _dot(hn, w_ref[:, 2560:3584])
    gr = _dot(hn, w_ref[:, 3584:3712])
    lane = lax.broadcasted_iota(jnp.int32, gr.shape, 1)
    z = gr + p1_ref[...]
    sp = _softplus(z)
    logsig = jnp.minimum(z, 0.0) - jnp.log(1.0 + jnp.exp(-jnp.abs(z)))
    glog = -jnp.exp(p2_ref[...]) * sp
    beta = _sigmoid(gr)
    g = jnp.where(lane < 8, logsig, jnp.where(lane < 12, glog, jnp.where(lane < 16, beta, 0.0)))
    g_ref[...] = g

    @pl.when(pl.program_id(0) % bps == 0)
    def _():
        carry_ref[...] = jnp.zeros_like(carry_ref)

    c = _exact_left(tril_ref[...], g) + carry_ref[...]
    carry_ref[...] = c[-1:, :]
    hi, mid, lo = _split3(c)
    pq = pq_ref[...]
    qa = _dot(bq.astype(BF16), pq) + oq_ref[...]
    ka = _dot(bk.astype(BF16), pq) + ok_ref[...]
    for j, part in enumerate((hi, mid, lo)):
        qa = qa + _dot(part, pcq_ref[j])
        ka = ka - _dot(part, pck_ref[j])
    qa_ref[...] = qa.astype(BF16)
    ka_ref[...] = ka.astype(BF16)


def _proj(x, ln, w, hs512, qg, kg, p1, p2, tm, bps):
    t = x.shape[0]
    pq = np.zeros((512, 1024), np.float32)
    pcq = np.zeros((3, LANES, 1024), np.float32)
    pck = np.zeros((3, LANES, 1024), np.float32)
    oq = np.zeros((1, 1024), np.float32)
    ok = np.zeros((1, 1024), np.float32)
    for h in range(B_HEADS):
        pq[h * HD + np.arange(HD), h * LANES + np.arange(HD)] = 1.0
        for j in range(3):
            pcq[j, h, h * LANES + HD + j] = 1.0
            pck[j, h, h * LANES + HD + 3 + j] = 1.0
            oq[0, h * LANES + HD + 3 + j] = 1.0
            ok[0, h * LANES + HD + j] = 1.0
    row = lambda i: (i, 0)
    fix = lambda i: (0, 0)
    fix3 = lambda i: (0, 0, 0)
    widths = (1024, 512, 512, 512, 1024, 128)
    outs = [jax.ShapeDtypeStruct((t, n), F32) for n in widths] + [jax.ShapeDtypeStruct((t, 1024), BF16)] * 2
    return pl.pallas_call(
        functools.partial(_proj_kernel, bps=bps),
        grid=(t // tm,),
        in_specs=[pl.BlockSpec((tm, D_MODEL), row), pl.BlockSpec((1, D_MODEL), fix),
                  pl.BlockSpec((D_MODEL, PROJ_COLS), fix), pl.BlockSpec((512, 512), fix),
                  pl.BlockSpec((1, 512), fix), pl.BlockSpec((1, 512), fix),
                  pl.BlockSpec((1, LANES), fix), pl.BlockSpec((1, LANES), fix),
                  pl.BlockSpec((tm, tm), fix), pl.BlockSpec((512, 1024), fix),
                  pl.BlockSpec((3, LANES, 1024), fix3), pl.BlockSpec((3, LANES, 1024), fix3),
                  pl.BlockSpec((1, 1024), fix), pl.BlockSpec((1, 1024), fix)],
        out_specs=[pl.BlockSpec((tm, n), row) for n in widths + (1024, 1024)],
        out_shape=outs,
        scratch_shapes=[pltpu.VMEM((1, LANES), F32)],
        compiler_params=_cp("arbitrary"),
        name="proj",
    )(x, ln, w, hs512, qg, kg, p1, p2, _const(_chunk_tril(tm, tm)), _const(pq), _const(pcq), _const(pck),
      _const(oq, F32), _const(ok, F32))


def _fox_kernel(qt_ref, kt_ref, q_ref, k_ref, v_ref, o_ref, m_ref, acc_ref, *, tq):
    j = pl.program_id(1)
    qi = qt_ref[j]
    ki = kt_ref[j]

    @pl.when(ki == 0)
    def _():
        m_ref[...] = jnp.full_like(m_ref, NEG)
        acc_ref[...] = jnp.zeros_like(acc_ref)

    lane = lax.broadcasted_iota(jnp.int32, (tq, LANES), 1)
    left = lane < HD
    ones0 = jnp.where(lane == HD, 1.0, 0.0)
    ones1 = jnp.where(lane == 0, 1.0, 0.0)

    def update(masked):
        if masked:
            rows = lax.broadcasted_iota(jnp.int32, (tq, tq), 0)
            cols = lax.broadcasted_iota(jnp.int32, (tq, tq), 1)
            keep = cols <= rows
        for pr in range(B_HEADS // 2):
            pair = (2 * pr, 2 * pr + 1)
            ss = [_dot_nt(q_ref[:, h * LANES:(h + 1) * LANES], k_ref[:, h * LANES:(h + 1) * LANES])
                  for h in pair]
            if masked:
                ss = [jnp.where(keep, s, NEG) for s in ss]
            ps, alphas = [], []
            for s, h in zip(ss, pair):
                m_old = m_ref[h]
                m_new = jnp.maximum(m_old, jnp.max(s, axis=1, keepdims=True))
                ps.append(jnp.exp(s - m_new[:, 0:1]).astype(BF16))
                alphas.append(jnp.exp(m_old - m_new))
                m_ref[h] = m_new
            vp = v_ref[:, pr * LANES:(pr + 1) * LANES]
            vms = (jnp.where(left, vp, ones0).astype(BF16), jnp.where(left, ones1, vp).astype(BF16))
            for hh, h in enumerate(pair):
                acc_ref[h] = alphas[hh] * acc_ref[h] + _dot(ps[hh], vms[hh])

    @pl.when(ki < qi)
    def _():
        update(False)

    @pl.when(ki == qi)
    def _():
        update(True)
        for pr in range(B_HEADS // 2):
            a0 = acc_ref[2 * pr]
            a1 = acc_ref[2 * pr + 1]
            o_ref[:, pr * LANES:(pr + 1) * LANES] = jnp.where(left, a0 / a0[:, HD:HD + 1], a1 / a1[:, 0:1])


def _fox(qa, ka, bv, nb, tq):
    t = qa.shape[0]
    nblk = t // nb // tq
    qi_tab = np.array([q for q in range(nblk) for _ in range(q + 1)], np.int32)
    ki_tab = np.array([k for q in range(nblk) for k in range(q + 1)], np.int32)
    grid_spec = pltpu.PrefetchScalarGridSpec(
        num_scalar_prefetch=2,
        grid=(nb, len(qi_tab)),
        in_specs=[pl.BlockSpec((tq, 1024), lambda b, j, qt, kt: (b * nblk + qt[j], 0)),
                  pl.BlockSpec((tq, 1024), lambda b, j, qt, kt: (b * nblk + kt[j], 0)),
                  pl.BlockSpec((tq, 512), lambda b, j, qt, kt: (b * nblk + kt[j], 0))],
        out_specs=pl.BlockSpec((tq, 512), lambda b, j, qt, kt: (b * nblk + qt[j], 0)),
        scratch_shapes=[pltpu.VMEM((B_HEADS, tq, LANES), F32), pltpu.VMEM((B_HEADS, tq, LANES), F32)],
    )
    return pl.pallas_call(
        functools.partial(_fox_kernel, tq=tq),
        grid_spec=grid_spec,
        out_shape=jax.ShapeDtypeStruct((t, 512), F32),
        compiler_params=_cp("arbitrary", "arbitrary"),
        name="fox",
    )(jnp.asarray(qi_tab), jnp.asarray(ki_tab), qa, ka, bv)


def _foxdec_kernel(pt_ref, q_ref, kn_ref, vn_ref, lfn_ref, cum_ref, wide_ref, *rest, pp, dec):
    kp = rest[0:pp]
    vp = rest[pp:2 * pp]
    lp = rest[2 * pp:3 * pp]
    o_ref = rest[3 * pp]
    qr_ref, m_ref, l_ref, r_ref, acc_ref = rest[3 * pp + 1:]
    step = pl.program_id(1)
    nrow = dec * B_HEADS
    rowi = lax.broadcasted_iota(jnp.int32, (nrow, 512), 0)
    coli = lax.broadcasted_iota(jnp.int32, (nrow, 512), 1)
    bd = _idiv(coli, HD) == _imod(rowi, B_HEADS)

    @pl.when(step == 0)
    def _():
        q = q_ref[0]
        qrows = jnp.concatenate(
            [jnp.broadcast_to(q[t:t + 1, :], (B_HEADS, 512)) for t in range(dec)], axis=0)
        qr_ref[...] = jnp.where(bd, qrows, 0.0).astype(BF16)
        m_ref[...] = jnp.full_like(m_ref, NEG)
        l_ref[...] = jnp.zeros_like(l_ref)
        r_ref[...] = jnp.zeros_like(r_ref)
        acc_ref[...] = jnp.zeros_like(acc_ref)

    def pages(krefs, vrefs, lrefs, causal):
        n = len(krefs)
        qr = qr_ref[...]
        ss = [_dot(qr, kr().astype(BF16)) for kr in krefs]
        cums = [_exact_right(lr(), cum_ref[...]) for lr in lrefs]
        r = r_ref[...]
        logits = []
        for u in range(n):
            w = jnp.concatenate([cums[u][:, 0:LANES]] * dec, axis=0)
            logit = ss[u] - (r + w)
            if causal:
                key = lax.broadcasted_iota(jnp.int32, (nrow, LANES), 1)
                trow = _idiv(lax.broadcasted_iota(jnp.int32, (nrow, LANES), 0), B_HEADS)
                logit = jnp.where(key <= trow, logit, NEG)
            logits.append(logit)
            r = r + jnp.concatenate([cums[u][:, LANES:2 * LANES]] * dec, axis=0)
        r_ref[...] = r
        m_old = m_ref[...]
        m_new = m_old
        for lg in logits:
            m_new = jnp.maximum(m_new, jnp.max(lg, axis=1, keepdims=True))
        ps = [jnp.exp(lg - m_new) for lg in logits]
        alpha = jnp.exp(m_old - m_new)
        lsum = jnp.sum(ps[0], axis=1, keepdims=True)
        for p in ps[1:]:
            lsum = lsum + jnp.sum(p, axis=1, keepdims=True)
        l_ref[...] = alpha * l_ref[...] + lsum
        m_ref[...] = m_new
        pv = _dot_nt(ps[0].astype(BF16), vrefs[0]().astype(BF16))
        for u in range(1, n):
            pv = pv + _dot_nt(ps[u].astype(BF16), vrefs[u]().astype(BF16))
        acc_ref[...] = alpha[:, 0:1] * acc_ref[...] + pv

    pages([lambda u=u: kp[u][...] for u in range(pp)], [lambda u=u: vp[u][...] for u in range(pp)],
          [lambda u=u: lp[u][...] for u in range(pp)], False)

    @pl.when(step == pl.num_programs(1) - 1)
    def _():
        wide = wide_ref[...]
        knew = _dot(kn_ref[0].astype(BF16), wide)
        vnew = _dot(vn_ref[0].astype(BF16), wide)
        lfnew = _exact_right(lfn_ref[0], wide)
        pages([lambda: knew], [lambda: vnew], [lambda: lfnew], True)
        res = jnp.where(bd, acc_ref[...] / l_ref[:, 0:1], 0.0)
        for t in range(dec):
            o_ref[0, t:t + 1, :] = jnp.sum(res[t * B_HEADS:(t + 1) * B_HEADS], axis=0, keepdims=True)


def _foxdec(layer, page_table, q, knew, vnew, lfn_t, cache_k, cache_v, cache_lft, pp):
    ns, dec, _ = q.shape
    n_pages = page_table.shape[1]
    page = cache_k.shape[3]
    nsteps = n_pages // pp
    nrow = dec * B_HEADS
    cum = np.concatenate([np.triu(np.ones((page, page), np.float32)), np.ones((page, page), np.float32)], axis=1)
    seq3 = lambda s, p, pt: (s, 0, 0)
    fix2 = lambda s, p, pt: (0, 0)

    def cache_map(u):
        return lambda s, p, pt: (layer, pt[s, p * pp + u], 0, 0)

    wide = np.eye(SUB, page, dtype=np.float32)
    in_specs = [pl.BlockSpec((1, dec, 512), seq3), pl.BlockSpec((1, 512, SUB), seq3),
                pl.BlockSpec((1, 512, SUB), seq3), pl.BlockSpec((1, B_HEADS, SUB), seq3),
                pl.BlockSpec((page, 2 * page), fix2), pl.BlockSpec((SUB, page), fix2)]
    in_specs += [pl.BlockSpec((None, None, 512, page), cache_map(u)) for u in range(pp)]
    in_specs += [pl.BlockSpec((None, None, 512, page), cache_map(u)) for u in range(pp)]
    in_specs += [pl.BlockSpec((None, None, B_HEADS, page), cache_map(u)) for u in range(pp)]
    grid_spec = pltpu.PrefetchScalarGridSpec(
        num_scalar_prefetch=1,
        grid=(ns, nsteps),
        in_specs=in_specs,
        out_specs=pl.BlockSpec((1, dec, 512), seq3),
        scratch_shapes=[pltpu.VMEM((nrow, 512), BF16), pltpu.VMEM((nrow, LANES), F32),
                        pltpu.VMEM((nrow, LANES), F32), pltpu.VMEM((nrow, LANES), F32),
                        pltpu.VMEM((nrow, 512), F32)],
    )
    return pl.pallas_call(
        functools.partial(_foxdec_kernel, pp=pp, dec=dec),
        grid_spec=grid_spec,
        out_shape=jax.ShapeDtypeStruct((ns, dec, 512), F32),
        compiler_params=_cp("arbitrary", "arbitrary"),
        name="foxdec",
    )(page_table, q, knew, vnew, lfn_t, _const(cum), _const(wide),
      *([cache_k] * pp), *([cache_v] * pp), *([cache_lft] * pp))


def _hgrn_kernel(*refs, layer, ns, cps, valid, has_init):
    (q_ref, f_ref, v_ref, gate_ref, lbraw_ref, ng_ref, tril_ref, rs_ref, tile_ref, hs_ref) = refs[:10]
    pos = 10
    s0_ref = None
    if has_init:
        s0_ref = refs[pos]
        pos += 1
    o_ref, sout_ref, st_ref, oin_ref, oint_ref, tmp_ref = refs[pos:pos + 6]
    i = pl.program_id(1)
    tb = q_ref.shape[0]
    nchunk = tb // SUB
    width = A_HEADS * HD

    lbraw = lbraw_ref[...]
    e = jnp.exp(lbraw - jnp.max(lbraw, axis=0, keepdims=True))
    prob = e / jnp.sum(e, axis=0, keepdims=True)
    lb = jnp.zeros((1, width), F32)
    for d in range(1, layer + 1):
        lb = lb + prob[d:d + 1, :]

    fr = f_ref[...]
    qr = q_ref[...]
    vr = v_ref[...]
    logf = jnp.log(lb + (1.0 - lb) * _sigmoid(fr))
    key = (1.0 - lb) * _sigmoid(-fr)
    if valid < SUB:
        rowv = _imod(lax.broadcasted_iota(jnp.int32, (tb, width), 0), SUB) < valid
        logf = jnp.where(rowv, logf, 0.0)
        key = jnp.where(rowv, key, 0.0)
    q = (qr * _sigmoid(qr)) * (HD ** -0.5)
    g = _exact_left(tril_ref[...], logf)
    g3 = g.reshape(nchunk, SUB, width)
    q3 = q.reshape(nchunk, SUB, width)
    k3 = key.reshape(nchunk, SUB, width)

    att = jnp.zeros((tb, LANES), F32)
    for s in range(SUB):
        d = jnp.minimum(g3 - g3[:, s:s + 1, :], 0.0)
        p = (q3 * jnp.exp(d)) * k3[:, s:s + 1, :]
        att = att + _dot(p.reshape(tb, width).astype(BF16), rs_ref[s])
    attb = att.astype(BF16)

    rows = lax.broadcasted_iota(jnp.int32, (tb, tb), 0)
    cols = lax.broadcasted_iota(jnp.int32, (tb, tb), 1)
    causal = (_idiv(rows, SUB) == _idiv(cols, SUB)) & (cols <= rows)
    lane = lax.broadcasted_iota(jnp.int32, (tb, LANES), 1)
    for pr in range(2):
        vpair = vr[:, pr * LANES:(pr + 1) * LANES]
        acc = None
        for hh in range(2):
            h = 2 * pr + hh
            full = _dot(attb, tile_ref[h])
            full = jnp.where(causal, full, 0.0).astype(BF16)
            vm = jnp.where((lane < HD) == (hh == 0), vpair, 0.0).astype(BF16)
            term = _dot(full, vm)
            acc = term if acc is None else acc + term
        oin_ref[:, pr * LANES:(pr + 1) * LANES] = acc

    gend3 = g3[:, SUB - 1:SUB, :]
    kd, kdl = _split2((k3 * jnp.exp(gend3 - g3)).reshape(tb, width))
    eg = jnp.exp(g)
    qg = (q * eg).astype(BF16)
    r128 = lax.broadcasted_iota(jnp.int32, (LANES, LANES), 0)
    c128 = lax.broadcasted_iota(jnp.int32, (LANES, LANES), 1)
    bdm = _idiv(r128, HD) == _idiv(c128, HD)
    vb, vbl = _split2(vr)
    last = i == pl.num_programs(1) - 1
    units = [(sq, pr) for sq in range(ns) for pr in range(2)]
    rowsl = lambda sq, c: slice((sq * cps + c) * SUB, (sq * cps + c + 1) * SUB)
    lanesl = lambda pr: slice(pr * LANES, (pr + 1) * LANES)
    incs = {}
    for sq, pr in units:
        for c in range(cps):
            rs_, ls_ = rowsl(sq, c), lanesl(pr)
            u = (_dot_tn(vb[rs_, ls_], kd[rs_, ls_]) + _dot_tn(vb[rs_, ls_], kdl[rs_, ls_])) \
                + _dot_tn(vbl[rs_, ls_], kd[rs_, ls_])
            incs[sq, pr, c] = jnp.where(bdm, u, 0.0)
    if has_init:
        tmp_ref[...] = jnp.zeros_like(tmp_ref)
        for sq, pr in units:
            tmp_ref[2 * sq + pr, 0:HD, 0:HD] = s0_ref[sq, 2 * pr]
            tmp_ref[2 * sq + pr, HD:LANES, HD:LANES] = s0_ref[sq, 2 * pr + 1]
    seen = {}
    final = {}
    for sq, pr in units:
        if has_init:
            st = tmp_ref[2 * sq + pr].T
        else:
            st = jnp.where(i == 0, 0.0, st_ref[pr])
        for c in range(cps):
            seen[sq, pr, c] = st.astype(BF16)
            r_end = (sq * cps + c + 1) * SUB
            st = st * eg[r_end - 1:r_end, lanesl(pr)] + incs[sq, pr, c]
        if not has_init:
            st_ref[pr] = st
        final[sq, pr] = st
    for sq, pr in units:
        for c in range(cps):
            oint_ref[rowsl(sq, c), lanesl(pr)] = _dot_nt(qg[rowsl(sq, c), lanesl(pr)], seen[sq, pr, c])

    @pl.when(last)
    def _():
        for sq, pr in units:
            tmp_ref[2 * sq + pr] = final[sq, pr].T
        for sq, pr in units:
            sout_ref[sq, 2 * pr] = tmp_ref[2 * sq + pr, 0:HD, 0:HD]
            sout_ref[sq, 2 * pr + 1] = tmp_ref[2 * sq + pr, HD:LANES, HD:LANES]

    o = oin_ref[...] + oint_ref[...]
    gr = gate_ref[...]
    o = (o * lax.rsqrt(_head_ssq(o, hs_ref[...]) * (1.0 / HD) + EPS)) * ng_ref[...]
    o_ref[...] = o * (gr * _sigmoid(gr))


def _hgrn(a, lbraw, ng, s0, layer, n_outer, n_inner, tb, ns, cps, valid):
    rows = a.shape[0]
    width = A_HEADS * HD
    has_init = s0 is not None
    nseq = n_outer * ns
    rs = np.zeros((SUB, width, LANES), np.float32)
    tile = np.zeros((A_HEADS, LANES, tb), np.float32)
    for s in range(SUB):
        for h in range(A_HEADS):
            rs[s, h * HD + np.arange(HD), h * SUB + s] = 1.0
            tile[h, h * SUB + s, np.arange(tb // SUB) * SUB + s] = 1.0

    def col(j):
        return lambda o, i: (o * n_inner + i, j)

    fix2 = lambda o, i: (0, 0)
    fix3 = lambda o, i: (0, 0, 0)
    in_specs = [pl.BlockSpec((tb, width), col(j)) for j in range(4)]
    in_specs += [pl.BlockSpec(lbraw.shape, fix2), pl.BlockSpec((1, width), fix2),
                 pl.BlockSpec((tb, tb), fix2), pl.BlockSpec((SUB, width, LANES), fix3),
                 pl.BlockSpec((A_HEADS, LANES, tb), fix3), pl.BlockSpec((width, width), fix2)]
    args = [a, a, a, a, lbraw, ng, _const(_chunk_tril(tb, SUB)), _const(rs), _const(tile),
            _const(_head_sum_matrix(width))]
    if has_init:
        in_specs.append(pl.BlockSpec((ns, A_HEADS, HD, HD), lambda o, i: (o, 0, 0, 0)))
        args.append(s0)
    return pl.pallas_call(
        functools.partial(_hgrn_kernel, layer=layer, ns=ns, cps=cps, valid=valid, has_init=has_init),
        grid=(n_outer, n_inner),
        in_specs=in_specs,
        out_specs=[pl.BlockSpec((tb, width), lambda o, i: (o * n_inner + i, 0)),
                   pl.BlockSpec((ns, A_HEADS, HD, HD), lambda o, i: (o, 0, 0, 0))],
        out_shape=[jax.ShapeDtypeStruct((rows, width), F32),
                   jax.ShapeDtypeStruct((nseq, A_HEADS, HD, HD), F32)],
        scratch_shapes=[pltpu.VMEM((2, LANES, LANES), F32), pltpu.VMEM((tb, width), F32),
                        pltpu.VMEM((tb, width), F32), pltpu.VMEM((2 * ns, LANES, LANES), F32)],
        compiler_params=_cp("arbitrary", "arbitrary"),
        name="hgrn",
    )(*args)


def _gdn_kernel(*refs, ns, cps, chunk, has_init):
    (x_ref, z_ref, g_ref, cw_ref, ng_ref, tril_ref, hs_ref, eg_ref, eb_ref, eg2_ref, eb2_ref) = refs[:11]
    pos = 11
    cinit_ref = s0_ref = None
    if has_init:
        cinit_ref, s0_ref = refs[pos:pos + 2]
        pos += 2
    o_ref, sout_ref, st_ref, cv_ref, oacc_ref, vn_ref, tmp_ref = refs[pos:pos + 7]
    i = pl.program_id(1)
    tb = x_ref.shape[0]
    width = C_HEADS * HD
    rows_seq = cps * chunk
    last = i == pl.num_programs(1) - 1

    x = x_ref[...]
    cw = cw_ref[...]
    acts = []
    for sq in range(ns):
        xs = x[sq * rows_seq:(sq + 1) * rows_seq]
        if has_init:
            prev = cinit_ref[sq]
        else:
            prev = jnp.where(i == 0, 0.0, cv_ref[...])
        xc = jnp.concatenate([prev, xs], axis=0)
        conv = xc * cw[CONV_W - 1:CONV_W, :]
        for j in range(1, CONV_W):
            conv = conv + pltpu.roll(xc, j, 0) * cw[CONV_W - 1 - j:CONV_W - j, :]
        acts.append(conv[8:])
        if not has_init:
            cv_ref[...] = xs[rows_seq - 8:]
    conv = acts[0] if ns == 1 else jnp.concatenate(acts, axis=0)
    act = conv * _sigmoid(conv)
    hs = hs_ref[...]
    q = act[:, 0:width]
    k = act[:, width:2 * width]
    v = act[:, 2 * width:3 * width]
    q = (q * lax.rsqrt(_head_ssq(q, hs) + EPS)) * (HD ** -0.5)
    k = k * lax.rsqrt(_head_ssq(k, hs) + EPS)

    gt = g_ref[...]
    gc = _exact_left(tril_ref[...], gt)
    gexp = _exact_right(gc, eg_ref[...])
    bexp = _exact_right(gt, eb_ref[...])
    gcol = _exact_right(gc, eg2_ref[...])
    bcol = _exact_right(gt, eb2_ref[...])
    nchunk = tb // chunk
    gexp3 = gexp.reshape(nchunk, chunk, width)
    gend3 = gexp3[:, chunk - 1:chunk, :]
    eg = jnp.exp(gexp)
    rhs = jnp.concatenate([v * bexp, (k * bexp) * eg], axis=1).astype(BF16)
    qg = (q * eg).astype(BF16)
    kd = (k.reshape(nchunk, chunk, width) * jnp.exp(gend3 - gexp3)).reshape(tb, width).astype(BF16)
    kb = k.astype(BF16)
    lane = lax.broadcasted_iota(jnp.int32, (tb, width), 1)
    kmask = [jnp.where(_idiv(lane, HD) == h, k, 0.0).astype(BF16) for h in range(C_HEADS)]
    qmask = [jnp.where(_idiv(lane, HD) == h, q, 0.0).astype(BF16) for h in range(C_HEADS)]

    rc = lax.broadcasted_iota(jnp.int32, (tb, tb), 0)
    cc = lax.broadcasted_iota(jnp.int32, (tb, tb), 1)
    same = _idiv(rc, chunk) == _idiv(cc, chunk)
    eye = rc == cc
    incl = same & (cc <= rc)
    strict = same & (cc < rc)
    hl = _idiv(lane, HD)
    r256 = lax.broadcasted_iota(jnp.int32, (width, width), 0)
    c256 = lax.broadcasted_iota(jnp.int32, (width, width), 1)
    bdm = _idiv(r256, HD) == _idiv(c256, HD)
    nstage = int(math.log2(chunk))
    heads = range(C_HEADS)
    reps = tb // LANES
    gct = gc.T
    decay, xm, tm, qkd = [], [], [], []
    for h in heads:
        g_t = jnp.concatenate([gcol[:, h * LANES:(h + 1) * LANES]] * reps, axis=1)
        decay.append(jnp.exp(jnp.minimum(g_t - gct[8 + h:9 + h, :], 0.0)))
    kk = [_dot_nt(kmask[h], kb) for h in heads]
    qk = [_dot_nt(qmask[h], kb) for h in heads]
    for h in heads:
        b_t = jnp.concatenate([bcol[:, h * LANES:(h + 1) * LANES]] * reps, axis=1)
        x0 = -jnp.where(strict, (b_t * kk[h]) * decay[h], 0.0)
        xm.append(x0)
        tm.append(jnp.where(eye, 1.0, 0.0) + x0)
        qkd.append(jnp.where(incl, qk[h] * decay[h], 0.0).astype(BF16))
    for _ in range(nstage - 1):
        xb = [x.astype(BF16) for x in xm]
        xm = [_dot(xb[h], xb[h]) for h in heads]
        tm = [tm[h] + _dot(tm[h].astype(BF16), xm[h].astype(BF16)) for h in heads]
    sol = [_dot(tm[h].astype(BF16), rhs) for h in heads]
    u = jnp.zeros((tb, width), F32)
    w = jnp.zeros((tb, width), F32)
    for h in heads:
        u = u + jnp.where(hl == h, sol[h][:, 0:width], 0.0)
        w = w + jnp.where(hl == h, sol[h][:, width:2 * width], 0.0)
    wb = w.astype(BF16)

    if has_init:
        tmp_ref[...] = jnp.zeros_like(tmp_ref)
        for sq in range(ns):
            for h in heads:
                tmp_ref[sq, h * HD:(h + 1) * HD, h * HD:(h + 1) * HD] = s0_ref[sq, h]
        sts = [tmp_ref[sq] for sq in range(ns)]
    else:
        sts = [jnp.where(i == 0, 0.0, st_ref[...])]
    rowsl = lambda sq, c: slice((sq * cps + c) * chunk, (sq * cps + c + 1) * chunk)
    for c in range(cps):
        stb = [st.astype(BF16) for st in sts]
        ws = [_dot(wb[rowsl(sq, c)], stb[sq]) for sq in range(ns)]
        for sq in range(ns):
            oacc_ref[rowsl(sq, c), :] = _dot(qg[rowsl(sq, c)], stb[sq])
        vns = [(u[rowsl(sq, c)] - ws[sq]).astype(BF16) for sq in range(ns)]
        for sq in range(ns):
            vn_ref[rowsl(sq, c), :] = vns[sq]
        incs = [jnp.where(bdm, _dot_tn(kd[rowsl(sq, c)], vns[sq]), 0.0) for sq in range(ns)]
        sts = [sts[sq] * eg[(sq * cps + c + 1) * chunk - 1:(sq * cps + c + 1) * chunk, :] + incs[sq]
               for sq in range(ns)]
    if not has_init:
        st_ref[...] = sts[0]

    @pl.when(last)
    def _():
        for sq in range(ns):
            tmp_ref[sq] = sts[sq]
        for sq in range(ns):
            for h in heads:
                sout_ref[sq, h] = tmp_ref[sq, h * HD:(h + 1) * HD, h * HD:(h + 1) * HD]

    vnb = vn_ref[...]
    o = oacc_ref[...]
    for h in heads:
        o = o + jnp.where(hl == h, _dot(qkd[h], vnb), 0.0)
    z = z_ref[...]
    o = (o * lax.rsqrt(_head_ssq(o, hs) * (1.0 / HD) + EPS)) * ng_ref[...]
    o_ref[...] = o * (z * _sigmoid(z))


def _gdn(cfull, g, cw, ng, cinit, s0, n_outer, n_inner, tb, ns, cps, chunk):
    rows = cfull.shape[0]
    width = C_HEADS * HD
    has_init = s0 is not None
    nseq = n_outer * ns
    eg = np.zeros((LANES, width), np.float32)
    eb = np.zeros((LANES, width), np.float32)
    eg2 = np.zeros((LANES, C_HEADS * LANES), np.float32)
    eb2 = np.zeros((LANES, C_HEADS * LANES), np.float32)
    for h in range(C_HEADS):
        eg[8 + h, h * HD:(h + 1) * HD] = 1.0
        eb[12 + h, h * HD:(h + 1) * HD] = 1.0
        eg2[8 + h, h * LANES:(h + 1) * LANES] = 1.0
        eb2[12 + h, h * LANES:(h + 1) * LANES] = 1.0
    rowm = lambda o, i: (o * n_inner + i, 0)
    fix2 = lambda o, i: (0, 0)
    in_specs = [pl.BlockSpec((tb, 3 * width), rowm),
                pl.BlockSpec((tb, width), lambda o, i: (o * n_inner + i, 3)),
                pl.BlockSpec((tb, LANES), rowm), pl.BlockSpec((CONV_W, 3 * width), fix2),
                pl.BlockSpec((1, width), fix2), pl.BlockSpec((tb, tb), fix2),
                pl.BlockSpec((width, width), fix2), pl.BlockSpec((LANES, width), fix2),
                pl.BlockSpec((LANES, width), fix2), pl.BlockSpec((LANES, C_HEADS * LANES), fix2),
                pl.BlockSpec((LANES, C_HEADS * LANES), fix2)]
    args = [cfull, cfull, g, cw, ng, _const(_chunk_tril(tb, chunk)), _const(_head_sum_matrix(width)),
            _const(eg), _const(eb), _const(eg2), _const(eb2)]
    if has_init:
        in_specs += [pl.BlockSpec((ns, 8, 3 * width), lambda o, i: (o, 0, 0)),
                     pl.BlockSpec((ns, C_HEADS, HD, HD), lambda o, i: (o, 0, 0, 0))]
        args += [cinit, s0]
    return pl.pallas_call(
        functools.partial(_gdn_kernel, ns=ns, cps=cps, chunk=chunk, has_init=has_init),
        grid=(n_outer, n_inner),
        in_specs=in_specs,
        out_specs=[pl.BlockSpec((tb, width), rowm),
                   pl.BlockSpec((ns, C_HEADS, HD, HD), lambda o, i: (o, 0, 0, 0))],
        out_shape=[jax.ShapeDtypeStruct((rows, width), F32),
                   jax.ShapeDtypeStruct((nseq, C_HEADS, HD, HD), F32)],
        scratch_shapes=[pltpu.VMEM((width, width), F32), pltpu.VMEM((8, 3 * width), F32),
                        pltpu.VMEM((tb, width), F32), pltpu.VMEM((tb, width), BF16),
                        pltpu.VMEM((ns, width, width), F32)],
        compiler_params=_cp("arbitrary", "arbitrary"),
        name="gdn",
    )(*args)


def _post_kernel(x_ref, oa_ref, ob_ref, oc_ref, wo_ref, ln_ref, wr_ref, br_ref, tril_ref,
                 xn_ref, hf_ref, gate_ref, sel_ref, rank_ref, cnt_ref, carry_ref):
    x = x_ref[...]
    mix = _dot(oa_ref[...].astype(BF16), wo_ref[0:256, :])
    mix = mix + _dot(ob_ref[...].astype(BF16), wo_ref[256:768, :])
    mix = mix + _dot(oc_ref[...].astype(BF16), wo_ref[768:1024, :])
    xn = x + mix
    xn_ref[...] = xn
    ms = jnp.mean(xn * xn, axis=-1, keepdims=True)
    hf = (xn * lax.rsqrt(ms + EPS)) * ln_ref[...]
    hf_ref[...] = hf
    logits = _dot(hf.astype(BF16), wr_ref[...]) + br_ref[...]
    lane = lax.broadcasted_iota(jnp.int32, logits.shape, 1)
    big = jnp.int32(1 << 20)
    isg = (lane >= N_EXPERTS) & (lane < N_EXPERTS + N_GROUPS)
    gl = jnp.where(isg, logits, NEG)
    gm = jnp.max(gl, axis=1, keepdims=True)
    gidx = jnp.min(jnp.where(isg & (gl == gm), lane, big), axis=1, keepdims=True) - N_EXPERTS
    top_gp = 1.0 / jnp.sum(jnp.where(isg, jnp.exp(gl - gm), 0.0), axis=1, keepdims=True)
    ing = (lane < N_EXPERTS) & (_idiv(lane, EXPERTS_PER_GROUP) == gidx)
    el = jnp.where(ing, logits, NEG)
    em = jnp.max(el, axis=1, keepdims=True)
    ee = jnp.where(ing, jnp.exp(el - em), 0.0)
    prob = ee / jnp.sum(ee, axis=1, keepdims=True)
    p1 = jnp.max(prob, axis=1, keepdims=True)
    i1 = jnp.min(jnp.where(ing & (prob == p1), lane, big), axis=1, keepdims=True)
    rest = jnp.where(ing & (lane != i1), prob, -1.0)
    p2 = jnp.max(rest, axis=1, keepdims=True)
    i2 = jnp.min(jnp.where(ing & (lane != i1) & (rest == p2), lane, big), axis=1, keepdims=True)
    den = p1 + p2
    gate_ref[...] = jnp.where(lane == i1, (top_gp * p1) / den,
                              jnp.where(lane == i2, (top_gp * p2) / den, 0.0))
    sel = jnp.where((lane == i1) | (lane == i2), 1.0, 0.0)
    sel_ref[...] = sel

    @pl.when(pl.program_id(0) == 0)
    def _():
        carry_ref[...] = jnp.zeros_like(carry_ref)

    rank_ref[...] = _dot(tril_ref[...], sel.astype(BF16)) + carry_ref[...]
    carry_ref[...] = carry_ref[...] + jnp.sum(sel, axis=0, keepdims=True)
    cnt_ref[...] = carry_ref[...]


def _post(x, oa, ob, oc, wo, ln, wr, br, tm):
    t = x.shape[0]
    row = lambda i: (i, 0)
    fix = lambda i: (0, 0)
    return pl.pallas_call(
        _post_kernel,
        grid=(t // tm,),
        in_specs=[pl.BlockSpec((tm, D_MODEL), row), pl.BlockSpec((tm, 256), row),
                  pl.BlockSpec((tm, 512), row), pl.BlockSpec((tm, 256), row),
                  pl.BlockSpec((D_MODEL, D_MODEL), fix), pl.BlockSpec((1, D_MODEL), fix),
                  pl.BlockSpec((D_MODEL, LANES), fix), pl.BlockSpec((1, LANES), fix),
                  pl.BlockSpec((tm, tm), fix)],
        out_specs=[pl.BlockSpec((tm, D_MODEL), row), pl.BlockSpec((tm, D_MODEL), row),
                   pl.BlockSpec((tm, LANES), row), pl.BlockSpec((tm, LANES), row),
                   pl.BlockSpec((tm, LANES), row), pl.BlockSpec((1, LANES), fix)],
        out_shape=[jax.ShapeDtypeStruct((t, D_MODEL), F32), jax.ShapeDtypeStruct((t, D_MODEL), F32),
                   jax.ShapeDtypeStruct((t, LANES), F32), jax.ShapeDtypeStruct((t, LANES), F32),
                   jax.ShapeDtypeStruct((t, LANES), F32), jax.ShapeDtypeStruct((1, LANES), F32)],
        scratch_shapes=[pltpu.VMEM((1, LANES), F32)],
        compiler_params=_cp("arbitrary"),
        name="post",
    )(x, oa, ob, oc, wo, ln, wr, br, _const(np.tril(np.ones((tm, tm), np.float32), -1)))


MOE_TM = 256


def _moe_dest_kernel(sel_ref, gate_ref, rank_ref, off_ref, info_ref):
    on = sel_ref[...] > 0.0
    gates = gate_ref[...]
    dest = off_ref[...] + rank_ref[...]
    lane = lax.broadcasted_iota(jnp.int32, gates.shape, 1)
    la = jnp.min(jnp.where(on, lane, LANES), axis=1, keepdims=True)
    lb = jnp.max(jnp.where(on, lane, -1), axis=1, keepdims=True)
    pick = lambda v, l: jnp.sum(jnp.where(lane == l, v, 0.0), axis=1, keepdims=True)
    info_ref[...] = jnp.where(lane == 0, pick(dest, la),
                              jnp.where(lane == 1, pick(dest, lb),
                                        jnp.where(lane == 2, pick(gates, la),
                                                  jnp.where(lane == 3, pick(gates, lb), 0.0))))


def _moe_dest(sel, gates, rank, off, tm):
    t = sel.shape[0]
    row = lambda i: (i, 0)
    return pl.pallas_call(
        _moe_dest_kernel,
        grid=(t // tm,),
        in_specs=[pl.BlockSpec((tm, LANES), row)] * 3 + [pl.BlockSpec((1, LANES), lambda i: (0, 0))],
        out_specs=pl.BlockSpec((tm, LANES), row),
        out_shape=jax.ShapeDtypeStruct((t, LANES), F32),
        compiler_params=_cp("arbitrary"),
        name="moe_dest",
    )(sel, gates, rank, off)


def _row_copy(src_ref, src_row, dst_ref, dst_row, sem):
    return pltpu.make_async_copy(src_ref.at[pl.ds(src_row, 1), :], dst_ref.at[pl.ds(dst_row, 1), :], sem)


def _rows_wait(src_ref, dst_ref, nrows, sem):
    pltpu.make_async_copy(src_ref.at[pl.ds(0, nrows), :], dst_ref.at[pl.ds(0, nrows), :], sem).wait()


def _moe_dispatch_kernel(da_ref, db_ref, hf_ref, xs_in_ref, xs_ref, sem):
    del xs_in_ref
    tm = hf_ref.shape[0]

    def issue(r, carry):
        _row_copy(hf_ref, r, xs_ref, da_ref[0, 0, r], sem.at[0]).start(priority=0)
        _row_copy(hf_ref, r, xs_ref, db_ref[0, 0, r], sem.at[0]).start(priority=1)
        return carry

    lax.fori_loop(0, tm, issue, 0, unroll=4)
    _rows_wait(hf_ref, xs_ref, tm, sem.at[0])
    _rows_wait(hf_ref, xs_ref, tm, sem.at[0])


def _moe_dispatch(hf, da, db, rows_buf, tm):
    t = hf.shape[0]
    nrows = rows_buf.shape[0]
    idx = pl.BlockSpec((1, 1, tm), lambda i: (i, 0, 0), memory_space=pltpu.SMEM)
    return pl.pallas_call(
        _moe_dispatch_kernel,
        grid=(t // tm,),
        in_specs=[idx, idx, pl.BlockSpec((tm, D_MODEL), lambda i: (i, 0)),
                  pl.BlockSpec(memory_space=pl.ANY)],
        out_specs=pl.BlockSpec(memory_space=pl.ANY),
        out_shape=jax.ShapeDtypeStruct((nrows, D_MODEL), F32),
        scratch_shapes=[pltpu.SemaphoreType.DMA((1,))],
        input_output_aliases={3: 0},
        compiler_params=_cp("arbitrary"),
        name="moe_dispatch",
    )(da, db, hf, rows_buf)


def _moe_ffn_kernel(texp_ref, nval_ref, x_ref, wg_ref, wu_ref, wd_ref, y_ref):
    del texp_ref
    i = pl.program_id(0)

    @pl.when(i < nval_ref[0])
    def _():
        x = x_ref[...].astype(BF16)
        gp = _dot(x, wg_ref[0])
        up = _dot(x, wu_ref[0])
        y_ref[...] = _dot(((gp * _sigmoid(gp)) * up).astype(BF16), wd_ref[0])

    @pl.when(i >= nval_ref[0])
    def _():
        y_ref[...] = jnp.zeros_like(y_ref)


def _moe_ffn(xs, texp, nval, wg, wu, wd):
    nrows = xs.shape[0]
    tile = lambda i, te, nv: (jnp.minimum(i, nv[0] - 1), 0)
    wmap = lambda i, te, nv: (te[i], 0, 0)
    grid_spec = pltpu.PrefetchScalarGridSpec(
        num_scalar_prefetch=2,
        grid=(nrows // MOE_TM,),
        in_specs=[pl.BlockSpec((MOE_TM, D_MODEL), tile),
                  pl.BlockSpec((1, D_MODEL, D_EXPERT), wmap), pl.BlockSpec((1, D_MODEL, D_EXPERT), wmap),
                  pl.BlockSpec((1, D_EXPERT, D_MODEL), wmap)],
        out_specs=pl.BlockSpec((MOE_TM, D_MODEL), lambda i, te, nv: (i, 0)),
    )
    return pl.pallas_call(
        _moe_ffn_kernel,
        grid_spec=grid_spec,
        out_shape=jax.ShapeDtypeStruct((nrows, D_MODEL), F32),
        compiler_params=_cp("arbitrary"),
        name="moe_ffn",
    )(texp, nval, xs, wg, wu, wd)


def _moe_combine_kernel(da_ref, db_ref, dan_ref, dbn_ref, info_ref, x_ref, ys_ref, o_ref, buf_ref, sem):
    tm = x_ref.shape[0]
    i = pl.program_id(0)
    n = pl.num_programs(0)
    slot = i % 2

    def gather(ia_ref, ib_ref, s):
        def issue(r, carry):
            _row_copy(ys_ref, ia_ref[0, 0, r], buf_ref.at[s, 0], r, sem.at[s]).start(priority=0)
            _row_copy(ys_ref, ib_ref[0, 0, r], buf_ref.at[s, 1], r, sem.at[s]).start(priority=1)
            return carry

        lax.fori_loop(0, tm, issue, 0, unroll=4)

    @pl.when(i == 0)
    def _():
        gather(da_ref, db_ref, 0)

    @pl.when(i + 1 < n)
    def _():
        gather(dan_ref, dbn_ref, 1 - slot)

    _rows_wait(ys_ref, buf_ref.at[slot, 0], tm, sem.at[slot])
    _rows_wait(ys_ref, buf_ref.at[slot, 1], tm, sem.at[slot])
    info = info_ref[...]
    o_ref[...] = (x_ref[...] + info[:, 2:3] * buf_ref[slot, 0]) + info[:, 3:4] * buf_ref[slot, 1]


def _moe_combine(xn, info, da, db, ys, tm):
    t = xn.shape[0]
    n = t // tm
    idx = pl.BlockSpec((1, 1, tm), lambda i: (i, 0, 0), memory_space=pltpu.SMEM)
    nxt = pl.BlockSpec((1, 1, tm), lambda i: (jnp.minimum(i + 1, n - 1), 0, 0), memory_space=pltpu.SMEM)
    return pl.pallas_call(
        _moe_combine_kernel,
        grid=(n,),
        in_specs=[idx, idx, nxt, nxt, pl.BlockSpec((tm, LANES), lambda i: (i, 0)),
                  pl.BlockSpec((tm, D_MODEL), lambda i: (i, 0)), pl.BlockSpec(memory_space=pl.ANY)],
        out_specs=pl.BlockSpec((tm, D_MODEL), lambda i: (i, 0)),
        out_shape=jax.ShapeDtypeStruct((t, D_MODEL), F32),
        scratch_shapes=[pltpu.VMEM((2, 2, tm, D_MODEL), F32), pltpu.SemaphoreType.DMA((2,))],
        compiler_params=_cp("arbitrary"),
        name="moe_combine",
    )(da, db, da, db, info, xn, ys)


def _moe_rows(t):
    return -(-(2 * t + N_EXPERTS * (MOE_TM - 1)) // MOE_TM) * MOE_TM


def _moe(hf, gates, sel, rank, cnt, xn, wg, wu, wd, layer, rows_buf, tm):
    t = hf.shape[0]
    nrows = rows_buf.shape[0]
    ntile = nrows // MOE_TM
    cnt = cnt[0, :N_EXPERTS].astype(jnp.int32)
    padded = (cnt + (MOE_TM - 1)) // MOE_TM * MOE_TM
    end = jnp.cumsum(padded)
    off = jnp.zeros((1, LANES), F32).at[0, :N_EXPERTS].set((end - padded).astype(F32))
    nval = (end[-1] // MOE_TM).astype(jnp.int32)
    tile_start = jnp.arange(ntile, dtype=jnp.int32) * MOE_TM
    texp = jnp.sum((end[None, :] <= tile_start[:, None]).astype(jnp.int32), axis=1)
    texp = jnp.minimum(texp, N_EXPERTS - 1)
    texp = jnp.where(jnp.arange(ntile) < nval, texp, texp[jnp.maximum(nval - 1, 0)]) + layer * N_EXPERTS
    info = _moe_dest(sel, gates, rank, off, min(MOE_DEST_TILE, t))
    da = info[:, 0].astype(jnp.int32)
    db = info[:, 1].astype(jnp.int32)
    tmd = min(MOE_DISPATCH_TILE, t)
    xs = _moe_dispatch(hf, da.reshape(t // tmd, 1, tmd), db.reshape(t // tmd, 1, tmd), rows_buf, tmd)
    ys = _moe_ffn(xs, texp, nval.reshape(1), wg, wu, wd)
    return _moe_combine(xn, info, da.reshape(t // tm, 1, tm), db.reshape(t // tm, 1, tm), ys, tm), xs


def _pad_rows(a, nseq, dec):
    c = a.shape[-1]
    return jnp.pad(a.reshape(nseq, dec, c), ((0, 0), (0, SUB - dec), (0, 0))).reshape(nseq * SUB, c)


def _layer_params(l, ln_mix, w_in, hgrn_norm, fox_bf, fox_qnorm, fox_knorm, gdn_conv, gdn_a_log,
                  gdn_dt_bias, gdn_norm, w_out, ln_ffn, w_group, b_group, w_router, b_router,
                  w_gate, w_up, w_down):
    w = w_in[l]
    gates = jnp.concatenate([w[:, 2560:2568], w[:, 3592:3600]], axis=1)
    wp = jnp.concatenate([w[:, 0:2560], w[:, 2568:3592], gates,
                          jnp.zeros((D_MODEL, LANES - 16), F32)], axis=1).astype(BF16)
    p1 = jnp.zeros((1, LANES), F32).at[0, 0:8].set(fox_bf[l]).at[0, 8:12].set(gdn_dt_bias[l])
    p2 = jnp.zeros((1, LANES), F32).at[0, 8:12].set(gdn_a_log[l])
    wr = jnp.concatenate([w_router[l], w_group[l],
                          jnp.zeros((D_MODEL, LANES - N_GROUPS - N_EXPERTS), F32)], axis=1).astype(BF16)
    br = jnp.zeros((1, LANES), F32).at[0, 0:N_EXPERTS].set(b_router[l])
    br = br.at[0, N_EXPERTS:N_EXPERTS + N_GROUPS].set(b_group[l])
    return dict(
        ln_mix=ln_mix[l][None, :], w=wp, p1=p1, p2=p2,
        qg=jnp.tile(fox_qnorm[l], B_HEADS)[None, :], kg=jnp.tile(fox_knorm[l], B_HEADS)[None, :],
        hgrn_ng=jnp.tile(hgrn_norm[l], A_HEADS)[None, :], gdn_ng=jnp.tile(gdn_norm[l], C_HEADS)[None, :],
        conv=gdn_conv[l], w_out=w_out[l].astype(BF16), ln_ffn=ln_ffn[l][None, :], wr=wr, br=br)


def kernel(x_prompt, x_sample, cache_k, cache_v, cache_logf, page_table, state_hgrn, state_gdn, state_conv,
           ln_mix, w_in, hgrn_lb, hgrn_norm, fox_bf, fox_qnorm, fox_knorm, gdn_conv, gdn_a_log, gdn_dt_bias,
           gdn_norm, w_out, ln_ffn, w_group, b_group, w_router, b_router, w_gate, w_up, w_down):
    nb, seq, _ = x_prompt.shape
    nsq, dec, _ = x_sample.shape
    depth = ln_mix.shape[0]
    n_phys, page = cache_k.shape[1], cache_k.shape[2]
    tp = nb * seq
    ts = nsq * dec
    hs512 = _const(_head_sum_matrix(512))
    ck = jnp.transpose(cache_k, (0, 1, 3, 4, 2)).reshape(depth, n_phys, 512, page)
    cv = jnp.transpose(cache_v, (0, 1, 3, 4, 2)).reshape(depth, n_phys, 512, page)
    clt = jnp.swapaxes(cache_logf, 2, 3)
    tmp_ = min(ROW_TILE, tp)
    tms_ = min(ROW_TILE, ts)
    tbp = min(ROW_TILE, seq)
    gchunk = min(GDN_CHUNK, seq)
    seq_blk = SAMPLE_SEQS

    wg = w_gate.astype(BF16).reshape(depth * N_EXPERTS, D_MODEL, D_EXPERT)
    wu = w_up.astype(BF16).reshape(depth * N_EXPERTS, D_MODEL, D_EXPERT)
    wd = w_down.astype(BF16).reshape(depth * N_EXPERTS, D_EXPERT, D_MODEL)

    rows_p = jnp.zeros((_moe_rows(tp), D_MODEL), F32)
    rows_s = jnp.zeros((_moe_rows(ts), D_MODEL), F32)

    yp = x_prompt.reshape(tp, D_MODEL)
    ys = x_sample.reshape(ts, D_MODEL)
    outs_p, outs_s = [], []
    for l in range(depth):
        P = _layer_params(l, ln_mix, w_in, hgrn_norm, fox_bf, fox_qnorm, fox_knorm, gdn_conv, gdn_a_log,
                          gdn_dt_bias, gdn_norm, w_out, ln_ffn, w_group, b_group, w_router, b_router,
                          w_gate, w_up, w_down)

        a, _, bk, bv, c, g, qa, ka = _proj(yp, P["ln_mix"], P["w"], hs512, P["qg"], P["kg"], P["p1"],
                                           P["p2"], tm=tbp, bps=seq // tbp)
        ob = _fox(qa, ka, bv, nb, tq=tbp)
        oa, hst = _hgrn(a, hgrn_lb, P["hgrn_ng"], None, l, nb, seq // tbp, tbp, 1, tbp // SUB, SUB)
        oc, gst = _gdn(c, g, P["conv"], P["gdn_ng"], None, None, nb, seq // tbp, tbp, 1,
                       tbp // gchunk, gchunk)
        xn, hf, gates, sel, rank, cnt = _post(yp, oa, ob, oc, P["w_out"], P["ln_ffn"], P["wr"], P["br"],
                                              tm=tmp_)
        yp_new, rows_p = _moe(hf, gates, sel, rank, cnt, xn, wg, wu, wd, l, rows_p, tm=tmp_)
        outs_p.append((bk.reshape(nb, seq, B_HEADS, HD), bv.reshape(nb, seq, B_HEADS, HD),
                       g[:, 0:8].reshape(nb, seq, B_HEADS), hst, gst,
                       c.reshape(nb, seq, 1024)[:, seq - (CONV_W - 1):, 0:768]))
        yp = yp_new

        a, bq, bk, bv, c, g, _, _ = _proj(ys, P["ln_mix"], P["w"], hs512, P["qg"], P["kg"], P["p1"],
                                          P["p2"], tm=tms_, bps=1)
        knew = jnp.pad(jnp.swapaxes(bk.reshape(nsq, dec, 512), 1, 2), ((0, 0), (0, 0), (0, SUB - dec)))
        vnew = jnp.pad(jnp.swapaxes(bv.reshape(nsq, dec, 512), 1, 2), ((0, 0), (0, 0), (0, SUB - dec)))
        lfn = jnp.pad(jnp.swapaxes(g[:, 0:8].reshape(nsq, dec, B_HEADS), 1, 2),
                      ((0, 0), (0, 0), (0, SUB - dec)))
        ob = _foxdec(l, page_table, bq.reshape(nsq, dec, 512), knew, vnew, lfn, ck, cv, clt,
                     pp=min(DECODE_PAGES, page_table.shape[1])).reshape(ts, 512)
        n_outer = nsq // seq_blk
        oa, hst = _hgrn(_pad_rows(a, nsq, dec), hgrn_lb, P["hgrn_ng"], state_hgrn[l], l,
                        n_outer, 1, seq_blk * SUB, seq_blk, 1, dec)
        cinit = jnp.pad(state_conv[l], ((0, 0), (8 - (CONV_W - 1), 0), (0, 0)))
        oc, gst = _gdn(_pad_rows(c, nsq, dec), _pad_rows(g, nsq, dec), P["conv"], P["gdn_ng"], cinit,
                       state_gdn[l], n_outer, 1, seq_blk * SUB, seq_blk, 1, SUB)
        oa = oa.reshape(nsq, SUB, 256)[:, :dec].reshape(ts, 256)
        oc = oc.reshape(nsq, SUB, 256)[:, :dec].reshape(ts, 256)
        xn, hf, gates, sel, rank, cnt = _post(ys, oa, ob, oc, P["w_out"], P["ln_ffn"], P["wr"], P["br"],
                                              tm=tms_)
        ys_new, rows_s = _moe(hf, gates, sel, rank, cnt, xn, wg, wu, wd, l, rows_s, tm=tms_)
        xpad = jnp.concatenate([state_conv[l], c.reshape(nsq, dec, 1024)[:, :, 0:768]], axis=1)
        outs_s.append((bk.reshape(nsq, dec, B_HEADS, HD), bv.reshape(nsq, dec, B_HEADS, HD),
                       g[:, 0:8].reshape(nsq, dec, B_HEADS), hst, gst, xpad[:, -(CONV_W - 1):, :]))
        ys = ys_new

    stack = lambda outs, j: jnp.stack([o[j] for o in outs], axis=0)
    return (yp.reshape(nb, seq, D_MODEL), ys.reshape(nsq, dec, D_MODEL),
            *[stack(outs_p, j) for j in range(6)], *[stack(outs_s, j) for j in range(6)])
```
